```python
import jax, jax.numpy as jnp
from jax import lax
import numpy as np

D_MODEL = 1024
BATCH = 4
SEQ = 8192
DEPTH = 2
DEC_BATCH = 32
DEC_SEQ = 4
PAST_LEN = 16384
PAGE_SIZE = 128

N_EVEN = (DEPTH + 1) // 2
N_ODD = DEPTH // 2
W_CONV = D_MODEL // 2
CONV_WIDTH = 31
N_HEADS = 8
HEAD_DIM = 64
KV_HEADS = 2
GROUP = N_HEADS // KV_HEADS
W_ATTN = N_HEADS * HEAD_DIM
KV_W = 2 * KV_HEADS * HEAD_DIM
CMP_STRIDE = 16
CMP_BLOCK = 2 * CMP_STRIDE
SEL_BLOCK = 64
N_SEL = 16
WINDOW = 512
Q_BLOCK = 128
W_POOL = D_MODEL
POOL_WINDOWS = (2, 4, 8, 16)
N_POOL_GROUPS = 4
POOL_GROUP_W = W_POOL // N_POOL_GROUPS
POOL_MAX = 16
LN_EPS = 1e-5
NEG_INF = -1e30
SEL_FORCE = 1e9
DEEPNORM_ALPHA = (2 * DEPTH) ** 0.25
DEEPNORM_BETA = (8 * DEPTH) ** -0.25
EVEN_SPLITS = (W_CONV, W_CONV, W_CONV, W_ATTN, KV_W, KV_W, KV_W, 3 * N_HEADS, W_ATTN)
E_IN = 3 * W_CONV + 2 * W_ATTN + 3 * KV_W + 3 * N_HEADS

kernel_name = 'hybrid_conv_nsa_pool_decoder_step'


def layer_norm(x, g, b):
    xf = x.astype(jnp.float32)
    mu = xf.mean(-1, keepdims=True)
    var = jnp.square(xf - mu).mean(-1, keepdims=True)
    y = (xf - mu) * lax.rsqrt(var + LN_EPS) * g.astype(jnp.float32) + b.astype(jnp.float32)
    return y.astype(x.dtype)


def masked_softmax(s, mask):
    s = jnp.where(mask, s.astype(jnp.float32), NEG_INF)
    return jax.nn.softmax(s, axis=-1) * mask


def even_project(x, w_in):
    B, T, _ = x.shape
    h = jnp.einsum('btd,de->bte', x, w_in)
    offsets = np.cumsum(EVEN_SPLITS)[:-1].tolist()
    a_val, a_glu, a_gate, q, kv_c, kv_s, kv_w, g, b_gate = jnp.split(h, offsets, axis=-1)
    kv_shape = (B, T, 2, KV_HEADS, HEAD_DIM)
    u = a_val * jax.nn.sigmoid(a_glu)
    return (u, a_gate, q.reshape(B, T, N_HEADS, HEAD_DIM), kv_c.reshape(kv_shape),
            kv_s.reshape(kv_shape), kv_w.reshape(kv_shape), g.reshape(B, T, N_HEADS, 3), b_gate)


def conv_branch(u_ext, a_gate, conv_w, conv_b, ln_g, ln_b):
    y = lax.conv_general_dilated(u_ext, conv_w[:, None, :], (1,), 'VALID',
                                 dimension_numbers=('NWC', 'WIO', 'NWC'),
                                 feature_group_count=W_CONV) + conv_b
    return jax.nn.silu(layer_norm(y, ln_g, ln_b)) * jax.nn.silu(a_gate)


def compress_kv(kv, w_cmp):
    B, L = kv.shape[:2]
    n_ch = L // CMP_STRIDE
    ch = kv[:, :n_ch * CMP_STRIDE].reshape(B, n_ch, CMP_STRIDE, 2, KV_HEADS, HEAD_DIM)
    kc = (jnp.einsum('bnjchd,jchd->bnchd', ch[:, :-1], w_cmp[:CMP_STRIDE])
          + jnp.einsum('bnjchd,jchd->bnchd', ch[:, 1:], w_cmp[CMP_STRIDE:]))
    kc_end = jnp.arange(n_ch - 1) * CMP_STRIDE + (CMP_BLOCK - 1)
    return kc[:, :, 0], kc[:, :, 1], kc_end


def select_blocks(kv):
    B, L = kv.shape[:2]
    ns = -(-L // SEL_BLOCK)
    kv = jnp.pad(kv, ((0, 0), (0, ns * SEL_BLOCK - L), (0, 0), (0, 0), (0, 0)))
    kv = kv.reshape(B, ns, SEL_BLOCK, 2, KV_HEADS, HEAD_DIM).transpose(3, 0, 4, 1, 2, 5)
    return kv[0], kv[1]


def nsa_block(q, g, q_pos, kc, vc, kc_end, ks, vs, kw, vw, kw_pos):
    B, Q = q.shape[:2]
    qg = q.reshape(B, Q, KV_HEADS, GROUP, HEAD_DIM) * (HEAD_DIM ** -0.5)
    s_c = jnp.einsum('bqhgd,bnhd->bhgqn', qg, kc)
    p_c = masked_softmax(s_c, kc_end[None, :] <= q_pos[:, None])
    o_c = jnp.einsum('bhgqn,bnhd->bqhgd', p_c.astype(vc.dtype), vc)
    ns, nc = ks.shape[2], kc.shape[1]
    per = SEL_BLOCK // CMP_STRIDE
    p_grp = jnp.pad(p_c.sum(2), ((0, 0), (0, 0), (0, 0), (0, per * ns - nc)))
    score = p_grp.reshape(B, KV_HEADS, Q, ns, per)[..., :per - 1].sum(-1)
    blk = jnp.arange(ns)[None, :]
    cur = (q_pos // SEL_BLOCK)[:, None]
    forced = (blk == 0) | ((blk >= cur - 1) & (blk <= cur))
    score = jnp.where(forced, SEL_FORCE, jnp.where(blk <= cur, score, -SEL_FORCE))
    n_top = min(N_SEL, ns)
    _, idx = lax.top_k(score, n_top)
    bi = jnp.arange(B)[:, None, None, None]
    hi = jnp.arange(KV_HEADS)[None, :, None, None]
    n_keys = n_top * SEL_BLOCK
    k_sel = ks[bi, hi, idx].reshape(B, KV_HEADS, Q, n_keys, HEAD_DIM)
    v_sel = vs[bi, hi, idx].reshape(B, KV_HEADS, Q, n_keys, HEAD_DIM)
    pos_sel = (idx[..., None] * SEL_BLOCK + jnp.arange(SEL_BLOCK)).reshape(B, KV_HEADS, Q, n_keys)
    s_s = jnp.einsum('bqhgd,bhqnd->bhgqn', qg, k_sel)
    p_s = masked_softmax(s_s, (pos_sel <= q_pos[:, None])[:, :, None])
    o_s = jnp.einsum('bhgqn,bhqnd->bqhgd', p_s.astype(v_sel.dtype), v_sel)
    rel = q_pos[:, None] - kw_pos[None, :]
    s_w = jnp.einsum('bqhgd,bwhd->bhgqw', qg, kw)
    p_w = masked_softmax(s_w, (rel >= 0) & (rel < WINDOW) & (kw_pos[None, :] >= 0))
    o_w = jnp.einsum('bhgqw,bwhd->bqhgd', p_w.astype(vw.dtype), vw)
    gq = jax.nn.sigmoid(g.reshape(B, Q, KV_HEADS, GROUP, 3).astype(jnp.float32)).astype(q.dtype)
    o = gq[..., 0:1] * o_c + gq[..., 1:2] * o_s + gq[..., 2:3] * o_w
    return o.reshape(B, Q, W_ATTN)


def mixer_out(a_out, b_out, b_gate, w_out):
    h = jnp.concatenate([a_out, b_out * jax.nn.silu(b_gate)], axis=-1)
    return jnp.einsum('bte,ed->btd', h, w_out)


def even_prompt(x, w_in, w_cmp, conv_w, conv_b, ln_g, ln_b, w_out):
    B, T, _ = x.shape
    u, a_gate, q, kv_c, kv_s, kv_w, g, b_gate = even_project(x, w_in)
    u_ext = jnp.pad(u, ((0, 0), (CONV_WIDTH - 1, 0), (0, 0)))
    a_out = conv_branch(u_ext, a_gate, conv_w, conv_b, ln_g, ln_b)
    kc, vc, kc_end = compress_kv(kv_c, w_cmp)
    ks, vs = select_blocks(kv_s)
    kw_pad = jnp.pad(kv_w, ((0, 0), (WINDOW, 0), (0, 0), (0, 0), (0, 0)))
    n_qb = T // Q_BLOCK
    q_b = jnp.swapaxes(q.reshape(B, n_qb, Q_BLOCK, N_HEADS, HEAD_DIM), 0, 1)
    g_b = jnp.swapaxes(g.reshape(B, n_qb, Q_BLOCK, N_HEADS, 3), 0, 1)

    def query_block(args):
        i, q_i, g_i = args
        start = i * Q_BLOCK
        kw_i = lax.dynamic_slice_in_dim(kw_pad, start, WINDOW + Q_BLOCK, axis=1)
        q_pos = start + jnp.arange(Q_BLOCK)
        kw_pos = start - WINDOW + jnp.arange(WINDOW + Q_BLOCK)
        return nsa_block(q_i, g_i, q_pos, kc, vc, kc_end, ks, vs, kw_i[:, :, 0], kw_i[:, :, 1], kw_pos)

    b_out = lax.map(query_block, (jnp.arange(n_qb), q_b, g_b))
    b_out = jnp.swapaxes(b_out, 0, 1).reshape(B, T, W_ATTN)
    y = mixer_out(a_out, b_out, b_gate, w_out)
    return y, kv_c, kv_s, kv_w[:, -min(WINDOW, T):], u_ext[:, -(CONV_WIDTH - 1):]


def even_sample(x, cache_c, cache_s, win_buf, conv_buf, page_table, w_in, w_cmp, conv_w, conv_b, ln_g, ln_b, w_out):
    B, T, _ = x.shape
    past_len = page_table.shape[1] * cache_c.shape[1]
    u, a_gate, q, kv_c, kv_s, kv_w, g, b_gate = even_project(x, w_in)
    u_ext = jnp.concatenate([conv_buf.astype(u.dtype), u], axis=1)
    a_out = conv_branch(u_ext, a_gate, conv_w, conv_b, ln_g, ln_b)

    def paged_rows(cache, new):
        past = cache[page_table].reshape((B, past_len) + cache.shape[2:])
        return jnp.concatenate([past.astype(new.dtype), new], axis=1)

    kc, vc, kc_end = compress_kv(paged_rows(cache_c, kv_c), w_cmp)
    ks, vs = select_blocks(paged_rows(cache_s, kv_s))
    win_ext = jnp.concatenate([win_buf.astype(kv_w.dtype), kv_w], axis=1)
    w_len = win_ext.shape[1]
    q_pos = past_len + jnp.arange(T)
    kw_pos = past_len + T - w_len + jnp.arange(w_len)
    b_out = nsa_block(q, g, q_pos, kc, vc, kc_end, ks, vs, win_ext[:, :, 0], win_ext[:, :, 1], kw_pos)
    y = mixer_out(a_out, b_out, b_gate, w_out)
    return y, kv_c, kv_s, win_ext[:, -min(WINDOW, w_len):], u_ext[:, -(CONV_WIDTH - 1):]


def pool_mixer(x, buf, pos0, w_in, w_grp, scale, w_out):
    B, T, _ = x.shape
    v, gate = jnp.split(jnp.einsum('btd,de->bte', x, w_in), 2, axis=-1)
    P = POOL_MAX - 1
    ext = jnp.concatenate([buf.astype(v.dtype), v], axis=1)
    cs = jnp.pad(jnp.cumsum(ext.astype(jnp.float32), axis=1), ((0, 0), (1, 0), (0, 0)))
    pos = pos0 + jnp.arange(T)
    means = []
    for gi, w in enumerate(POOL_WINDOWS):
        c0, c1 = gi * POOL_GROUP_W, (gi + 1) * POOL_GROUP_W
        win_sum = cs[:, P + 1:P + 1 + T, c0:c1] - cs[:, P + 1 - w:P + 1 - w + T, c0:c1]
        cnt = jnp.minimum(pos + 1, w).astype(jnp.float32)[None, :, None]
        means.append(win_sum / cnt)
    mean = jnp.stack(means, axis=2)
    d = (mean - v.reshape(B, T, N_POOL_GROUPS, POOL_GROUP_W).astype(jnp.float32)).astype(v.dtype)
    mixed = jnp.einsum('btgc,gce->btge', d, w_grp).reshape(B, T, W_POOL) * scale
    y = jnp.einsum('bte,ed->btd', mixed * jax.nn.silu(gate), w_out)
    return y, ext[:, -P:]


def setup_inputs(seed: int = 0) -> dict:
    key = jax.random.key(seed)
    ks = jax.random.split(key, 24)
    n_pages = PAST_LEN // PAGE_SIZE
    n_phys = (DEC_BATCH * n_pages * 5) // 4
    win_buf = min(WINDOW, PAST_LEN)
    nrm = jax.random.normal
    f32 = jnp.float32
    page_table = jax.random.permutation(ks[0], n_phys)[:DEC_BATCH * n_pages].reshape(DEC_BATCH, n_pages).astype(jnp.int32)
    cache_shape = (N_EVEN, n_phys, PAGE_SIZE, 2, KV_HEADS, HEAD_DIM)
    return {
        'x_prompt': nrm(ks[1], (BATCH, SEQ, D_MODEL), f32),
        'x_sample': nrm(ks[2], (DEC_BATCH, DEC_SEQ, D_MODEL), f32),
        'cache_kv_cmp': nrm(ks[3], cache_shape, f32),
        'cache_kv_sel': nrm(ks[4], cache_shape, f32),
        'state_win_kv': nrm(ks[5], (N_EVEN, DEC_BATCH, win_buf, 2, KV_HEADS, HEAD_DIM), f32),
        'state_conv': 0.5 * nrm(ks[6], (N_EVEN, DEC_BATCH, CONV_WIDTH - 1, W_CONV), f32),
        'state_pool': nrm(ks[7], (N_ODD, DEC_BATCH, POOL_MAX - 1, W_POOL), f32),
        'page_table': page_table,
        'w_in_even': nrm(ks[8], (N_EVEN, D_MODEL, E_IN), f32) * D_MODEL ** -0.5,
        'w_cmp': (1.0 + 0.1 * nrm(ks[9], (N_EVEN, CMP_BLOCK, 2, KV_HEADS, HEAD_DIM), f32)) / CMP_BLOCK,
        'conv_w': nrm(ks[10], (N_EVEN, CONV_WIDTH, W_CONV), f32) * CONV_WIDTH ** -0.5,
        'conv_b': 0.02 * nrm(ks[11], (N_EVEN, W_CONV), f32),
        'conv_ln_g': 1.0 + 0.05 * nrm(ks[12], (N_EVEN, W_CONV), f32),
        'conv_ln_b': 0.02 * nrm(ks[13], (N_EVEN, W_CONV), f32),
        'w_out_even': nrm(ks[14], (N_EVEN, W_CONV + W_ATTN, D_MODEL), f32) * (W_CONV + W_ATTN) ** -0.5 * DEEPNORM_BETA,
        'w_in_odd': nrm(ks[15], (N_ODD, D_MODEL, 2 * W_POOL), f32) * D_MODEL ** -0.5,
        'w_pool_grp': nrm(ks[16], (N_ODD, N_POOL_GROUPS, POOL_GROUP_W, POOL_GROUP_W), f32) * POOL_GROUP_W ** -0.5,
        'pool_scale': 1.0 + 0.05 * nrm(ks[17], (N_ODD, W_POOL), f32),
        'w_out_odd': nrm(ks[18], (N_ODD, W_POOL, D_MODEL), f32) * W_POOL ** -0.5 * DEEPNORM_BETA,
        'ln_g': 1.0 + 0.05 * nrm(ks[19], (DEPTH, D_MODEL), f32),
        'ln_b': 0.02 * nrm(ks[20], (DEPTH, D_MODEL), f32),
    }


def reference(x_prompt, x_sample, cache_kv_cmp, cache_kv_sel, state_win_kv, state_conv, state_pool, page_table,
              w_in_even, w_cmp, conv_w, conv_b, conv_ln_g, conv_ln_b, w_out_even,
              w_in_odd, w_pool_grp, pool_scale, w_out_odd, ln_g, ln_b):
    xp, xs = x_prompt, x_sample
    past_len = page_table.shape[1] * cache_kv_cmp.shape[2]
    kvc_p, kvs_p, win_p, conv_p, pool_p = [], [], [], [], []
    kvc_s, kvs_s, win_s, conv_s, pool_s = [], [], [], [], []
    for layer in range(DEPTH):
        j = layer // 2
        if layer % 2 == 0:
            dp, a1, a2, a3, a4 = even_prompt(xp, w_in_even[j], w_cmp[j], conv_w[j], conv_b[j],
                                             conv_ln_g[j], conv_ln_b[j], w_out_even[j])
            ds, b1, b2, b3, b4 = even_sample(xs, cache_kv_cmp[j], cache_kv_sel[j], state_win_kv[j], state_conv[j],
                                             page_table, w_in_even[j], w_cmp[j], conv_w[j], conv_b[j],
                                             conv_ln_g[j], conv_ln_b[j], w_out_even[j])
            kvc_p.append(a1); kvs_p.append(a2); win_p.append(a3); conv_p.append(a4)
            kvc_s.append(b1); kvs_s.append(b2); win_s.append(b3); conv_s.append(b4)
        else:
            buf0 = jnp.zeros((xp.shape[0], POOL_MAX - 1, W_POOL), xp.dtype)
            dp, a5 = pool_mixer(xp, buf0, 0, w_in_odd[j], w_pool_grp[j], pool_scale[j], w_out_odd[j])
            ds, b5 = pool_mixer(xs, state_pool[j], past_len, w_in_odd[j], w_pool_grp[j], pool_scale[j], w_out_odd[j])
            pool_p.append(a5); pool_s.append(b5)
        xp = layer_norm(DEEPNORM_ALPHA * xp + dp, ln_g[layer], ln_b[layer])
        xs = layer_norm(DEEPNORM_ALPHA * xs + ds, ln_g[layer], ln_b[layer])
    return (xp, xs,
            jnp.stack(kvc_p), jnp.stack(kvs_p), jnp.stack(win_p), jnp.stack(conv_p), jnp.stack(pool_p),
            jnp.stack(kvc_s), jnp.stack(kvs_s), jnp.stack(win_s), jnp.stack(conv_s), jnp.stack(pool_s))
```

```python
import functools

import jax
import jax.numpy as jnp
import numpy as np
from jax import lax
from jax.experimental import pallas as pl
from jax.experimental.pallas import tpu as pltpu

F32 = jnp.float32
BF16 = jnp.bfloat16

D_MODEL = 1024
W_CONV = 512
CONV_WIDTH = 31
N_HEADS = 8
HEAD_DIM = 64
KV_HEADS = 2
GROUP = N_HEADS // KV_HEADS
W_ATTN = N_HEADS * HEAD_DIM
KV_W = 2 * KV_HEADS * HEAD_DIM
CMP_STRIDE = 16
CMP_BLOCK = 2 * CMP_STRIDE
SEL_BLOCK = 64
N_SEL = 16
WINDOW = 512
W_POOL = 1024
POOL_WINDOWS = (2, 4, 8, 16)
POOL_GROUP_W = W_POOL // len(POOL_WINDOWS)
POOL_MAX = 16
LN_EPS = 1e-5
NEG_INF = -1e30
SEL_FORCE = 1e9
DEPTH = 2
DEEPNORM_ALPHA = (2 * DEPTH) ** 0.25
EVEN_SPLITS = (W_CONV, W_CONV, W_CONV, W_ATTN, KV_W, KV_W, KV_W, 3 * N_HEADS, W_ATTN)

LANES = 128
E_PAD = 3 * W_CONV + 2 * W_ATTN + 3 * KV_W + KV_HEADS * LANES
VMEM_LIMIT = 56 * 1024 * 1024
Q_TILE = 128
K_TILE = 512
PER_SEL = SEL_BLOCK // CMP_STRIDE
LOWEST = -3.0e38


def _cparams(sem):
    return pltpu.CompilerParams(dimension_semantics=sem, vmem_limit_bytes=VMEM_LIMIT)


def _sigmoid(x):
    return 1.0 / (1.0 + jnp.exp(-x))


def _silu(x):
    return x * _sigmoid(x)


def _layer_norm(z, g, b):
    mu = jnp.mean(z, axis=-1, keepdims=True)
    zc = z - mu
    var = jnp.mean(zc * zc, axis=-1, keepdims=True)
    return zc * lax.rsqrt(var + LN_EPS) * g + b


def _dot_t(a, b):
    return lax.dot_general(a, b, (((1,), (1,)), ((), ())), preferred_element_type=F32)


def _dot(a, b):
    return jnp.dot(a, b, preferred_element_type=F32)


def _split3(x):
    hi = x.astype(BF16)
    r1 = x - hi.astype(F32)
    mid = r1.astype(BF16)
    lo = (r1 - mid.astype(F32)).astype(BF16)
    return hi, mid, lo


def _proj_even_kernel(x_ref, w_ref, u_ref, sg_ref, q_ref, kvc_ref, kvs_ref, kvw_ref, bg_ref, gt_ref):
    xb = x_ref[...].astype(BF16)

    def mm(lo, hi):
        return _dot(xb, w_ref[:, lo:hi])

    o = 0
    a_val = mm(o, o + W_CONV); o += W_CONV
    a_glu = mm(o, o + W_CONV); o += W_CONV
    u_ref[...] = a_val * _sigmoid(a_glu)
    sg_ref[...] = _silu(mm(o, o + W_CONV)); o += W_CONV
    q_ref[...] = (mm(o, o + W_ATTN) * (HEAD_DIM ** -0.5)).astype(BF16); o += W_ATTN
    kvc_ref[...] = mm(o, o + KV_W); o += KV_W
    kvs_ref[...] = mm(o, o + KV_W); o += KV_W
    kvw_ref[...] = mm(o, o + KV_W); o += KV_W
    bg_ref[...] = _silu(mm(o, o + W_ATTN)); o += W_ATTN
    gt_ref[...] = _sigmoid(mm(o, o + KV_HEADS * LANES))


def _proj_even(x2d, w_pad, tm):
    n = x2d.shape[0]
    row = lambda w: pl.BlockSpec((tm, w), lambda i: (i, 0))
    return pl.pallas_call(
        _proj_even_kernel,
        grid=(n // tm,),
        in_specs=[row(D_MODEL), pl.BlockSpec((D_MODEL, E_PAD), lambda i: (0, 0))],
        out_specs=[row(W_CONV), row(W_CONV), row(W_ATTN), row(KV_W), row(KV_W), row(KV_W), row(W_ATTN),
                   row(KV_HEADS * LANES)],
        out_shape=[jax.ShapeDtypeStruct((n, W_CONV), F32), jax.ShapeDtypeStruct((n, W_CONV), F32),
                   jax.ShapeDtypeStruct((n, W_ATTN), BF16), jax.ShapeDtypeStruct((n, KV_W), F32),
                   jax.ShapeDtypeStruct((n, KV_W), F32), jax.ShapeDtypeStruct((n, KV_W), F32),
                   jax.ShapeDtypeStruct((n, W_ATTN), F32), jax.ShapeDtypeStruct((n, KV_HEADS * LANES), F32)],
        compiler_params=_cparams(("parallel",)),
        name="proj_even",
    )(x2d, w_pad)


def _pad_even_weights(w):
    offs = np.cumsum(EVEN_SPLITS)[:-1].tolist()
    a_val, a_glu, a_gate, wq, wkc, wks, wkw, wg, wbg = jnp.split(w, offs, axis=1)
    wg = wg.reshape(D_MODEL, KV_HEADS, GROUP, 3).transpose(0, 1, 3, 2).reshape(D_MODEL, KV_HEADS, 3 * GROUP)
    wg = jnp.pad(wg, ((0, 0), (0, 0), (0, LANES - 3 * GROUP))).reshape(D_MODEL, KV_HEADS * LANES)
    return jnp.concatenate([a_val, a_glu, a_gate, wq, wkc, wks, wkw, wbg, wg], axis=1).astype(BF16)


CONV_HALO = 32
CONV_CHUNK = 32


def _conv_prompt_kernel(u_ref, sg_ref, w_ref, cb_ref, g_ref, b_ref, o_ref, ext_ref, *, tt):
    @pl.when(pl.program_id(1) == 0)
    def _():
        ext_ref[0:CONV_HALO, :] = jnp.zeros((CONV_HALO, W_CONV), F32)

    ext_ref[CONV_HALO:CONV_HALO + tt, :] = u_ref[...]
    base = CONV_HALO - (CONV_WIDTH - 1)
    for c in range(tt // CONV_CHUNK):
        r0 = c * CONV_CHUNK
        acc = jnp.zeros((CONV_CHUNK, W_CONV), F32) + cb_ref[...]
        for k in range(CONV_WIDTH):
            acc = acc + w_ref[k:k + 1, :] * ext_ref[base + r0 + k:base + r0 + k + CONV_CHUNK, :]
        y = _layer_norm(acc, g_ref[...], b_ref[...])
        o_ref[r0:r0 + CONV_CHUNK, :] = _silu(y) * sg_ref[r0:r0 + CONV_CHUNK, :]
    ext_ref[0:CONV_HALO, :] = ext_ref[tt:tt + CONV_HALO, :]


def _conv_prompt(u2d, sg2d, conv_w, conv_b, ln_g, ln_b, batch, seq, tt):
    nt = seq // tt
    row = pl.BlockSpec((tt, W_CONV), lambda b, i: (b * nt + i, 0))
    vec = pl.BlockSpec((1, W_CONV), lambda b, i: (0, 0))
    return pl.pallas_call(
        functools.partial(_conv_prompt_kernel, tt=tt),
        grid=(batch, nt),
        in_specs=[row, row, pl.BlockSpec((CONV_WIDTH, W_CONV), lambda b, i: (0, 0)), vec, vec, vec],
        out_specs=row,
        out_shape=jax.ShapeDtypeStruct((batch * seq, W_CONV), F32),
        scratch_shapes=[pltpu.VMEM((tt + CONV_HALO, W_CONV), F32)],
        compiler_params=_cparams(("arbitrary", "arbitrary")),
        name="conv_prompt",
    )(u2d, sg2d, conv_w, conv_b, ln_g, ln_b)


def _conv_sample_kernel(ext_ref, sg_ref, w_ref, cb_ref, g_ref, b_ref, o_ref, *, ts):
    for t in range(ts):
        acc = jnp.zeros(ext_ref.shape[1:], F32) + cb_ref[...]
        for k in range(CONV_WIDTH):
            acc = acc + w_ref[k:k + 1, :] * ext_ref[t + k]
        y = _layer_norm(acc, g_ref[...], b_ref[...])
        o_ref[t] = _silu(y) * sg_ref[t]


def _conv_sample(ext, sg, conv_w, conv_b, ln_g, ln_b):
    ts, bsz, _ = sg.shape
    return pl.pallas_call(
        functools.partial(_conv_sample_kernel, ts=ts),
        out_shape=jax.ShapeDtypeStruct((ts, bsz, W_CONV), F32),
        name="conv_sample",
    )(ext, sg, conv_w, conv_b, ln_g, ln_b)


def _compress_rows(x_main, x_halo, w_ref, n_chunks):
    x3 = x_main.reshape(n_chunks, CMP_STRIDE, KV_W)
    xs = jnp.concatenate([x3[1:], x_halo.reshape(1, CMP_STRIDE, KV_W)], axis=0)
    w1 = w_ref[0:CMP_STRIDE, :]
    w2 = w_ref[CMP_STRIDE:CMP_BLOCK, :]
    return jnp.sum(x3 * w1[None], axis=1) + jnp.sum(xs * w2[None], axis=1)


def _compress_prompt_kernel(x_ref, h_ref, w_ref, o_ref, *, n_chunks, chunks_per_seq):
    kc = _compress_rows(x_ref[...], h_ref[...], w_ref, n_chunks)
    n = pl.program_id(0) * n_chunks + lax.broadcasted_iota(jnp.int32, (n_chunks, 1), 0)
    o_ref[...] = jnp.where(n % chunks_per_seq == chunks_per_seq - 1, 0.0, kc)


def _compress_prompt(kvc2d, w_cmp2d, seq, rows):
    n = kvc2d.shape[0]
    n_chunks = rows // CMP_STRIDE
    n_halo_blocks = n // CMP_STRIDE
    return pl.pallas_call(
        functools.partial(_compress_prompt_kernel, n_chunks=n_chunks, chunks_per_seq=seq // CMP_STRIDE),
        grid=(n // rows,),
        in_specs=[pl.BlockSpec((rows, KV_W), lambda i: (i, 0)),
                  pl.BlockSpec((CMP_STRIDE, KV_W), lambda i: (jnp.minimum((i + 1) * n_chunks, n_halo_blocks - 1), 0)),
                  pl.BlockSpec((CMP_BLOCK, KV_W), lambda i: (0, 0))],
        out_specs=pl.BlockSpec((n_chunks, KV_W), lambda i: (i, 0)),
        out_shape=jax.ShapeDtypeStruct((n // CMP_STRIDE, KV_W), F32),
        compiler_params=_cparams(("parallel",)),
        name="compress_prompt",
    )(kvc2d, kvc2d, w_cmp2d)


PAGES_PER_STEP = 8


def _compress_paged_kernel(pt_ref, *refs, page, n_steps):
    del pt_ref
    pages = refs[:PAGES_PER_STEP]
    halo_ref, w_ref, o_ref = refs[PAGES_PER_STEP:]
    cpp = page // CMP_STRIDE
    for k in range(PAGES_PER_STEP):
        nxt = pages[k + 1][0, 0:CMP_STRIDE, :] if k + 1 < PAGES_PER_STEP else halo_ref[0]
        kc = _compress_rows(pages[k][0], nxt, w_ref, cpp)
        if k == PAGES_PER_STEP - 1:
            last = (pl.program_id(1) == n_steps - 1)
            n = lax.broadcasted_iota(jnp.int32, (cpp, 1), 0)
            kc = jnp.where(jnp.logical_and(last, n == cpp - 1), 0.0, kc)
        o_ref[0, k * cpp:(k + 1) * cpp, :] = kc


def _compress_paged(cache3d, page_table, w_cmp2d):
    n_phys, page, _ = cache3d.shape
    bsz, n_pages = page_table.shape
    n_steps = n_pages // PAGES_PER_STEP
    cpp = page // CMP_STRIDE
    halo_view = cache3d.reshape(n_phys * cpp, CMP_STRIDE, KV_W)

    def page_spec(k):
        return pl.BlockSpec((1, page, KV_W), lambda b, s, pt: (pt[b, s * PAGES_PER_STEP + k], 0, 0))

    halo_spec = pl.BlockSpec(
        (1, CMP_STRIDE, KV_W),
        lambda b, s, pt: (pt[b, jnp.minimum((s + 1) * PAGES_PER_STEP, n_pages - 1)] * cpp, 0, 0))
    grid_spec = pltpu.PrefetchScalarGridSpec(
        num_scalar_prefetch=1,
        grid=(bsz, n_steps),
        in_specs=[page_spec(k) for k in range(PAGES_PER_STEP)] + [
            halo_spec, pl.BlockSpec((CMP_BLOCK, KV_W), lambda b, s, pt: (0, 0))],
        out_specs=pl.BlockSpec((1, PAGES_PER_STEP * cpp, KV_W), lambda b, s, pt: (b, s, 0)),
    )
    return pl.pallas_call(
        functools.partial(_compress_paged_kernel, page=page, n_steps=n_steps),
        grid_spec=grid_spec,
        out_shape=jax.ShapeDtypeStruct((bsz, n_pages * cpp, KV_W), F32),
        compiler_params=_cparams(("parallel", "arbitrary")),
        name="compress_paged",
    )(page_table, *([cache3d] * PAGES_PER_STEP), halo_view, w_cmp2d)


def _top_blocks_mask(sc, blk, n_pick):
    n_blocks = sc.shape[0]
    sel = jnp.zeros(sc.shape, F32)
    for _ in range(n_pick):
        mx = jnp.max(sc, axis=0, keepdims=True)
        jm = jnp.min(jnp.where(sc == mx, blk, n_blocks), axis=0, keepdims=True)
        pick = blk == jm
        sel = jnp.where(pick, 1.0, sel)
        sc = jnp.where(pick, LOWEST, sc)
    return sel


def _force_scores(score, blk, cur):
    forced = (blk == 0) | ((blk >= cur - 1) & (blk <= cur))
    return jnp.where(forced, SEL_FORCE, jnp.where(blk <= cur, score, -SEL_FORCE))


def _nsa_prompt_kernel(q_ref, kaug_ref, vaug_ref, kw_ref, vw_ref, kc_ref, vc_ref, gt_ref, m01_ref, o_ref,
                       qa_ref, acc_ref, m_ref, *, n_cmp):
    i = pl.program_id(2)
    rows = GROUP * Q_TILE
    q = q_ref[...]
    qs = jnp.concatenate([q[:, g * HEAD_DIM:(g + 1) * HEAD_DIM] for g in range(GROUP)], axis=0)
    n_cols = kc_ref.shape[2]

    def row_pos(width):
        return i * Q_TILE + (lax.broadcasted_iota(jnp.int32, (rows, width), 0) & (Q_TILE - 1))

    s = _dot_t(qs, kc_ref[0, 0])
    n = lax.broadcasted_iota(jnp.int32, (rows, n_cols), 1)
    valid = n * CMP_STRIDE + (CMP_BLOCK - 1) <= jnp.minimum(row_pos(n_cols), (n_cmp - 1) * CMP_STRIDE + CMP_BLOCK - 1)
    s = jnp.where(valid, s, NEG_INF)
    e = jnp.where(valid, jnp.exp(s - jnp.max(s, axis=1, keepdims=True)), 0.0)
    l = jnp.sum(e, axis=1, keepdims=True)
    p = e / jnp.where(l > 0.0, l, 1.0)
    o_c = _dot(p.astype(BF16), vc_ref[0, 0])

    p_grp = p[0:Q_TILE]
    for g in range(1, GROUP):
        p_grp = p_grp + p[g * Q_TILE:(g + 1) * Q_TILE]
    score_t = sum(_dot_t(m01_ref[...], part) for part in _split3(p_grp))
    n_sel_blocks = score_t.shape[0]
    blk = lax.broadcasted_iota(jnp.int32, (n_sel_blocks, Q_TILE), 0)
    tok = i * Q_TILE + lax.broadcasted_iota(jnp.int32, (n_sel_blocks, Q_TILE), 1)
    sel_t = _top_blocks_mask(_force_scores(score_t, blk, tok // SEL_BLOCK), blk, min(N_SEL, n_sel_blocks))
    not_sel = (1.0 - sel_t).T.astype(BF16)
    for g in range(GROUP):
        qa_ref[g * Q_TILE:(g + 1) * Q_TILE, 0:n_sel_blocks] = not_sel
    qa_ref[:, n_sel_blocks:n_sel_blocks + HEAD_DIM] = qs
    qa_ref[:, n_sel_blocks + HEAD_DIM:] = jnp.zeros((rows, qa_ref.shape[1] - n_sel_blocks - HEAD_DIM), BF16)

    m_ref[...] = jnp.full(m_ref.shape, NEG_INF, F32)
    acc_ref[...] = jnp.zeros(acc_ref.shape, F32)

    def key_tile(j, causal):
        k0 = pl.multiple_of(j * K_TILE, K_TILE)
        s = _dot_t(qa_ref[...], kaug_ref[0, 0, pl.ds(k0, K_TILE), :])
        if causal:
            kpos = k0 + lax.broadcasted_iota(jnp.int32, (rows, K_TILE), 1)
            s = jnp.where(kpos <= row_pos(K_TILE), s, NEG_INF)
        m_old = m_ref[...]
        m_new = jnp.maximum(m_old, jnp.max(s, axis=1, keepdims=True))
        pe = jnp.exp(s - m_new).astype(BF16)
        acc_ref[...] = jnp.exp(m_old - m_new) * acc_ref[...] + _dot(pe, vaug_ref[0, 0, pl.ds(k0, K_TILE), :])
        m_ref[...] = m_new

    n_full = (i * Q_TILE) // K_TILE

    def body(j, carry):
        key_tile(j, False)
        return carry

    lax.fori_loop(0, n_full, body, 0)
    key_tile(n_full, True)
    acc = acc_ref[...]
    o_s = acc[:, 0:HEAD_DIM] / acc[:, HEAD_DIM:HEAD_DIM + 1]

    w_keys = WINDOW + Q_TILE
    w0 = pl.multiple_of(i * Q_TILE, Q_TILE)
    s = _dot_t(qs, kw_ref[0, 0, pl.ds(w0, w_keys), :])
    kpos = i * Q_TILE - WINDOW + lax.broadcasted_iota(jnp.int32, (rows, w_keys), 1)
    rp = row_pos(w_keys)
    rel = rp - kpos
    valid = (rel >= 0) & (rel <= jnp.minimum(rp, WINDOW - 1))
    s = jnp.where(valid, s, NEG_INF)
    e = jnp.where(valid, jnp.exp(s - jnp.max(s, axis=1, keepdims=True)), 0.0)
    acc_w = _dot(e.astype(BF16), vw_ref[0, 0, pl.ds(w0, w_keys), :])
    o_w = acc_w[:, 0:HEAD_DIM] / acc_w[:, HEAD_DIM:HEAD_DIM + 1]

    gt = gt_ref[...]
    outs = []
    for g in range(GROUP):
        r = slice(g * Q_TILE, (g + 1) * Q_TILE)
        outs.append(gt[:, g:g + 1] * o_c[r] + gt[:, GROUP + g:GROUP + g + 1] * o_s[r]
                    + gt[:, 2 * GROUP + g:2 * GROUP + g + 1] * o_w[r])
    o_ref[...] = jnp.concatenate(outs, axis=1)


def _nsa_prompt(q2d, kaug, vaug, kw, vw, kc, vc, gt2d, m01, batch, seq, n_cmp):
    nq = seq // Q_TILE
    rows = GROUP * Q_TILE
    kdim = kaug.shape[-1]
    per_head = lambda a: pl.BlockSpec((1, 1) + a.shape[2:], lambda b, h, i: (b, h, 0, 0))
    return pl.pallas_call(
        functools.partial(_nsa_prompt_kernel, n_cmp=n_cmp),
        grid=(batch, KV_HEADS, nq),
        in_specs=[pl.BlockSpec((Q_TILE, GROUP * HEAD_DIM), lambda b, h, i: (b * nq + i, h)),
                  per_head(kaug), per_head(vaug), per_head(kw), per_head(vw), per_head(kc), per_head(vc),
                  pl.BlockSpec((Q_TILE, LANES), lambda b, h, i: (b * nq + i, h)),
                  pl.BlockSpec(m01.shape, lambda b, h, i: (0, 0))],
        out_specs=pl.BlockSpec((Q_TILE, GROUP * HEAD_DIM), lambda b, h, i: (b * nq + i, h)),
        out_shape=jax.ShapeDtypeStruct((batch * seq, W_ATTN), F32),
        scratch_shapes=[pltpu.VMEM((rows, kdim), BF16), pltpu.VMEM((rows, LANES), F32),
                        pltpu.VMEM((rows, 1), F32)],
        compiler_params=_cparams(("parallel", "parallel", "arbitrary")),
        name="nsa_prompt",
    )(q2d, kaug, vaug, kw, vw, kc, vc, gt2d, m01)


def _sel_matrix(n_sel_blocks, n_cols):
    n = np.arange(n_cols)
    m = (n[None, :] // PER_SEL == np.arange(n_sel_blocks)[:, None]) & (n[None, :] % PER_SEL < PER_SEL - 1)
    return jnp.asarray(m, BF16)


def _with_ones(v):
    ones = jnp.ones(v.shape[:-1] + (1,), v.dtype)
    zeros = jnp.zeros(v.shape[:-1] + (LANES - HEAD_DIM - 1,), v.dtype)
    return jnp.concatenate([v, ones, zeros], axis=-1).astype(BF16)


def _prompt_attention(q2d, kvc2d, kvs2d, kvw2d, gt2d, w_cmp2d, batch, seq):
    n_chunks = seq // CMP_STRIDE
    n_cmp = n_chunks - 1
    n_sel_blocks = -(-seq // SEL_BLOCK)
    kcv = _compress_prompt(kvc2d, w_cmp2d, seq, min(seq, 2048))

    def heads(x2d, rows):
        x = x2d.reshape(batch, rows, 2, KV_HEADS, HEAD_DIM).transpose(2, 0, 3, 1, 4)
        return x[0], x[1]

    kc, vc = heads(kcv, n_chunks)
    ks, vs = heads(kvs2d, seq)
    kw, vw = heads(kvw2d, seq)
    onehot = (jnp.arange(seq)[:, None] // SEL_BLOCK == jnp.arange(n_sel_blocks)[None, :]).astype(F32) * NEG_INF
    kpad = LANES * (-(-(n_sel_blocks + HEAD_DIM) // LANES)) - n_sel_blocks - HEAD_DIM
    kaug = jnp.concatenate([jnp.broadcast_to(onehot, (batch, KV_HEADS, seq, n_sel_blocks)), ks,
                            jnp.zeros((batch, KV_HEADS, seq, kpad), F32)], axis=-1).astype(BF16)
    pad = ((0, 0), (0, 0), (WINDOW, 0), (0, 0))
    return _nsa_prompt(q2d, kaug, _with_ones(vs), jnp.pad(kw, pad).astype(BF16), _with_ones(jnp.pad(vw, pad)),
                       kc.astype(BF16), vc.astype(BF16), gt2d, _sel_matrix(n_sel_blocks, n_chunks),
                       batch, seq, n_cmp)


def _nsa_sample_a_kernel(q_ref, kcv_ref, win_ref, gt_ref, m01_ref, part_ref, idx_ref, *, past, ts, n_cmp, w_len):
    rows = GROUP * ts
    kcv = kcv_ref[0]
    win = win_ref[0]
    n_cols = kcv.shape[0]
    w_pad = win.shape[0]
    n_blk_pad = m01_ref.shape[0]
    qpos = past + (lax.broadcasted_iota(jnp.int32, (rows, 1), 0) % ts)
    scores = []
    for h in range(KV_HEADS):
        qh = q_ref[0, h]
        ksl = slice(h * HEAD_DIM, (h + 1) * HEAD_DIM)
        vsl = slice((KV_HEADS + h) * HEAD_DIM, (KV_HEADS + h + 1) * HEAD_DIM)
        s = _dot_t(qh, kcv[:, ksl].astype(BF16))
        n = lax.broadcasted_iota(jnp.int32, (rows, n_cols), 1)
        valid = n * CMP_STRIDE + (CMP_BLOCK - 1) <= jnp.minimum(qpos, (n_cmp - 1) * CMP_STRIDE + CMP_BLOCK - 1)
        s = jnp.where(valid, s, NEG_INF)
        e = jnp.where(valid, jnp.exp(s - jnp.max(s, axis=1, keepdims=True)), 0.0)
        l = jnp.sum(e, axis=1, keepdims=True)
        p = e / jnp.where(l > 0.0, l, 1.0)
        o_c = _dot(p.astype(BF16), kcv[:, vsl].astype(BF16))
        p_grp = p[0:ts]
        for g in range(1, GROUP):
            p_grp = p_grp + p[g * ts:(g + 1) * ts]
        scores.append(sum(_dot_t(part, m01_ref[...]) for part in _split3(p_grp)))

        s = _dot_t(qh, win[:, ksl].astype(BF16))
        r = lax.broadcasted_iota(jnp.int32, (rows, w_pad), 1)
        kpos = past + ts - w_len + r
        rel = qpos - kpos
        valid = (rel >= 0) & (rel <= jnp.minimum(qpos, WINDOW - 1))
        s = jnp.where(valid, s, NEG_INF)
        e = jnp.where(valid, jnp.exp(s - jnp.max(s, axis=1, keepdims=True)), 0.0)
        o_w = _dot(e.astype(BF16), win[:, vsl].astype(BF16)) / jnp.sum(e, axis=1, keepdims=True)
        gt = gt_ref[0, h]
        part_ref[0, h] = gt[:, 0:1] * o_c + gt[:, 2:3] * o_w

    sc = jnp.concatenate(scores, axis=0)
    nr = KV_HEADS * ts
    blk = lax.broadcasted_iota(jnp.int32, (nr, n_blk_pad), 1)
    cur = (past + (lax.broadcasted_iota(jnp.int32, (nr, 1), 0) % ts)) // SEL_BLOCK
    sc = _force_scores(sc, blk, cur)
    lane = lax.broadcasted_iota(jnp.int32, (nr, LANES), 1)
    idx = jnp.zeros((nr, LANES), jnp.int32)
    for k in range(N_SEL):
        mx = jnp.max(sc, axis=1, keepdims=True)
        jm = jnp.min(jnp.where(sc == mx, blk, n_blk_pad), axis=1, keepdims=True)
        idx = jnp.where(lane == k, jm, idx)
        sc = jnp.where(blk == jm, LOWEST, sc)
    idx_ref[0] = idx


def _nsa_sample_a(q4, kcv, win_ext, gt4, m01, past, ts, n_cmp, w_len):
    bsz = q4.shape[0]
    rows = GROUP * ts
    return pl.pallas_call(
        functools.partial(_nsa_sample_a_kernel, past=past, ts=ts, n_cmp=n_cmp, w_len=w_len),
        grid=(bsz,),
        in_specs=[pl.BlockSpec((1, KV_HEADS, rows, HEAD_DIM), lambda b: (b, 0, 0, 0)),
                  pl.BlockSpec((1,) + kcv.shape[1:], lambda b: (b, 0, 0)),
                  pl.BlockSpec((1,) + win_ext.shape[1:], lambda b: (b, 0, 0)),
                  pl.BlockSpec((1, KV_HEADS, rows, LANES), lambda b: (b, 0, 0, 0)),
                  pl.BlockSpec(m01.shape, lambda b: (0, 0))],
        out_specs=[pl.BlockSpec((1, KV_HEADS, rows, HEAD_DIM), lambda b: (b, 0, 0, 0)),
                   pl.BlockSpec((1, KV_HEADS * ts, LANES), lambda b: (b, 0, 0))],
        out_shape=[jax.ShapeDtypeStruct((bsz, KV_HEADS, rows, HEAD_DIM), F32),
                   jax.ShapeDtypeStruct((bsz, KV_HEADS * ts, LANES), jnp.int32)],
        compiler_params=_cparams(("parallel",)),
        name="nsa_sample_scores",
    )(q4, kcv, win_ext, gt4, m01)


def _nsa_sample_b_kernel(idx_ref, pt_ref, *refs, past, ts, n_past_blocks):
    del pt_ref
    blocks = refs[:N_SEL]
    q_ref, new_ref, gt_ref, part_ref, o_ref = refs[N_SEL:]
    r = pl.program_id(0)
    t = r % ts
    head0 = ((r // ts) % KV_HEADS) == 0
    q = q_ref[0]

    def pick(x, off):
        return jnp.where(head0, x[:, off:off + HEAD_DIM], x[:, off + HEAD_DIM:off + 2 * HEAD_DIM])

    s_list, v_list = [], []
    has_new = False
    for k in range(N_SEL):
        j = idx_ref[r * N_SEL + k]
        is_past = j < n_past_blocks
        has_new = jnp.logical_or(has_new, jnp.logical_not(is_past))
        blk = blocks[k][0]
        kpos = j * SEL_BLOCK + lax.broadcasted_iota(jnp.int32, (GROUP, SEL_BLOCK), 1)
        ok = kpos <= jnp.where(is_past, past + t, -1)
        s_list.append(jnp.where(ok, _dot_t(q, pick(blk, 0).astype(BF16)), NEG_INF))
        v_list.append(pick(blk, KV_HEADS * HEAD_DIM).astype(BF16))
    new = new_ref[0]
    tn = lax.broadcasted_iota(jnp.int32, (GROUP, ts), 1)
    s_new = jnp.where(tn <= jnp.where(has_new, t, -1), _dot_t(q, pick(new, 0).astype(BF16)), NEG_INF)
    m = jnp.max(s_new, axis=1, keepdims=True)
    for s in s_list:
        m = jnp.maximum(m, jnp.max(s, axis=1, keepdims=True))
    e_new = jnp.exp(s_new - m)
    l = jnp.sum(e_new, axis=1, keepdims=True)
    o = _dot(e_new.astype(BF16), pick(new, KV_HEADS * HEAD_DIM).astype(BF16))
    for s, v in zip(s_list, v_list):
        e = jnp.exp(s - m)
        l = l + jnp.sum(e, axis=1, keepdims=True)
        o = o + _dot(e.astype(BF16), v)
    gt = gt_ref[0]
    o_ref[0] = part_ref[0] + gt[:, 1:2] * (o / l)


def _nsa_sample_b(idx_flat, pt_flat, cache_blocks, q_rows, new_rows, gt_rows, part_rows, past, ts, n_pages):
    n_rows = q_rows.shape[0]
    n_past_blocks = past // SEL_BLOCK
    bpp = n_past_blocks // n_pages

    def blk_spec(k):
        def imap(r, idx, pt):
            j = jnp.minimum(idx[r * N_SEL + k], n_past_blocks - 1)
            b = r // (KV_HEADS * ts)
            return (pt[b * n_pages + j // bpp] * bpp + j % bpp, 0, 0)
        return pl.BlockSpec((1, SEL_BLOCK, KV_W), imap)

    row3 = lambda a: pl.BlockSpec((1,) + a.shape[1:], lambda r, idx, pt: (r, 0, 0))
    grid_spec = pltpu.PrefetchScalarGridSpec(
        num_scalar_prefetch=2,
        grid=(n_rows,),
        in_specs=[blk_spec(k) for k in range(N_SEL)] + [
            row3(q_rows),
            pl.BlockSpec((1,) + new_rows.shape[1:], lambda r, idx, pt: (r // (KV_HEADS * ts), 0, 0)),
            row3(gt_rows), row3(part_rows)],
        out_specs=row3(part_rows),
    )
    return pl.pallas_call(
        functools.partial(_nsa_sample_b_kernel, past=past, ts=ts, n_past_blocks=n_past_blocks),
        grid_spec=grid_spec,
        out_shape=jax.ShapeDtypeStruct(part_rows.shape, F32),
        compiler_params=_cparams(("arbitrary",)),
        name="nsa_sample_select",
    )(idx_flat, pt_flat, *([cache_blocks] * N_SEL), q_rows, new_rows, gt_rows, part_rows)


def _sample_attention(q_tm, kvs_tm, gt_tm, cache_c, cache_s, win_ext, page_table, w_cmp2d, bsz, ts):
    n_phys, page = cache_c.shape[0], cache_c.shape[1]
    n_pages = page_table.shape[1]
    past = n_pages * page
    assert ts < CMP_STRIDE and page % SEL_BLOCK == 0
    total = past + ts
    n_cmp = total // CMP_STRIDE - 1
    n_sel_blocks = -(-total // SEL_BLOCK)
    kcv = _compress_paged(cache_c.reshape(n_phys, page, KV_W), page_table, w_cmp2d)
    n_cols = kcv.shape[1]
    n_blk_pad = LANES * (-(-n_sel_blocks // LANES))
    m01 = _sel_matrix(n_blk_pad, n_cols)
    q4 = q_tm.reshape(ts, bsz, KV_HEADS, GROUP, HEAD_DIM).transpose(1, 2, 3, 0, 4).reshape(
        bsz, KV_HEADS, GROUP * ts, HEAD_DIM)
    g5 = gt_tm.reshape(ts, bsz, KV_HEADS, LANES)[..., :3 * GROUP].reshape(ts, bsz, KV_HEADS, 3, GROUP)
    g5 = g5.transpose(1, 2, 4, 0, 3).reshape(bsz, KV_HEADS, GROUP * ts, 3)
    gt4 = jnp.pad(g5, ((0, 0), (0, 0), (0, 0), (0, LANES - 3)))
    w_len = win_ext.shape[1]
    w_pad = 8 * (-(-w_len // 8))
    win_p = jnp.pad(win_ext.reshape(bsz, w_len, KV_W), ((0, 0), (0, w_pad - w_len), (0, 0)))
    part, idx = _nsa_sample_a(q4, kcv, win_p, gt4, m01, past, ts, n_cmp, w_len)
    to_rows = lambda a: a.reshape(bsz, KV_HEADS, GROUP, ts, a.shape[-1]).transpose(0, 1, 3, 2, 4).reshape(
        bsz * KV_HEADS * ts, GROUP, a.shape[-1])
    new_rows = kvs_tm.reshape(ts, bsz, KV_W).transpose(1, 0, 2)
    spp = page // SEL_BLOCK
    o_rows = _nsa_sample_b(idx[:, :, :N_SEL].reshape(-1), page_table.reshape(-1),
                           cache_s.reshape(n_phys * spp, SEL_BLOCK, KV_W),
                           to_rows(q4), new_rows, to_rows(gt4), to_rows(part), past, ts, n_pages)
    return o_rows.reshape(bsz, KV_HEADS, ts, GROUP, HEAD_DIM).transpose(2, 0, 1, 3, 4).reshape(ts * bsz, W_ATTN)


def _mix_out_kernel(x_ref, a_ref, b_ref, bg_ref, w_ref, g_ref, beta_ref, o_ref):
    a = a_ref[...].astype(BF16)
    b = (b_ref[...] * bg_ref[...]).astype(BF16)
    d = _dot(a, w_ref[0:W_CONV, :]) + _dot(b, w_ref[W_CONV:W_CONV + W_ATTN, :])
    o_ref[...] = _layer_norm(DEEPNORM_ALPHA * x_ref[...] + d, g_ref[...], beta_ref[...])


def _mix_out(x2d, a2d, b2d, bg2d, w_out, ln_g, ln_b, tm):
    n = x2d.shape[0]
    row = lambda w: pl.BlockSpec((tm, w), lambda i: (i, 0))
    vec = pl.BlockSpec((1, D_MODEL), lambda i: (0, 0))
    return pl.pallas_call(
        _mix_out_kernel,
        grid=(n // tm,),
        in_specs=[row(D_MODEL), row(W_CONV), row(W_ATTN), row(W_ATTN),
                  pl.BlockSpec((W_CONV + W_ATTN, D_MODEL), lambda i: (0, 0)), vec, vec],
        out_specs=row(D_MODEL),
        out_shape=jax.ShapeDtypeStruct((n, D_MODEL), F32),
        compiler_params=_cparams(("parallel",)),
        name="mix_out",
    )(x2d, a2d, b2d, bg2d, w_out, ln_g, ln_b)


POOL_HALO = 16


def _pool_tail(x, d_groups, gate, wg_ref, sc_ref, wo_ref, g_ref, beta_ref):
    mixed = jnp.concatenate([_dot(d_groups[gi].astype(BF16), wg_ref[gi]) for gi in range(len(POOL_WINDOWS))], axis=1)
    h = (mixed * sc_ref[...] * _silu(gate)).astype(BF16)
    return _layer_norm(DEEPNORM_ALPHA * x + _dot(h, wo_ref[...]), g_ref[...], beta_ref[...])


def _pool_prompt_kernel(x_ref, wi_ref, wg_ref, sc_ref, wo_ref, g_ref, beta_ref, o_ref, tail_ref, ext_ref, *, tm):
    i = pl.program_id(1)

    @pl.when(i == 0)
    def _():
        ext_ref[0:POOL_HALO, :] = jnp.zeros((POOL_HALO, W_POOL), F32)

    x = x_ref[...]
    xb = x.astype(BF16)
    v = _dot(xb, wi_ref[:, 0:W_POOL])
    gate = _dot(xb, wi_ref[:, W_POOL:2 * W_POOL])
    ext_ref[POOL_HALO:POOL_HALO + tm, :] = v
    pos = i * tm + lax.broadcasted_iota(jnp.int32, (tm, 1), 0)
    d_groups = []
    for gi, w in enumerate(POOL_WINDOWS):
        c = slice(gi * POOL_GROUP_W, (gi + 1) * POOL_GROUP_W)
        win_sum = v[:, c]
        for k in range(1, w):
            win_sum = win_sum + ext_ref[POOL_HALO - k:POOL_HALO - k + tm, c]
        cnt = jnp.minimum(pos + 1, w).astype(F32)
        d_groups.append(win_sum / cnt - v[:, c])
    o_ref[...] = _pool_tail(x, d_groups, gate, wg_ref, sc_ref, wo_ref, g_ref, beta_ref)
    ext_ref[0:POOL_HALO, :] = ext_ref[tm:tm + POOL_HALO, :]
    tail_ref[0] = ext_ref[0:POOL_HALO, :]


def _pool_prompt(x2d, w_in, w_grp, scale, w_out, ln_g, ln_b, batch, seq, tm):
    nt = seq // tm
    const = lambda a: pl.BlockSpec(a.shape, lambda b, i: (0,) * a.ndim)
    row = pl.BlockSpec((tm, D_MODEL), lambda b, i: (b * nt + i, 0))
    return pl.pallas_call(
        functools.partial(_pool_prompt_kernel, tm=tm),
        grid=(batch, nt),
        in_specs=[row, const(w_in), const(w_grp), const(scale), const(w_out), const(ln_g), const(ln_b)],
        out_specs=[row, pl.BlockSpec((1, POOL_HALO, W_POOL), lambda b, i: (b, 0, 0))],
        out_shape=[jax.ShapeDtypeStruct((batch * seq, D_MODEL), F32),
                   jax.ShapeDtypeStruct((batch, POOL_HALO, W_POOL), F32)],
        scratch_shapes=[pltpu.VMEM((tm + POOL_HALO, W_POOL), F32)],
        compiler_params=_cparams(("arbitrary", "arbitrary")),
        name="pool_prompt",
    )(x2d, w_in, w_grp, scale, w_out, ln_g, ln_b)


def _pool_sample_kernel(x_ref, st_ref, wi_ref, wg_ref, sc_ref, wo_ref, g_ref, beta_ref, o_ref, v_ref, *,
                        ts, bsz, pos0):
    x = x_ref[...]
    xb = x.astype(BF16)
    v = _dot(xb, wi_ref[:, 0:W_POOL])
    gate = _dot(xb, wi_ref[:, W_POOL:2 * W_POOL])
    v_ref[...] = v
    n_hist = st_ref.shape[0]

    def ext(r):
        return st_ref[r] if r < n_hist else v[(r - n_hist) * bsz:(r - n_hist + 1) * bsz, :]

    d_groups = []
    for gi, w in enumerate(POOL_WINDOWS):
        c = slice(gi * POOL_GROUP_W, (gi + 1) * POOL_GROUP_W)
        per_t = []
        for t in range(ts):
            win_sum = ext(n_hist + t)[:, c]
            for k in range(1, w):
                win_sum = win_sum + ext(n_hist + t - k)[:, c]
            per_t.append(win_sum / float(min(pos0 + t + 1, w)) - ext(n_hist + t)[:, c])
        d_groups.append(jnp.concatenate(per_t, axis=0))
    o_ref[...] = _pool_tail(x, d_groups, gate, wg_ref, sc_ref, wo_ref, g_ref, beta_ref)


def _pool_sample(x_tm, state_tm, w_in, w_grp, scale, w_out, ln_g, ln_b, ts, bsz, pos0):
    return pl.pallas_call(
        functools.partial(_pool_sample_kernel, ts=ts, bsz=bsz, pos0=pos0),
        out_shape=[jax.ShapeDtypeStruct((ts * bsz, D_MODEL), F32), jax.ShapeDtypeStruct((ts * bsz, W_POOL), F32)],
        compiler_params=pltpu.CompilerParams(vmem_limit_bytes=VMEM_LIMIT),
        name="pool_sample",
    )(x_tm, state_tm, w_in, w_grp, scale, w_out, ln_g, ln_b)


def kernel(x_prompt, x_sample, cache_kv_cmp, cache_kv_sel, state_win_kv, state_conv, state_pool, page_table,
           w_in_even, w_cmp, conv_w, conv_b, conv_ln_g, conv_ln_b, w_out_even,
           w_in_odd, w_pool_grp, pool_scale, w_out_odd, ln_g, ln_b):
    batch, seq, _ = x_prompt.shape
    bsz, ts, _ = x_sample.shape
    past = page_table.shape[1] * cache_kv_cmp.shape[2]
    kv_shape = (2, KV_HEADS, HEAD_DIM)
    vec = lambda a: a.reshape(1, -1)

    w_pad = _pad_even_weights(w_in_even[0])
    w_cmp2d = w_cmp[0].reshape(CMP_BLOCK, KV_W)
    w_oe = w_out_even[0].astype(BF16)
    cw, cb, cg, cbeta = conv_w[0], vec(conv_b[0]), vec(conv_ln_g[0]), vec(conv_ln_b[0])
    g0, b0 = vec(ln_g[0]), vec(ln_b[0])

    xp = x_prompt.reshape(batch * seq, D_MODEL)
    u, sg, q, kvc, kvs, kvw, bg, gt = _proj_even(xp, w_pad, 256)
    a_out = _conv_prompt(u, sg, cw, cb, cg, cbeta, batch, seq, 256)
    b_raw = _prompt_attention(q, kvc, kvs, kvw, gt, w_cmp2d, batch, seq)
    xp1 = _mix_out(xp, a_out, b_raw, bg, w_oe, g0, b0, 256)

    n_win_p = min(WINDOW, seq)
    kvc_p = kvc.reshape((1, batch, seq) + kv_shape)
    kvs_p = kvs.reshape((1, batch, seq) + kv_shape)
    win_p = kvw.reshape((batch, seq) + kv_shape)[:, seq - n_win_p:][None]
    assert seq >= CONV_WIDTH - 1 and seq >= POOL_MAX - 1
    conv_p = u.reshape(batch, seq, W_CONV)[:, seq - (CONV_WIDTH - 1):][None]

    xs = x_sample.transpose(1, 0, 2).reshape(ts * bsz, D_MODEL)
    us, sgs, qs, kvcs, kvss, kvws, bgs, gts = _proj_even(xs, w_pad, ts * bsz)
    conv_ext = jnp.concatenate([state_conv[0].transpose(1, 0, 2), us.reshape(ts, bsz, W_CONV)], axis=0)
    a_out_s = _conv_sample(conv_ext, sgs.reshape(ts, bsz, W_CONV), cw, cb, cg, cbeta).reshape(ts * bsz, W_CONV)
    to_bt = lambda a: a.reshape(ts, bsz, -1).transpose(1, 0, 2)
    win_ext = jnp.concatenate([state_win_kv[0].reshape(bsz, -1, KV_W), to_bt(kvws)], axis=1)
    b_raw_s = _sample_attention(qs, kvss, gts, cache_kv_cmp[0], cache_kv_sel[0], win_ext, page_table, w_cmp2d,
                                bsz, ts)
    xs1 = _mix_out(xs, a_out_s, b_raw_s, bgs, w_oe, g0, b0, ts * bsz)

    w_len = win_ext.shape[1]
    kvc_s = to_bt(kvcs).reshape((1, bsz, ts) + kv_shape)
    kvs_s = to_bt(kvss).reshape((1, bsz, ts) + kv_shape)
    win_s = win_ext[:, w_len - min(WINDOW, w_len):].reshape((1, bsz, min(WINDOW, w_len)) + kv_shape)
    conv_s = conv_ext[-(CONV_WIDTH - 1):].transpose(1, 0, 2)[None]

    wi_o = w_in_odd[0].astype(BF16)
    wg_o = w_pool_grp[0].astype(BF16)
    wo_o = w_out_odd[0].astype(BF16)
    sc_o = vec(pool_scale[0])
    g1, b1 = vec(ln_g[1]), vec(ln_b[1])
    n_keep = POOL_MAX - 1
    yp, tail_p = _pool_prompt(xp1, wi_o, wg_o, sc_o, wo_o, g1, b1, batch, seq, 256)
    pool_p = tail_p[:, POOL_HALO - n_keep:][None]
    st_tm = state_pool[0].transpose(1, 0, 2)
    ys, vs_new = _pool_sample(xs1, st_tm, wi_o, wg_o, sc_o, wo_o, g1, b1, ts, bsz, past)
    pool_ext = jnp.concatenate([st_tm, vs_new.reshape(ts, bsz, W_POOL)], axis=0)
    pool_s = pool_ext[-n_keep:].transpose(1, 0, 2)[None]

    y_prompt = yp.reshape(batch, seq, D_MODEL)
    y_sample = ys.reshape(ts, bsz, D_MODEL).transpose(1, 0, 2)
    return (y_prompt, y_sample, kvc_p, kvs_p, win_p, conv_p, pool_p, kvc_s, kvs_s, win_s, conv_s, pool_s)
```

```python
import functools

import jax
import jax.numpy as jnp
import numpy as np
from jax import lax
from jax.experimental import pallas as pl
from jax.experimental.pallas import tpu as pltpu

F32 = jnp.float32
BF16 = jnp.bfloat16

D_MODEL = 1024
W_CONV = 512
CONV_WIDTH = 31
N_HEADS = 8
HEAD_DIM = 64
KV_HEADS = 2
GROUP = N_HEADS // KV_HEADS
W_ATTN = N_HEADS * HEAD_DIM
KV_W = 2 * KV_HEADS * HEAD_DIM
V_OFF = KV_HEADS * HEAD_DIM
CMP_STRIDE = 16
CMP_BLOCK = 2 * CMP_STRIDE
SEL_BLOCK = 64
N_SEL = 16
WINDOW = 512
W_POOL = 1024
POOL_WINDOWS = (2, 4, 8, 16)
POOL_GROUP_W = W_POOL // len(POOL_WINDOWS)
POOL_MAX = 16
LN_EPS = 1e-5
NEG_INF = -1e30
SEL_FORCE = 1e9
DEPTH = 2
DEEPNORM_ALPHA = (2 * DEPTH) ** 0.25
EVEN_SPLITS = (W_CONV, W_CONV, W_CONV, W_ATTN, KV_W, KV_W, KV_W, 3 * N_HEADS, W_ATTN)

LANES = 128
E_PAD = 3 * W_CONV + 2 * W_ATTN + 3 * KV_W + KV_HEADS * LANES
VMEM_LIMIT = 56 * 1024 * 1024
Q_TILE = 128
K_TILE = 512
CMP_LANES = 2048
PER_SEL = SEL_BLOCK // CMP_STRIDE
LOWEST = -3.0e38


def _cparams(sem):
    return pltpu.CompilerParams(dimension_semantics=sem, vmem_limit_bytes=VMEM_LIMIT)


def _round_up(x, m):
    return m * (-(-x // m))


def _sigmoid(x):
    return 1.0 / (1.0 + jnp.exp(-x))


def _silu(x):
    return x * _sigmoid(x)


def _layer_norm(z, g, b):
    mu = jnp.mean(z, axis=-1, keepdims=True)
    zc = z - mu
    var = jnp.mean(zc * zc, axis=-1, keepdims=True)
    return zc * lax.rsqrt(var + LN_EPS) * g + b


def _dot_t(a, b):
    return lax.dot_general(a, b, (((1,), (1,)), ((), ())), preferred_element_type=F32)


def _dot(a, b):
    return jnp.dot(a, b, preferred_element_type=F32)


def _split3(x):
    hi = x.astype(BF16)
    r1 = x - hi.astype(F32)
    mid = r1.astype(BF16)
    lo = (r1 - mid.astype(F32)).astype(BF16)
    return hi, mid, lo


def _shift_left_lanes(x):
    n = x.shape[1]
    col = lax.broadcasted_iota(jnp.int32, x.shape, 1)
    return jnp.where(col < n - 1, pltpu.roll(x, n - 1, 1), 0.0)


KV_COL0 = 3 * W_CONV + W_ATTN


def _proj_common(xb, w_ref, u_ref, sg_ref, q_ref, bg_ref, gt_ref):
    def mm(lo, hi):
        return _dot(xb, w_ref[:, lo:hi])

    o = 0
    a_val = mm(o, o + W_CONV); o += W_CONV
    a_glu = mm(o, o + W_CONV); o += W_CONV
    u_ref[...] = a_val * _sigmoid(a_glu)
    sg_ref[...] = _silu(mm(o, o + W_CONV)); o += W_CONV
    q_ref[...] = (mm(o, o + W_ATTN) * (HEAD_DIM ** -0.5)).astype(BF16); o += W_ATTN
    o += 3 * KV_W
    bg_ref[...] = _silu(mm(o, o + W_ATTN)); o += W_ATTN
    gt_ref[...] = _sigmoid(mm(o, o + KV_HEADS * LANES))


def _proj_sample_kernel(x_ref, w_ref, u_ref, sg_ref, q_ref, bg_ref, gt_ref, kvc_ref, kvs_ref, kvw_ref):
    xb = x_ref[...].astype(BF16)
    _proj_common(xb, w_ref, u_ref, sg_ref, q_ref, bg_ref, gt_ref)
    for k, ref in enumerate((kvc_ref, kvs_ref, kvw_ref)):
        ref[...] = _dot(xb, w_ref[:, KV_COL0 + k * KV_W:KV_COL0 + (k + 1) * KV_W])


def _proj_prompt_kernel(x_ref, w_ref, wkv_t_ref, u_ref, sg_ref, q_ref, bg_ref, gt_ref, kvc_t_ref, kvs_t_ref,
                        kvw_t_ref, kaug_ref, vaug_ref, kw_ref, vwa_ref, *, tm, nt, n_sel):
    xb = x_ref[...].astype(BF16)
    _proj_common(xb, w_ref, u_ref, sg_ref, q_ref, bg_ref, gt_ref)
    kvc_t_ref[0] = _dot_t(wkv_t_ref[0:KV_W, :], xb)
    ks_t = _dot_t(wkv_t_ref[KV_W:2 * KV_W, :], xb)
    kvs_t_ref[0] = ks_t
    kw_t = _dot_t(wkv_t_ref[2 * KV_W:3 * KV_W, :], xb)
    kvw_t_ref[0] = kw_t
    key = (pl.program_id(0) % nt) * tm + lax.broadcasted_iota(jnp.int32, (n_sel, tm), 1)
    blk = lax.broadcasted_iota(jnp.int32, (n_sel, tm), 0)
    onehot = jnp.where(key // SEL_BLOCK == blk, NEG_INF, 0.0).astype(BF16)
    ones_row = jnp.where(lax.broadcasted_iota(jnp.int32, (LANES - HEAD_DIM, tm), 0) == 0, 1.0, 0.0).astype(BF16)
    n_zero = kaug_ref.shape[2] - n_sel - HEAD_DIM
    for h in range(KV_HEADS):
        k_rows = slice(h * HEAD_DIM, (h + 1) * HEAD_DIM)
        v_rows = slice(V_OFF + h * HEAD_DIM, V_OFF + (h + 1) * HEAD_DIM)
        kaug_ref[0, h, 0:n_sel, :] = onehot
        kaug_ref[0, h, n_sel:n_sel + HEAD_DIM, :] = ks_t[k_rows].astype(BF16)
        if n_zero:
            kaug_ref[0, h, n_sel + HEAD_DIM:, :] = jnp.zeros((n_zero, tm), BF16)
        vaug_ref[0, h, 0:HEAD_DIM, :] = ks_t[v_rows].astype(BF16)
        vaug_ref[0, h, HEAD_DIM:, :] = ones_row
        kw_ref[0, h] = kw_t[k_rows].astype(BF16)
        vwa_ref[0, h, 0:HEAD_DIM, :] = kw_t[v_rows].astype(BF16)
        vwa_ref[0, h, HEAD_DIM:, :] = ones_row


def _proj_row_specs(n, tm):
    row = lambda w: pl.BlockSpec((tm, w), lambda i: (i, 0))
    widths = (W_CONV, W_CONV, W_ATTN, W_ATTN, KV_HEADS * LANES)
    dtypes = (F32, F32, BF16, F32, F32)
    return row, [row(w) for w in widths], [jax.ShapeDtypeStruct((n, w), d) for w, d in zip(widths, dtypes)]


def _proj_sample(x2d, w_pad):
    n = x2d.shape[0]
    row, specs, shapes = _proj_row_specs(n, n)
    return pl.pallas_call(
        _proj_sample_kernel,
        grid=(1,),
        in_specs=[row(D_MODEL), pl.BlockSpec((D_MODEL, E_PAD), lambda i: (0, 0))],
        out_specs=specs + [row(KV_W)] * 3,
        out_shape=shapes + [jax.ShapeDtypeStruct((n, KV_W), F32)] * 3,
        compiler_params=_cparams(("arbitrary",)),
        name="proj_sample",
    )(x2d, w_pad)


def _proj_prompt(x2d, w_pad, batch, seq, tm):
    n = x2d.shape[0]
    wkv_t = w_pad[:, KV_COL0:KV_COL0 + 3 * KV_W].T
    nt = seq // tm
    n_sel = -(-seq // SEL_BLOCK)
    k_rows = _round_up(n_sel + HEAD_DIM, LANES)
    row, specs, shapes = _proj_row_specs(n, tm)
    chan = pl.BlockSpec((1, KV_W, tm), lambda i: (i // nt, 0, i % nt))
    head = lambda r: pl.BlockSpec((1, KV_HEADS, r, tm), lambda i: (i // nt, 0, 0, i % nt))
    hshape = lambda r: jax.ShapeDtypeStruct((batch, KV_HEADS, r, seq), BF16)
    return pl.pallas_call(
        functools.partial(_proj_prompt_kernel, tm=tm, nt=nt, n_sel=n_sel),
        grid=(n // tm,),
        in_specs=[row(D_MODEL), pl.BlockSpec((D_MODEL, E_PAD), lambda i: (0, 0)),
                  pl.BlockSpec((3 * KV_W, D_MODEL), lambda i: (0, 0))],
        out_specs=specs + [chan] * 3 + [head(k_rows), head(LANES), head(HEAD_DIM), head(LANES)],
        out_shape=shapes + [jax.ShapeDtypeStruct((batch, KV_W, seq), F32)] * 3 + [
            hshape(k_rows), hshape(LANES), hshape(HEAD_DIM), hshape(LANES)],
        compiler_params=_cparams(("parallel",)),
        name="proj_prompt",
    )(x2d, w_pad, wkv_t)


def _pad_even_weights(w):
    offs = np.cumsum(EVEN_SPLITS)[:-1].tolist()
    a_val, a_glu, a_gate, wq, wkc, wks, wkw, wg, wbg = jnp.split(w, offs, axis=1)
    wg = wg.reshape(D_MODEL, KV_HEADS, GROUP, 3).transpose(0, 1, 3, 2).reshape(D_MODEL, KV_HEADS, 3 * GROUP)
    wg = jnp.pad(wg, ((0, 0), (0, 0), (0, LANES - 3 * GROUP))).reshape(D_MODEL, KV_HEADS * LANES)
    return jnp.concatenate([a_val, a_glu, a_gate, wq, wkc, wks, wkw, wbg, wg], axis=1).astype(BF16)


CONV_HALO = 32
CONV_CHUNK = 32


def _conv_prompt_kernel(u_ref, sg_ref, w_ref, cb_ref, g_ref, b_ref, o_ref, ext_ref, *, tt):
    @pl.when(pl.program_id(1) == 0)
    def _():
        ext_ref[0:CONV_HALO, :] = jnp.zeros((CONV_HALO, W_CONV), F32)

    ext_ref[CONV_HALO:CONV_HALO + tt, :] = u_ref[...]
    base = CONV_HALO - (CONV_WIDTH - 1)
    for c in range(tt // CONV_CHUNK):
        r0 = c * CONV_CHUNK
        acc = jnp.zeros((CONV_CHUNK, W_CONV), F32) + cb_ref[...]
        for k in range(CONV_WIDTH):
            acc = acc + w_ref[k:k + 1, :] * ext_ref[base + r0 + k:base + r0 + k + CONV_CHUNK, :]
        y = _layer_norm(acc, g_ref[...], b_ref[...])
        o_ref[r0:r0 + CONV_CHUNK, :] = _silu(y) * sg_ref[r0:r0 + CONV_CHUNK, :]
    ext_ref[0:CONV_HALO, :] = ext_ref[tt:tt + CONV_HALO, :]


def _conv_prompt(u2d, sg2d, conv_w, conv_b, ln_g, ln_b, batch, seq, tt):
    nt = seq // tt
    row = pl.BlockSpec((tt, W_CONV), lambda b, i: (b * nt + i, 0))
    vec = pl.BlockSpec((1, W_CONV), lambda b, i: (0, 0))
    return pl.pallas_call(
        functools.partial(_conv_prompt_kernel, tt=tt),
        grid=(batch, nt),
        in_specs=[row, row, pl.BlockSpec((CONV_WIDTH, W_CONV), lambda b, i: (0, 0)), vec, vec, vec],
        out_specs=row,
        out_shape=jax.ShapeDtypeStruct((batch * seq, W_CONV), F32),
        scratch_shapes=[pltpu.VMEM((tt + CONV_HALO, W_CONV), F32)],
        compiler_params=_cparams(("arbitrary", "arbitrary")),
        name="conv_prompt",
    )(u2d, sg2d, conv_w, conv_b, ln_g, ln_b)


def _conv_sample_kernel(ext_ref, sg_ref, w_ref, cb_ref, g_ref, b_ref, o_ref, *, ts):
    for t in range(ts):
        acc = jnp.zeros(ext_ref.shape[1:], F32) + cb_ref[...]
        for k in range(CONV_WIDTH):
            acc = acc + w_ref[k:k + 1, :] * ext_ref[t + k]
        y = _layer_norm(acc, g_ref[...], b_ref[...])
        o_ref[t] = _silu(y) * sg_ref[t]


def _conv_sample(ext, sg, conv_w, conv_b, ln_g, ln_b):
    ts, bsz, _ = sg.shape
    return pl.pallas_call(
        functools.partial(_conv_sample_kernel, ts=ts),
        out_shape=jax.ShapeDtypeStruct((ts, bsz, W_CONV), F32),
        name="conv_sample",
    )(ext, sg, conv_w, conv_b, ln_g, ln_b)


def _chunk_sums(x, w1_ref, w2_ref, s_ref):
    reps = x.shape[1] // LANES

    def seg(w_ref):
        y = x * jnp.tile(w_ref[...], (1, reps))
        hi = y.astype(BF16)
        lo = (y - hi.astype(F32)).astype(BF16)
        return _dot(hi, s_ref[...]) + _dot(lo, s_ref[...])

    return seg(w1_ref), seg(w2_ref)


def _compress_prompt_kernel(x_ref, w1_ref, w2_ref, s_ref, o_ref, *, lanes):
    firsts, seconds = [], []
    for c in range(x_ref.shape[2] // lanes):
        a, b = _chunk_sums(x_ref[0, :, c * lanes:(c + 1) * lanes], w1_ref, w2_ref, s_ref)
        firsts.append(a)
        seconds.append(b)
    o_ref[0] = jnp.concatenate(firsts, axis=1) + _shift_left_lanes(jnp.concatenate(seconds, axis=1))


def _compress_prompt(kvc_t, w1t, w2t, seg):
    batch, _, seq = kvc_t.shape
    lanes = seg.shape[0]
    const = lambda a: pl.BlockSpec(a.shape, lambda b: (0, 0))
    return pl.pallas_call(
        functools.partial(_compress_prompt_kernel, lanes=lanes),
        grid=(batch,),
        in_specs=[pl.BlockSpec((1, KV_W, seq), lambda b: (b, 0, 0)), const(w1t), const(w2t), const(seg)],
        out_specs=pl.BlockSpec((1, KV_W, seq // CMP_STRIDE), lambda b: (b, 0, 0)),
        out_shape=jax.ShapeDtypeStruct((batch, KV_W, seq // CMP_STRIDE), F32),
        compiler_params=_cparams(("parallel",)),
        name="compress_prompt",
    )(kvc_t, w1t, w2t, seg)


def _compress_paged_kernel(pt_ref, *refs, n_pages_step):
    del pt_ref
    pages = refs[:n_pages_step]
    w1_ref, w2_ref, s_ref, a_ref, b_ref = refs[n_pages_step:]
    x = jnp.concatenate([p[0] for p in pages], axis=1)
    a_ref[0], b_ref[0] = _chunk_sums(x, w1_ref, w2_ref, s_ref)


def _compress_paged(cache_t, page_table, w1t, w2t, seg):
    _, _, page = cache_t.shape
    bsz, n_pages = page_table.shape
    lanes = seg.shape[0]
    pps = lanes // page
    n_steps = n_pages // pps
    n_out = lanes // CMP_STRIDE

    def page_spec(k):
        return pl.BlockSpec((1, KV_W, page), lambda b, s, pt: (pt[b, s * pps + k], 0, 0))

    const = lambda a: pl.BlockSpec(a.shape, lambda b, s, pt: (0, 0))
    out_spec = pl.BlockSpec((1, KV_W, n_out), lambda b, s, pt: (b, 0, s))
    out_shape = jax.ShapeDtypeStruct((bsz, KV_W, n_steps * n_out), F32)
    grid_spec = pltpu.PrefetchScalarGridSpec(
        num_scalar_prefetch=1,
        grid=(bsz, n_steps),
        in_specs=[page_spec(k) for k in range(pps)] + [const(w1t), const(w2t), const(seg)],
        out_specs=[out_spec, out_spec],
    )
    return pl.pallas_call(
        functools.partial(_compress_paged_kernel, n_pages_step=pps),
        grid_spec=grid_spec,
        out_shape=[out_shape, out_shape],
        compiler_params=_cparams(("parallel", "arbitrary")),
        name="compress_paged",
    )(page_table, *([cache_t] * pps), w1t, w2t, seg)


def _compress_consts(w_cmp2d, lanes):
    w1t = jnp.tile(w_cmp2d[:CMP_STRIDE].T, (1, LANES // CMP_STRIDE))
    w2t = jnp.tile(w_cmp2d[CMP_STRIDE:].T, (1, LANES // CMP_STRIDE))
    seg = jnp.asarray(np.arange(lanes)[:, None] // CMP_STRIDE == np.arange(lanes // CMP_STRIDE)[None, :], BF16)
    return w1t, w2t, seg


def _top_blocks_mask(sc, blk, n_pick):
    n_blocks = sc.shape[0]
    sel = jnp.zeros(sc.shape, F32)
    for _ in range(n_pick):
        mx = jnp.max(sc, axis=0, keepdims=True)
        jm = jnp.min(jnp.where(sc == mx, blk, n_blocks), axis=0, keepdims=True)
        pick = blk == jm
        sel = jnp.where(pick, 1.0, sel)
        sc = jnp.where(pick, LOWEST, sc)
    return sel


def _force_scores(score, blk, cur):
    forced = (blk == 0) | ((blk >= cur - 1) & (blk <= cur))
    return jnp.where(forced, SEL_FORCE, jnp.where(blk <= cur, score, -SEL_FORCE))


def _sel_matrix(n_sel_blocks, n_cols):
    n = np.arange(n_cols)
    m = (n[None, :] // PER_SEL == np.arange(n_sel_blocks)[:, None]) & (n[None, :] % PER_SEL < PER_SEL - 1)
    return jnp.asarray(m, BF16)


def _cmp_valid(n, qpos, n_cmp):
    return n * CMP_STRIDE + (CMP_BLOCK - 1) <= jnp.minimum(qpos, (n_cmp - 1) * CMP_STRIDE + CMP_BLOCK - 1)


def _nsa_prompt_kernel(q_ref, kaug_ref, vaug_ref, kw_ref, vwa_ref, kc_ref, vc_ref, gt_ref, m01_ref, o_ref,
                       qa_ref, acc_ref, m_ref, *, n_cmp, n_sel):
    i = pl.program_id(2)
    rows = GROUP * Q_TILE
    q = q_ref[...]
    qs = jnp.concatenate([q[:, g * HEAD_DIM:(g + 1) * HEAD_DIM] for g in range(GROUP)], axis=0)
    n_cols = kc_ref.shape[3]

    def row_pos(width):
        return i * Q_TILE + (lax.broadcasted_iota(jnp.int32, (rows, width), 0) & (Q_TILE - 1))

    s = _dot(qs, kc_ref[0, 0].astype(BF16))
    valid = _cmp_valid(lax.broadcasted_iota(jnp.int32, (rows, n_cols), 1), row_pos(n_cols), n_cmp)
    s = jnp.where(valid, s, NEG_INF)
    e = jnp.where(valid, jnp.exp(s - jnp.max(s, axis=1, keepdims=True)), 0.0)
    l = jnp.sum(e, axis=1, keepdims=True)
    p = e / jnp.where(l > 0.0, l, 1.0)
    o_c = _dot_t(p.astype(BF16), vc_ref[0, 0].astype(BF16))

    p_grp = p[0:Q_TILE]
    for g in range(1, GROUP):
        p_grp = p_grp + p[g * Q_TILE:(g + 1) * Q_TILE]
    score_t = sum(_dot_t(m01_ref[...], part) for part in _split3(p_grp))
    blk = lax.broadcasted_iota(jnp.int32, (n_sel, Q_TILE), 0)
    tok = i * Q_TILE + lax.broadcasted_iota(jnp.int32, (n_sel, Q_TILE), 1)
    sel_t = _top_blocks_mask(_force_scores(score_t, blk, tok // SEL_BLOCK), blk, min(N_SEL, n_sel))
    not_sel = (1.0 - sel_t).T.astype(BF16)
    for g in range(GROUP):
        qa_ref[g * Q_TILE:(g + 1) * Q_TILE, 0:n_sel] = not_sel
    qa_ref[:, n_sel:n_sel + HEAD_DIM] = qs
    if qa_ref.shape[1] > n_sel + HEAD_DIM:
        qa_ref[:, n_sel + HEAD_DIM:] = jnp.zeros((rows, qa_ref.shape[1] - n_sel - HEAD_DIM), BF16)

    m_ref[...] = jnp.full(m_ref.shape, NEG_INF, F32)
    acc_ref[...] = jnp.zeros(acc_ref.shape, F32)

    def key_tile(j, causal):
        k0 = pl.multiple_of(j * K_TILE, K_TILE)
        s = _dot(qa_ref[...], kaug_ref[0, 0, :, pl.ds(k0, K_TILE)])
        if causal:
            kpos = k0 + lax.broadcasted_iota(jnp.int32, (rows, K_TILE), 1)
            s = jnp.where(kpos <= row_pos(K_TILE), s, NEG_INF)
        m_old = m_ref[...]
        m_new = jnp.maximum(m_old, jnp.max(s, axis=1, keepdims=True))
        pe = jnp.exp(s - m_new).astype(BF16)
        acc_ref[...] = jnp.exp(m_old - m_new) * acc_ref[...] + _dot_t(pe, vaug_ref[0, 0, :, pl.ds(k0, K_TILE)])
        m_ref[...] = m_new

    n_full = (i * Q_TILE) // K_TILE

    def body(j, carry):
        key_tile(j, False)
        return carry

    lax.fori_loop(0, n_full, body, 0)
    key_tile(n_full, True)
    acc = acc_ref[...]
    o_s = acc[:, 0:HEAD_DIM] / acc[:, HEAD_DIM:HEAD_DIM + 1]

    w_keys = WINDOW + Q_TILE
    w0 = pl.multiple_of(jnp.maximum(i * Q_TILE - WINDOW, 0), Q_TILE)
    s = _dot(qs, kw_ref[0, 0, :, pl.ds(w0, w_keys)])
    rel = row_pos(w_keys) - (w0 + lax.broadcasted_iota(jnp.int32, (rows, w_keys), 1))
    valid = (rel >= 0) & (rel < WINDOW)
    s = jnp.where(valid, s, NEG_INF)
    e = jnp.where(valid, jnp.exp(s - jnp.max(s, axis=1, keepdims=True)), 0.0)
    acc_w = _dot_t(e.astype(BF16), vwa_ref[0, 0, :, pl.ds(w0, w_keys)])
    o_w = acc_w[:, 0:HEAD_DIM] / acc_w[:, HEAD_DIM:HEAD_DIM + 1]

    gt = gt_ref[...]
    outs = []
    for g in range(GROUP):
        r = slice(g * Q_TILE, (g + 1) * Q_TILE)
        outs.append(gt[:, g:g + 1] * o_c[r] + gt[:, GROUP + g:GROUP + g + 1] * o_s[r]
                    + gt[:, 2 * GROUP + g:2 * GROUP + g + 1] * o_w[r])
    o_ref[...] = jnp.concatenate(outs, axis=1)


def _nsa_prompt(q2d, kaug, vaug, kw, vwa, kcv4, gt2d, m01, batch, seq, n_cmp, n_sel):
    nq = seq // Q_TILE
    rows = GROUP * Q_TILE
    n_cols = kcv4.shape[-1]
    per_head = lambda a: pl.BlockSpec((1, 1) + a.shape[2:], lambda b, h, i: (b, h, 0, 0))
    return pl.pallas_call(
        functools.partial(_nsa_prompt_kernel, n_cmp=n_cmp, n_sel=n_sel),
        grid=(batch, KV_HEADS, nq),
        in_specs=[pl.BlockSpec((Q_TILE, GROUP * HEAD_DIM), lambda b, h, i: (b * nq + i, h)),
                  per_head(kaug), per_head(vaug), per_head(kw), per_head(vwa),
                  pl.BlockSpec((1, 1, HEAD_DIM, n_cols), lambda b, h, i: (b, h, 0, 0)),
                  pl.BlockSpec((1, 1, HEAD_DIM, n_cols), lambda b, h, i: (b, KV_HEADS + h, 0, 0)),
                  pl.BlockSpec((Q_TILE, LANES), lambda b, h, i: (b * nq + i, h)),
                  pl.BlockSpec(m01.shape, lambda b, h, i: (0, 0))],
        out_specs=pl.BlockSpec((Q_TILE, GROUP * HEAD_DIM), lambda b, h, i: (b * nq + i, h)),
        out_shape=jax.ShapeDtypeStruct((batch * seq, W_ATTN), F32),
        scratch_shapes=[pltpu.VMEM((rows, kaug.shape[2]), BF16), pltpu.VMEM((rows, LANES), F32),
                        pltpu.VMEM((rows, 1), F32)],
        compiler_params=_cparams(("parallel", "parallel", "arbitrary")),
        name="nsa_prompt",
    )(q2d, kaug, vaug, kw, vwa, kcv4, kcv4, gt2d, m01)


def _new_token_scores(qf, nk, tq, ts):
    cols = []
    for t in range(ts):
        s = jnp.sum(qf * nk[t:t + 1, :], axis=1, keepdims=True)
        cols.append(jnp.where(tq >= t, s, NEG_INF))
    return cols


def _bf16_round(x):
    return x.astype(BF16).astype(F32)


def _nsa_sample_a_kernel(q_ref, a_ref, b_ref, win_ref, new_ref, gt_ref, m01_ref, part_ref, idx_ref, *,
                         past, ts, n_cmp):
    rows = GROUP * ts
    kcv = a_ref[0] + _shift_left_lanes(b_ref[0])
    win = win_ref[0]
    new = new_ref[0]
    n_cols = kcv.shape[1]
    w_hist = win.shape[1]
    n_blk_pad = m01_ref.shape[0]
    tq = lax.broadcasted_iota(jnp.int32, (rows, 1), 0) % ts
    qpos = past + tq
    scores = []
    for h in range(KV_HEADS):
        qh = q_ref[0, h]
        ksl = slice(h * HEAD_DIM, (h + 1) * HEAD_DIM)
        vsl = slice(V_OFF + h * HEAD_DIM, V_OFF + (h + 1) * HEAD_DIM)
        s = _dot(qh, kcv[ksl].astype(BF16))
        valid = _cmp_valid(lax.broadcasted_iota(jnp.int32, (rows, n_cols), 1), qpos, n_cmp)
        s = jnp.where(valid, s, NEG_INF)
        e = jnp.where(valid, jnp.exp(s - jnp.max(s, axis=1, keepdims=True)), 0.0)
        l = jnp.sum(e, axis=1, keepdims=True)
        p = e / jnp.where(l > 0.0, l, 1.0)
        o_c = _dot_t(p.astype(BF16), kcv[vsl].astype(BF16))
        p_grp = p[0:ts]
        for g in range(1, GROUP):
            p_grp = p_grp + p[g * ts:(g + 1) * ts]
        scores.append(sum(_dot_t(part, m01_ref[...]) for part in _split3(p_grp)))

        s1 = _dot(qh, win[ksl].astype(BF16))
        rel = qpos - (past - w_hist + lax.broadcasted_iota(jnp.int32, (rows, w_hist), 1))
        valid = (rel >= 0) & (rel <= jnp.minimum(qpos, WINDOW - 1))
        s1 = jnp.where(valid, s1, NEG_INF)
        s2 = _new_token_scores(qh.astype(F32), _bf16_round(new[:, ksl]), tq, ts)
        m = jnp.max(s1, axis=1, keepdims=True)
        for c in s2:
            m = jnp.maximum(m, c)
        e1 = jnp.where(valid, jnp.exp(s1 - m), 0.0)
        l = jnp.sum(e1, axis=1, keepdims=True)
        o_w = _dot_t(e1.astype(BF16), win[vsl].astype(BF16))
        nv = _bf16_round(new[:, vsl])
        for t in range(ts):
            e2 = jnp.exp(s2[t] - m)
            l = l + e2
            o_w = o_w + _bf16_round(e2) * nv[t:t + 1, :]
        gt = gt_ref[0, h]
        part_ref[0, h] = gt[:, 0:1] * o_c + gt[:, 2:3] * (o_w / l)

    sc = jnp.concatenate(scores, axis=0)
    nr = KV_HEADS * ts
    blk = lax.broadcasted_iota(jnp.int32, (nr, n_blk_pad), 1)
    cur = (past + (lax.broadcasted_iota(jnp.int32, (nr, 1), 0) % ts)) // SEL_BLOCK
    sc = _force_scores(sc, blk, cur)
    lane = lax.broadcasted_iota(jnp.int32, (nr, LANES), 1)
    idx = jnp.zeros((nr, LANES), jnp.int32)
    for k in range(N_SEL):
        mx = jnp.max(sc, axis=1, keepdims=True)
        jm = jnp.min(jnp.where(sc == mx, blk, n_blk_pad), axis=1, keepdims=True)
        idx = jnp.where(lane == k, jm, idx)
        sc = jnp.where(blk == jm, LOWEST, sc)
    idx_ref[0] = idx


def _nsa_sample_a(q4, a_t, b_t, win_t, new_w, gt4, m01, past, ts, n_cmp):
    bsz = q4.shape[0]
    rows = GROUP * ts
    per_b = lambda a: pl.BlockSpec((1,) + a.shape[1:], lambda b: (b,) + (0,) * (a.ndim - 1))
    return pl.pallas_call(
        functools.partial(_nsa_sample_a_kernel, past=past, ts=ts, n_cmp=n_cmp),
        grid=(bsz,),
        in_specs=[per_b(q4), per_b(a_t), per_b(b_t), per_b(win_t), per_b(new_w), per_b(gt4),
                  pl.BlockSpec(m01.shape, lambda b: (0, 0))],
        out_specs=[pl.BlockSpec((1, KV_HEADS, rows, HEAD_DIM), lambda b: (b, 0, 0, 0)),
                   pl.BlockSpec((1, KV_HEADS * ts, LANES), lambda b: (b, 0, 0))],
        out_shape=[jax.ShapeDtypeStruct((bsz, KV_HEADS, rows, HEAD_DIM), F32),
                   jax.ShapeDtypeStruct((bsz, KV_HEADS * ts, LANES), jnp.int32)],
        compiler_params=_cparams(("parallel",)),
        name="nsa_sample_scores",
    )(q4, a_t, b_t, win_t, new_w, gt4, m01)


def _nsa_sample_b_kernel(idx_ref, pt_ref, *refs, past, ts, n_past_blocks, page):
    del pt_ref
    k_blocks = refs[:N_SEL]
    v_blocks = refs[N_SEL:2 * N_SEL]
    q_ref, new_ref, gt_ref, part_ref, o_ref = refs[2 * N_SEL:]
    r = pl.program_id(0)
    t = r % ts
    head0 = ((r // ts) % KV_HEADS) == 0
    bpp = page // SEL_BLOCK
    q = q_ref[0]
    lane = lax.broadcasted_iota(jnp.int32, (GROUP, page), 1)

    def pick(x, off):
        return jnp.where(head0, x[:, off:off + HEAD_DIM], x[:, off + HEAD_DIM:off + 2 * HEAD_DIM])

    s_list = []
    has_new = False
    for k in range(N_SEL):
        j = idx_ref[r * N_SEL + k]
        is_past = j < n_past_blocks
        has_new = jnp.logical_or(has_new, jnp.logical_not(is_past))
        off = (j % bpp) * SEL_BLOCK
        page_start = (j // bpp) * page
        lo = jnp.where(is_past, off, page)
        hi = jnp.minimum(off + SEL_BLOCK, past + t + 1 - page_start)
        s = _dot(q, k_blocks[k][0].astype(BF16))
        s_list.append(jnp.where((lane >= lo) & (lane < hi), s, NEG_INF))
    new = new_ref[0]
    tq = jnp.where(has_new, t, -1) + jnp.zeros((GROUP, 1), jnp.int32)
    s_new = _new_token_scores(q.astype(F32), _bf16_round(pick(new, 0)), tq, ts)
    m = s_new[0]
    for c in s_new[1:]:
        m = jnp.maximum(m, c)
    for s in s_list:
        m = jnp.maximum(m, jnp.max(s, axis=1, keepdims=True))
    nv = _bf16_round(pick(new, V_OFF))
    l = jnp.zeros((GROUP, 1), F32)
    o = jnp.zeros((GROUP, HEAD_DIM), F32)
    for tt in range(ts):
        e2 = jnp.exp(s_new[tt] - m)
        l = l + e2
        o = o + _bf16_round(e2) * nv[tt:tt + 1, :]
    for k in range(N_SEL):
        e = jnp.exp(s_list[k] - m)
        l = l + jnp.sum(e, axis=1, keepdims=True)
        o = o + _dot_t(e.astype(BF16), v_blocks[k][0].astype(BF16))
    gt = gt_ref[0]
    o_ref[0] = part_ref[0] + gt[:, 1:2] * (o / l)


def _nsa_sample_b(idx_flat, pt_flat, cache_slabs, q_rows, new_rows, gt_rows, part_rows, past, ts, n_pages, page):
    n_rows = q_rows.shape[0]
    n_past_blocks = past // SEL_BLOCK
    bpp = page // SEL_BLOCK
    slabs_per_page = 2 * KV_HEADS

    def slab_spec(k, is_v):
        def imap(r, idx, pt):
            j = jnp.minimum(idx[r * N_SEL + k], n_past_blocks - 1)
            b = r // (KV_HEADS * ts)
            h = (r // ts) % KV_HEADS
            return (pt[b * n_pages + j // bpp] * slabs_per_page + is_v * KV_HEADS + h, 0, 0)
        return pl.BlockSpec((1, HEAD_DIM, page), imap)

    row3 = lambda a: pl.BlockSpec((1,) + a.shape[1:], lambda r, idx, pt: (r, 0, 0))
    grid_spec = pltpu.PrefetchScalarGridSpec(
        num_scalar_prefetch=2,
        grid=(n_rows,),
        in_specs=[slab_spec(k, 0) for k in range(N_SEL)] + [slab_spec(k, 1) for k in range(N_SEL)] + [
            row3(q_rows),
            pl.BlockSpec((1,) + new_rows.shape[1:], lambda r, idx, pt: (r // (KV_HEADS * ts), 0, 0)),
            row3(gt_rows), row3(part_rows)],
        out_specs=row3(part_rows),
    )
    return pl.pallas_call(
        functools.partial(_nsa_sample_b_kernel, past=past, ts=ts, n_past_blocks=n_past_blocks, page=page),
        grid_spec=grid_spec,
        out_shape=jax.ShapeDtypeStruct(part_rows.shape, F32),
        compiler_params=_cparams(("arbitrary",)),
        name="nsa_sample_select",
    )(idx_flat, pt_flat, *([cache_slabs] * (2 * N_SEL)), q_rows, new_rows, gt_rows, part_rows)


def _channel_major(x):
    n, rows = x.shape[:2]
    return x.transpose(0, 2, 3, 4, 1).reshape(n, KV_W, rows)


def _sample_attention(q_tm, kvs_tm, kvw_tm, gt_tm, cache_c, cache_s, state_win, page_table, cmp_consts, bsz, ts):
    n_phys, page = cache_c.shape[0], cache_c.shape[1]
    n_pages = page_table.shape[1]
    past = n_pages * page
    w_hist = state_win.shape[1]
    assert ts < CMP_STRIDE and page % SEL_BLOCK == 0 and ts <= WINDOW and w_hist <= past
    total = past + ts
    n_cmp = total // CMP_STRIDE - 1
    n_sel_blocks = -(-total // SEL_BLOCK)
    a_t, b_t = _compress_paged(_channel_major(cache_c), page_table, *cmp_consts)
    m01 = _sel_matrix(_round_up(n_sel_blocks, LANES), a_t.shape[2])
    q4 = q_tm.reshape(ts, bsz, KV_HEADS, GROUP, HEAD_DIM).transpose(1, 2, 3, 0, 4).reshape(
        bsz, KV_HEADS, GROUP * ts, HEAD_DIM)
    g5 = gt_tm.reshape(ts, bsz, KV_HEADS, LANES)[..., :3 * GROUP].reshape(ts, bsz, KV_HEADS, 3, GROUP)
    g5 = g5.transpose(1, 2, 4, 0, 3).reshape(bsz, KV_HEADS, GROUP * ts, 3)
    gt4 = jnp.pad(g5, ((0, 0), (0, 0), (0, 0), (0, LANES - 3)))
    to_bt = lambda a: a.reshape(ts, bsz, KV_W).transpose(1, 0, 2)
    part, idx = _nsa_sample_a(q4, a_t, b_t, _channel_major(state_win), to_bt(kvw_tm), gt4, m01, past, ts, n_cmp)
    to_rows = lambda a: a.reshape(bsz, KV_HEADS, GROUP, ts, a.shape[-1]).transpose(0, 1, 3, 2, 4).reshape(
        bsz * KV_HEADS * ts, GROUP, a.shape[-1])
    slabs = _channel_major(cache_s).reshape(n_phys * 2 * KV_HEADS, HEAD_DIM, page)
    o_rows = _nsa_sample_b(idx[:, :, :N_SEL].reshape(-1), page_table.reshape(-1), slabs,
                           to_rows(q4), to_bt(kvs_tm), to_rows(gt4), to_rows(part), past, ts, n_pages, page)
    return o_rows.reshape(bsz, KV_HEADS, ts, GROUP, HEAD_DIM).transpose(2, 0, 1, 3, 4).reshape(ts * bsz, W_ATTN)


def _mix_out_kernel(x_ref, a_ref, b_ref, bg_ref, w_ref, g_ref, beta_ref, o_ref):
    a = a_ref[...].astype(BF16)
    b = (b_ref[...] * bg_ref[...]).astype(BF16)
    d = _dot(a, w_ref[0:W_CONV, :]) + _dot(b, w_ref[W_CONV:W_CONV + W_ATTN, :])
    o_ref[...] = _layer_norm(DEEPNORM_ALPHA * x_ref[...] + d, g_ref[...], beta_ref[...])


def _mix_out(x2d, a2d, b2d, bg2d, w_out, ln_g, ln_b, tm):
    n = x2d.shape[0]
    row = lambda w: pl.BlockSpec((tm, w), lambda i: (i, 0))
    vec = pl.BlockSpec((1, D_MODEL), lambda i: (0, 0))
    return pl.pallas_call(
        _mix_out_kernel,
        grid=(n // tm,),
        in_specs=[row(D_MODEL), row(W_CONV), row(W_ATTN), row(W_ATTN),
                  pl.BlockSpec((W_CONV + W_ATTN, D_MODEL), lambda i: (0, 0)), vec, vec],
        out_specs=row(D_MODEL),
        out_shape=jax.ShapeDtypeStruct((n, D_MODEL), F32),
        compiler_params=_cparams(("parallel",)),
        name="mix_out",
    )(x2d, a2d, b2d, bg2d, w_out, ln_g, ln_b)


POOL_HALO = 16


def _pool_tail(x, d_groups, gate, wg_ref, sc_ref, wo_ref, g_ref, beta_ref):
    mixed = jnp.concatenate([_dot(d_groups[gi].astype(BF16), wg_ref[gi]) for gi in range(len(POOL_WINDOWS))], axis=1)
    h = (mixed * sc_ref[...] * _silu(gate)).astype(BF16)
    return _layer_norm(DEEPNORM_ALPHA * x + _dot(h, wo_ref[...]), g_ref[...], beta_ref[...])


def _pool_prompt_kernel(x_ref, wi_ref, wg_ref, sc_ref, wo_ref, g_ref, beta_ref, o_ref, tail_ref, ext_ref, *, tm):
    i = pl.program_id(1)

    @pl.when(i == 0)
    def _():
        ext_ref[0:POOL_HALO, :] = jnp.zeros((POOL_HALO, W_POOL), F32)

    x = x_ref[...]
    xb = x.astype(BF16)
    v = _dot(xb, wi_ref[:, 0:W_POOL])
    gate = _dot(xb, wi_ref[:, W_POOL:2 * W_POOL])
    ext_ref[POOL_HALO:POOL_HALO + tm, :] = v
    pos = i * tm + lax.broadcasted_iota(jnp.int32, (tm, 1), 0)
    d_groups = []
    for gi, w in enumerate(POOL_WINDOWS):
        c = slice(gi * POOL_GROUP_W, (gi + 1) * POOL_GROUP_W)
        win_sum = v[:, c]
        for k in range(1, w):
            win_sum = win_sum + ext_ref[POOL_HALO - k:POOL_HALO - k + tm, c]
        cnt = jnp.minimum(pos + 1, w).astype(F32)
        d_groups.append(win_sum / cnt - v[:, c])
    o_ref[...] = _pool_tail(x, d_groups, gate, wg_ref, sc_ref, wo_ref, g_ref, beta_ref)
    ext_ref[0:POOL_HALO, :] = ext_ref[tm:tm + POOL_HALO, :]
    tail_ref[0] = ext_ref[0:POOL_HALO, :]


def _pool_prompt(x2d, w_in, w_grp, scale, w_out, ln_g, ln_b, batch, seq, tm):
    nt = seq // tm
    const = lambda a: pl.BlockSpec(a.shape, lambda b, i: (0,) * a.ndim)
    row = pl.BlockSpec((tm, D_MODEL), lambda b, i: (b * nt + i, 0))
    return pl.pallas_call(
        functools.partial(_pool_prompt_kernel, tm=tm),
        grid=(batch, nt),
        in_specs=[row, const(w_in), const(w_grp), const(scale), const(w_out), const(ln_g), const(ln_b)],
        out_specs=[row, pl.BlockSpec((1, POOL_HALO, W_POOL), lambda b, i: (b, 0, 0))],
        out_shape=[jax.ShapeDtypeStruct((batch * seq, D_MODEL), F32),
                   jax.ShapeDtypeStruct((batch, POOL_HALO, W_POOL), F32)],
        scratch_shapes=[pltpu.VMEM((tm + POOL_HALO, W_POOL), F32)],
        compiler_params=_cparams(("arbitrary", "arbitrary")),
        name="pool_prompt",
    )(x2d, w_in, w_grp, scale, w_out, ln_g, ln_b)


def _pool_sample_kernel(x_ref, st_ref, wi_ref, wg_ref, sc_ref, wo_ref, g_ref, beta_ref, o_ref, v_ref, *,
                        ts, bsz, pos0):
    x = x_ref[...]
    xb = x.astype(BF16)
    v = _dot(xb, wi_ref[:, 0:W_POOL])
    gate = _dot(xb, wi_ref[:, W_POOL:2 * W_POOL])
    v_ref[...] = v
    n_hist = st_ref.shape[0]

    def ext(r):
        return st_ref[r] if r < n_hist else v[(r - n_hist) * bsz:(r - n_hist + 1) * bsz, :]

    d_groups = []
    for gi, w in enumerate(POOL_WINDOWS):
        c = slice(gi * POOL_GROUP_W, (gi + 1) * POOL_GROUP_W)
        per_t = []
        for t in range(ts):
            win_sum = ext(n_hist + t)[:, c]
            for k in range(1, w):
                win_sum = win_sum + ext(n_hist + t - k)[:, c]
            per_t.append(win_sum / float(min(pos0 + t + 1, w)) - ext(n_hist + t)[:, c])
        d_groups.append(jnp.concatenate(per_t, axis=0))
    o_ref[...] = _pool_tail(x, d_groups, gate, wg_ref, sc_ref, wo_ref, g_ref, beta_ref)


def _pool_sample(x_tm, state_tm, w_in, w_grp, scale, w_out, ln_g, ln_b, ts, bsz, pos0):
    return pl.pallas_call(
        functools.partial(_pool_sample_kernel, ts=ts, bsz=bsz, pos0=pos0),
        out_shape=[jax.ShapeDtypeStruct((ts * bsz, D_MODEL), F32), jax.ShapeDtypeStruct((ts * bsz, W_POOL), F32)],
        compiler_params=pltpu.CompilerParams(vmem_limit_bytes=VMEM_LIMIT),
        name="pool_sample",
    )(x_tm, state_tm, w_in, w_grp, scale, w_out, ln_g, ln_b)


def kernel(x_prompt, x_sample, cache_kv_cmp, cache_kv_sel, state_win_kv, state_conv, state_pool, page_table,
           w_in_even, w_cmp, conv_w, conv_b, conv_ln_g, conv_ln_b, w_out_even,
           w_in_odd, w_pool_grp, pool_scale, w_out_odd, ln_g, ln_b):
    batch, seq, _ = x_prompt.shape
    bsz, ts, _ = x_sample.shape
    past = page_table.shape[1] * cache_kv_cmp.shape[2]
    kv_shape = (2, KV_HEADS, HEAD_DIM)
    vec = lambda a: a.reshape(1, -1)
    assert seq >= CONV_WIDTH - 1 and seq >= POOL_MAX - 1 and seq >= WINDOW + Q_TILE and seq % K_TILE == 0

    def token_major(x_t):
        n, _, rows = x_t.shape
        return x_t.reshape((n,) + kv_shape + (rows,)).transpose(0, 4, 1, 2, 3)[None]

    w_pad = _pad_even_weights(w_in_even[0])
    cmp_lanes = min(CMP_LANES, seq, past)
    cmp_consts = _compress_consts(w_cmp[0].reshape(CMP_BLOCK, KV_W), cmp_lanes)
    w_oe = w_out_even[0].astype(BF16)
    cw, cb, cg, cbeta = conv_w[0], vec(conv_b[0]), vec(conv_ln_g[0]), vec(conv_ln_b[0])
    g0, b0 = vec(ln_g[0]), vec(ln_b[0])

    xp = x_prompt.reshape(batch * seq, D_MODEL)
    u, sg, q, bg, gt, kvc_t, kvs_t, kvw_t, kaug, vaug, kw, vwa = _proj_prompt(xp, w_pad, batch, seq, 256)
    a_out = _conv_prompt(u, sg, cw, cb, cg, cbeta, batch, seq, 256)
    n_chunks = seq // CMP_STRIDE
    n_sel = -(-seq // SEL_BLOCK)
    kcv = _compress_prompt(kvc_t, *cmp_consts)
    b_raw = _nsa_prompt(q, kaug, vaug, kw, vwa, kcv.reshape(batch, 2 * KV_HEADS, HEAD_DIM, n_chunks), gt,
                        _sel_matrix(n_sel, n_chunks), batch, seq, n_chunks - 1, n_sel)
    xp1 = _mix_out(xp, a_out, b_raw, bg, w_oe, g0, b0, 256)

    n_win_p = min(WINDOW, seq)
    kvc_p = token_major(kvc_t)
    kvs_p = token_major(kvs_t)
    win_p = token_major(kvw_t[:, :, seq - n_win_p:])
    conv_p = u.reshape(batch, seq, W_CONV)[:, seq - (CONV_WIDTH - 1):][None]

    xs = x_sample.transpose(1, 0, 2).reshape(ts * bsz, D_MODEL)
    us, sgs, qs, bgs, gts, kvcs, kvss, kvws = _proj_sample(xs, w_pad)
    conv_ext = jnp.concatenate([state_conv[0].transpose(1, 0, 2), us.reshape(ts, bsz, W_CONV)], axis=0)
    a_out_s = _conv_sample(conv_ext, sgs.reshape(ts, bsz, W_CONV), cw, cb, cg, cbeta).reshape(ts * bsz, W_CONV)
    b_raw_s = _sample_attention(qs, kvss, kvws, gts, cache_kv_cmp[0], cache_kv_sel[0], state_win_kv[0], page_table,
                                cmp_consts, bsz, ts)
    xs1 = _mix_out(xs, a_out_s, b_raw_s, bgs, w_oe, g0, b0, ts * bsz)

    to_bt = lambda a: a.reshape(ts, bsz, -1).transpose(1, 0, 2)
    kvc_s = to_bt(kvcs).reshape((1, bsz, ts) + kv_shape)
    kvs_s = to_bt(kvss).reshape((1, bsz, ts) + kv_shape)
    win_ext = jnp.concatenate([state_win_kv[0], to_bt(kvws).reshape((bsz, ts) + kv_shape)], axis=1)
    w_len = win_ext.shape[1]
    win_s = win_ext[:, w_len - min(WINDOW, w_len):][None]
    conv_s = conv_ext[-(CONV_WIDTH - 1):].transpose(1, 0, 2)[None]

    wi_o = w_in_odd[0].astype(BF16)
    wg_o = w_pool_grp[0].astype(BF16)
    wo_o = w_out_odd[0].astype(BF16)
    sc_o = vec(pool_scale[0])
    g1, b1 = vec(ln_g[1]), vec(ln_b[1])
    n_keep = POOL_MAX - 1
    yp, tail_p = _pool_prompt(xp1, wi_o, wg_o, sc_o, wo_o, g1, b1, batch, seq, 256)
    pool_p = tail_p[:, POOL_HALO - n_keep:][None]
    st_tm = state_pool[0].transpose(1, 0, 2)
    ys, vs_new = _pool_sample(xs1, st_tm, wi_o, wg_o, sc_o, wo_o, g1, b1, ts, bsz, past)
    pool_ext = jnp.concatenate([st_tm, vs_new.reshape(ts, bsz, W_POOL)], axis=0)
    pool_s = pool_ext[-n_keep:].transpose(1, 0, 2)[None]

    y_prompt = yp.reshape(batch, seq, D_MODEL)
    y_sample = ys.reshape(ts, bsz, D_MODEL).transpose(1, 0, 2)
    return (y_prompt, y_sample, kvc_p, kvs_p, win_p, conv_p, pool_p, kvc_s, kvs_s, win_s, conv_s, pool_s)
```

```python
import functools

import jax
import jax.numpy as jnp
import numpy as np
from jax import lax
from jax.experimental import pallas as pl
from jax.experimental.pallas import tpu as pltpu

F32 = jnp.float32
BF16 = jnp.bfloat16

D_MODEL = 1024
W_CONV = 512
CONV_WIDTH = 31
N_HEADS = 8
HEAD_DIM = 64
KV_HEADS = 2
GROUP = N_HEADS // KV_HEADS
W_ATTN = N_HEADS * HEAD_DIM
KV_W = 2 * KV_HEADS * HEAD_DIM
V_OFF = KV_HEADS * HEAD_DIM
CMP_STRIDE = 16
CMP_BLOCK = 2 * CMP_STRIDE
SEL_BLOCK = 64
N_SEL = 16
WINDOW = 512
W_POOL = 1024
POOL_WINDOWS = (2, 4, 8, 16)
POOL_GROUP_W = W_POOL // len(POOL_WINDOWS)
POOL_MAX = 16
LN_EPS = 1e-5
NEG_INF = -1e30
SEL_FORCE = 1e9
DEPTH = 2
DEEPNORM_ALPHA = (2 * DEPTH) ** 0.25
EVEN_SPLITS = (W_CONV, W_CONV, W_CONV, W_ATTN, KV_W, KV_W, KV_W, 3 * N_HEADS, W_ATTN)

LANES = 128
E_PAD = 3 * W_CONV + 2 * W_ATTN + 3 * KV_W + KV_HEADS * LANES
VMEM_LIMIT = 56 * 1024 * 1024
Q_TILE = 128
K_TILE = 512
CMP_LANES = 2048
PER_SEL = SEL_BLOCK // CMP_STRIDE
LOWEST = -3.0e38


def _cparams(sem):
    return pltpu.CompilerParams(dimension_semantics=sem, vmem_limit_bytes=VMEM_LIMIT)


def _round_up(x, m):
    return m * (-(-x // m))


def _sigmoid(x):
    return 1.0 / (1.0 + jnp.exp(-x))


def _silu(x):
    return x * _sigmoid(x)


def _layer_norm(z, g, b):
    mu = jnp.mean(z, axis=-1, keepdims=True)
    zc = z - mu
    var = jnp.mean(zc * zc, axis=-1, keepdims=True)
    return zc * lax.rsqrt(var + LN_EPS) * g + b


def _dot_t(a, b):
    return lax.dot_general(a, b, (((1,), (1,)), ((), ())), preferred_element_type=F32)


def _dot(a, b):
    return jnp.dot(a, b, preferred_element_type=F32)


def _split3(x):
    hi = x.astype(BF16)
    r1 = x - hi.astype(F32)
    mid = r1.astype(BF16)
    lo = (r1 - mid.astype(F32)).astype(BF16)
    return hi, mid, lo


def _shift_left_lanes(x):
    n = x.shape[1]
    col = lax.broadcasted_iota(jnp.int32, x.shape, 1)
    return jnp.where(col < n - 1, pltpu.roll(x, n - 1, 1), 0.0)


KV_COL0 = 3 * W_CONV + W_ATTN


def _proj_common(xb, w_ref, u_ref, sg_ref, q_ref, bg_ref, gt_ref):
    def mm(lo, hi):
        return _dot(xb, w_ref[:, lo:hi])

    o = 0
    a_val = mm(o, o + W_CONV); o += W_CONV
    a_glu = mm(o, o + W_CONV); o += W_CONV
    u_ref[...] = a_val * _sigmoid(a_glu)
    sg_ref[...] = _silu(mm(o, o + W_CONV)); o += W_CONV
    q_ref[...] = (mm(o, o + W_ATTN) * (HEAD_DIM ** -0.5)).astype(BF16); o += W_ATTN
    o += 3 * KV_W
    bg_ref[...] = _silu(mm(o, o + W_ATTN)); o += W_ATTN
    gt_ref[...] = _sigmoid(mm(o, o + KV_HEADS * LANES))


def _proj_sample_kernel(x_ref, w_ref, u_ref, sg_ref, q_ref, bg_ref, gt_ref, kvc_ref, kvs_ref, kvw_ref):
    xb = x_ref[...].astype(BF16)
    _proj_common(xb, w_ref, u_ref, sg_ref, q_ref, bg_ref, gt_ref)
    for k, ref in enumerate((kvc_ref, kvs_ref, kvw_ref)):
        ref[...] = _dot(xb, w_ref[:, KV_COL0 + k * KV_W:KV_COL0 + (k + 1) * KV_W])


def _proj_prompt_kernel(x_ref, w_ref, wkv_t_ref, u_ref, sg_ref, q_ref, bg_ref, gt_ref, kvc_t_ref, kvs_t_ref,
                        kvw_t_ref, kaug_ref, vaug_ref, kw_ref, vwa_ref, *, tm, nt, n_sel):
    xb = x_ref[...].astype(BF16)
    _proj_common(xb, w_ref, u_ref, sg_ref, q_ref, bg_ref, gt_ref)
    kvc_t_ref[0] = _dot_t(wkv_t_ref[0:KV_W, :], xb)
    ks_t = _dot_t(wkv_t_ref[KV_W:2 * KV_W, :], xb)
    kvs_t_ref[0] = ks_t
    kw_t = _dot_t(wkv_t_ref[2 * KV_W:3 * KV_W, :], xb)
    kvw_t_ref[0] = kw_t
    key = (pl.program_id(0) % nt) * tm + lax.broadcasted_iota(jnp.int32, (n_sel, tm), 1)
    blk = lax.broadcasted_iota(jnp.int32, (n_sel, tm), 0)
    onehot = jnp.where(key // SEL_BLOCK == blk, NEG_INF, 0.0).astype(BF16)
    ones_row = jnp.where(lax.broadcasted_iota(jnp.int32, (LANES - HEAD_DIM, tm), 0) == 0, 1.0, 0.0).astype(BF16)
    n_zero = kaug_ref.shape[2] - n_sel - HEAD_DIM
    for h in range(KV_HEADS):
        k_rows = slice(h * HEAD_DIM, (h + 1) * HEAD_DIM)
        v_rows = slice(V_OFF + h * HEAD_DIM, V_OFF + (h + 1) * HEAD_DIM)
        kaug_ref[0, h, 0:n_sel, :] = onehot
        kaug_ref[0, h, n_sel:n_sel + HEAD_DIM, :] = ks_t[k_rows].astype(BF16)
        if n_zero:
            kaug_ref[0, h, n_sel + HEAD_DIM:, :] = jnp.zeros((n_zero, tm), BF16)
        vaug_ref[0, h, 0:HEAD_DIM, :] = ks_t[v_rows].astype(BF16)
        vaug_ref[0, h, HEAD_DIM:, :] = ones_row
        kw_ref[0, h] = kw_t[k_rows].astype(BF16)
        vwa_ref[0, h, 0:HEAD_DIM, :] = kw_t[v_rows].astype(BF16)
        vwa_ref[0, h, HEAD_DIM:, :] = ones_row


def _proj_row_specs(n, tm):
    row = lambda w: pl.BlockSpec((tm, w), lambda i: (i, 0))
    widths = (W_CONV, W_CONV, W_ATTN, W_ATTN, KV_HEADS * LANES)
    dtypes = (F32, F32, BF16, F32, F32)
    return row, [row(w) for w in widths], [jax.ShapeDtypeStruct((n, w), d) for w, d in zip(widths, dtypes)]


def _proj_sample(x2d, w_pad):
    n = x2d.shape[0]
    row, specs, shapes = _proj_row_specs(n, n)
    return pl.pallas_call(
        _proj_sample_kernel,
        grid=(1,),
        in_specs=[row(D_MODEL), pl.BlockSpec((D_MODEL, E_PAD), lambda i: (0, 0))],
        out_specs=specs + [row(KV_W)] * 3,
        out_shape=shapes + [jax.ShapeDtypeStruct((n, KV_W), F32)] * 3,
        compiler_params=_cparams(("arbitrary",)),
        name="proj_sample",
    )(x2d, w_pad)


def _proj_prompt(x2d, w_pad, batch, seq, tm):
    n = x2d.shape[0]
    wkv_t = w_pad[:, KV_COL0:KV_COL0 + 3 * KV_W].T
    nt = seq // tm
    n_sel = -(-seq // SEL_BLOCK)
    k_rows = _round_up(n_sel + HEAD_DIM, LANES)
    row, specs, shapes = _proj_row_specs(n, tm)
    chan = pl.BlockSpec((1, KV_W, tm), lambda i: (i // nt, 0, i % nt))
    head = lambda r: pl.BlockSpec((1, KV_HEADS, r, tm), lambda i: (i // nt, 0, 0, i % nt))
    hshape = lambda r: jax.ShapeDtypeStruct((batch, KV_HEADS, r, seq), BF16)
    return pl.pallas_call(
        functools.partial(_proj_prompt_kernel, tm=tm, nt=nt, n_sel=n_sel),
        grid=(n // tm,),
        in_specs=[row(D_MODEL), pl.BlockSpec((D_MODEL, E_PAD), lambda i: (0, 0)),
                  pl.BlockSpec((3 * KV_W, D_MODEL), lambda i: (0, 0))],
        out_specs=specs + [chan] * 3 + [head(k_rows), head(LANES), head(HEAD_DIM), head(LANES)],
        out_shape=shapes + [jax.ShapeDtypeStruct((batch, KV_W, seq), F32)] * 3 + [
            hshape(k_rows), hshape(LANES), hshape(HEAD_DIM), hshape(LANES)],
        compiler_params=_cparams(("parallel",)),
        name="proj_prompt",
    )(x2d, w_pad, wkv_t)


def _pad_even_weights(w):
    offs = np.cumsum(EVEN_SPLITS)[:-1].tolist()
    a_val, a_glu, a_gate, wq, wkc, wks, wkw, wg, wbg = jnp.split(w, offs, axis=1)
    wg = wg.reshape(D_MODEL, KV_HEADS, GROUP, 3).transpose(0, 1, 3, 2).reshape(D_MODEL, KV_HEADS, 3 * GROUP)
    wg = jnp.pad(wg, ((0, 0), (0, 0), (0, LANES - 3 * GROUP))).reshape(D_MODEL, KV_HEADS * LANES)
    return jnp.concatenate([a_val, a_glu, a_gate, wq, wkc, wks, wkw, wbg, wg], axis=1).astype(BF16)


CONV_HALO = 32
CONV_CHUNK = 32


def _conv_prompt_kernel(u_ref, sg_ref, w_ref, cb_ref, g_ref, b_ref, o_ref, ext_ref, *, tt):
    @pl.when(pl.program_id(1) == 0)
    def _():
        ext_ref[0:CONV_HALO, :] = jnp.zeros((CONV_HALO, W_CONV), F32)

    ext_ref[CONV_HALO:CONV_HALO + tt, :] = u_ref[...]
    base = CONV_HALO - (CONV_WIDTH - 1)
    for c in range(tt // CONV_CHUNK):
        r0 = c * CONV_CHUNK
        acc = jnp.zeros((CONV_CHUNK, W_CONV), F32) + cb_ref[...]
        for k in range(CONV_WIDTH):
            acc = acc + w_ref[k:k + 1, :] * ext_ref[base + r0 + k:base + r0 + k + CONV_CHUNK, :]
        y = _layer_norm(acc, g_ref[...], b_ref[...])
        o_ref[r0:r0 + CONV_CHUNK, :] = _silu(y) * sg_ref[r0:r0 + CONV_CHUNK, :]
    ext_ref[0:CONV_HALO, :] = ext_ref[tt:tt + CONV_HALO, :]


def _conv_prompt(u2d, sg2d, conv_w, conv_b, ln_g, ln_b, batch, seq, tt):
    nt = seq // tt
    row = pl.BlockSpec((tt, W_CONV), lambda b, i: (b * nt + i, 0))
    vec = pl.BlockSpec((1, W_CONV), lambda b, i: (0, 0))
    return pl.pallas_call(
        functools.partial(_conv_prompt_kernel, tt=tt),
        grid=(batch, nt),
        in_specs=[row, row, pl.BlockSpec((CONV_WIDTH, W_CONV), lambda b, i: (0, 0)), vec, vec, vec],
        out_specs=row,
        out_shape=jax.ShapeDtypeStruct((batch * seq, W_CONV), F32),
        scratch_shapes=[pltpu.VMEM((tt + CONV_HALO, W_CONV), F32)],
        compiler_params=_cparams(("arbitrary", "arbitrary")),
        name="conv_prompt",
    )(u2d, sg2d, conv_w, conv_b, ln_g, ln_b)


def _conv_sample_kernel(ext_ref, sg_ref, w_ref, cb_ref, g_ref, b_ref, o_ref, *, ts):
    for t in range(ts):
        acc = jnp.zeros(ext_ref.shape[1:], F32) + cb_ref[...]
        for k in range(CONV_WIDTH):
            acc = acc + w_ref[k:k + 1, :] * ext_ref[t + k]
        y = _layer_norm(acc, g_ref[...], b_ref[...])
        o_ref[t] = _silu(y) * sg_ref[t]


def _conv_sample(ext, sg, conv_w, conv_b, ln_g, ln_b):
    ts, bsz, _ = sg.shape
    return pl.pallas_call(
        functools.partial(_conv_sample_kernel, ts=ts),
        out_shape=jax.ShapeDtypeStruct((ts, bsz, W_CONV), F32),
        name="conv_sample",
    )(ext, sg, conv_w, conv_b, ln_g, ln_b)


def _chunk_sums(x, w1_ref, w2_ref, s_ref):
    reps = x.shape[1] // LANES

    def seg(w_ref):
        y = x * jnp.tile(w_ref[...], (1, reps))
        hi = y.astype(BF16)
        lo = (y - hi.astype(F32)).astype(BF16)
        return _dot(hi, s_ref[...]) + _dot(lo, s_ref[...])

    return seg(w1_ref), seg(w2_ref)


def _compress_prompt_kernel(x_ref, w1_ref, w2_ref, s_ref, o_ref, *, lanes):
    firsts, seconds = [], []
    for c in range(x_ref.shape[2] // lanes):
        a, b = _chunk_sums(x_ref[0, :, c * lanes:(c + 1) * lanes], w1_ref, w2_ref, s_ref)
        firsts.append(a)
        seconds.append(b)
    o_ref[0] = jnp.concatenate(firsts, axis=1) + _shift_left_lanes(jnp.concatenate(seconds, axis=1))


def _compress_prompt(kvc_t, w1t, w2t, seg):
    batch, _, seq = kvc_t.shape
    lanes = seg.shape[0]
    const = lambda a: pl.BlockSpec(a.shape, lambda b: (0, 0))
    return pl.pallas_call(
        functools.partial(_compress_prompt_kernel, lanes=lanes),
        grid=(batch,),
        in_specs=[pl.BlockSpec((1, KV_W, seq), lambda b: (b, 0, 0)), const(w1t), const(w2t), const(seg)],
        out_specs=pl.BlockSpec((1, KV_W, seq // CMP_STRIDE), lambda b: (b, 0, 0)),
        out_shape=jax.ShapeDtypeStruct((batch, KV_W, seq // CMP_STRIDE), F32),
        compiler_params=_cparams(("parallel",)),
        name="compress_prompt",
    )(kvc_t, w1t, w2t, seg)


def _compress_paged_kernel(pt_ref, *refs, n_pages_step):
    del pt_ref
    pages = refs[:n_pages_step]
    w1_ref, w2_ref, s_ref, a_ref, b_ref = refs[n_pages_step:]
    x = jnp.concatenate([p[0] for p in pages], axis=1)
    a_ref[0], b_ref[0] = _chunk_sums(x, w1_ref, w2_ref, s_ref)


def _compress_paged(cache_t, page_table, w1t, w2t, seg):
    _, _, page = cache_t.shape
    bsz, n_pages = page_table.shape
    lanes = seg.shape[0]
    pps = lanes // page
    n_steps = n_pages // pps
    n_out = lanes // CMP_STRIDE

    def page_spec(k):
        return pl.BlockSpec((1, KV_W, page), lambda b, s, pt: (pt[b, s * pps + k], 0, 0))

    const = lambda a: pl.BlockSpec(a.shape, lambda b, s, pt: (0, 0))
    out_spec = pl.BlockSpec((1, KV_W, n_out), lambda b, s, pt: (b, 0, s))
    out_shape = jax.ShapeDtypeStruct((bsz, KV_W, n_steps * n_out), F32)
    grid_spec = pltpu.PrefetchScalarGridSpec(
        num_scalar_prefetch=1,
        grid=(bsz, n_steps),
        in_specs=[page_spec(k) for k in range(pps)] + [const(w1t), const(w2t), const(seg)],
        out_specs=[out_spec, out_spec],
    )
    return pl.pallas_call(
        functools.partial(_compress_paged_kernel, n_pages_step=pps),
        grid_spec=grid_spec,
        out_shape=[out_shape, out_shape],
        compiler_params=_cparams(("parallel", "arbitrary")),
        name="compress_paged",
    )(page_table, *([cache_t] * pps), w1t, w2t, seg)


def _compress_consts(w_cmp2d, lanes):
    w1t = jnp.tile(w_cmp2d[:CMP_STRIDE].T, (1, LANES // CMP_STRIDE))
    w2t = jnp.tile(w_cmp2d[CMP_STRIDE:].T, (1, LANES // CMP_STRIDE))
    seg = jnp.asarray(np.arange(lanes)[:, None] // CMP_STRIDE == np.arange(lanes // CMP_STRIDE)[None, :], BF16)
    return w1t, w2t, seg


def _top_blocks_mask(sc, blk, n_pick):
    n_blocks = sc.shape[0]
    sel = jnp.zeros(sc.shape, F32)
    for _ in range(n_pick):
        mx = jnp.max(sc, axis=0, keepdims=True)
        jm = jnp.min(jnp.where(sc == mx, blk, n_blocks), axis=0, keepdims=True)
        pick = blk == jm
        sel = jnp.where(pick, 1.0, sel)
        sc = jnp.where(pick, LOWEST, sc)
    return sel


def _force_scores(score, blk, cur):
    forced = (blk == 0) | ((blk >= cur - 1) & (blk <= cur))
    return jnp.where(forced, SEL_FORCE, jnp.where(blk <= cur, score, -SEL_FORCE))


def _sel_matrix(n_sel_blocks, n_cols):
    n = np.arange(n_cols)
    m = (n[None, :] // PER_SEL == np.arange(n_sel_blocks)[:, None]) & (n[None, :] % PER_SEL < PER_SEL - 1)
    return jnp.asarray(m, BF16)


def _cmp_valid(n, qpos, n_cmp):
    return n * CMP_STRIDE + (CMP_BLOCK - 1) <= jnp.minimum(qpos, (n_cmp - 1) * CMP_STRIDE + CMP_BLOCK - 1)


def _nsa_prompt_kernel(q_ref, kaug_ref, vaug_ref, kw_ref, vwa_ref, kc_ref, vc_ref, gt_ref, m01_ref, o_ref,
                       qa_ref, acc_ref, mx_ref, s_ref, *, n_cmp, n_sel):
    i = pl.program_id(2)
    rows = GROUP * Q_TILE
    q = q_ref[...]
    qs = jnp.concatenate([q[:, g * HEAD_DIM:(g + 1) * HEAD_DIM] for g in range(GROUP)], axis=0)
    n_cols = kc_ref.shape[3]

    def row_pos(width):
        return i * Q_TILE + (lax.broadcasted_iota(jnp.int32, (rows, width), 0) & (Q_TILE - 1))

    s = _dot(qs, kc_ref[0, 0].astype(BF16))
    valid = _cmp_valid(lax.broadcasted_iota(jnp.int32, (rows, n_cols), 1), row_pos(n_cols), n_cmp)
    s = jnp.where(valid, s, NEG_INF)
    e = jnp.where(valid, jnp.exp(s - jnp.max(s, axis=1, keepdims=True)), 0.0)
    l = jnp.sum(e, axis=1, keepdims=True)
    p = e / jnp.where(l > 0.0, l, 1.0)
    o_c = _dot_t(p.astype(BF16), vc_ref[0, 0].astype(BF16))

    w_keys = WINDOW + Q_TILE
    w0 = pl.multiple_of(jnp.maximum(i * Q_TILE - WINDOW, 0), Q_TILE)
    s_w = _dot(qs, kw_ref[0, 0, :, pl.ds(w0, w_keys)])
    rel = row_pos(w_keys) - (w0 + lax.broadcasted_iota(jnp.int32, (rows, w_keys), 1))
    s_w = jnp.where((rel >= 0) & (rel < WINDOW), s_w, NEG_INF)
    e_w = jnp.exp(s_w - jnp.max(s_w, axis=1, keepdims=True))
    acc_w = _dot_t(e_w.astype(BF16), vwa_ref[0, 0, :, pl.ds(w0, w_keys)])
    o_w = acc_w[:, 0:HEAD_DIM] / acc_w[:, HEAD_DIM:HEAD_DIM + 1]

    p_grp = p[0:Q_TILE]
    for g in range(1, GROUP):
        p_grp = p_grp + p[g * Q_TILE:(g + 1) * Q_TILE]
    score_t = sum(_dot_t(m01_ref[...], part) for part in _split3(p_grp))
    blk = lax.broadcasted_iota(jnp.int32, (n_sel, Q_TILE), 0)
    tok = i * Q_TILE + lax.broadcasted_iota(jnp.int32, (n_sel, Q_TILE), 1)
    sel_t = _top_blocks_mask(_force_scores(score_t, blk, tok // SEL_BLOCK), blk, min(N_SEL, n_sel))
    not_sel = (1.0 - sel_t).T.astype(BF16)
    for g in range(GROUP):
        qa_ref[g * Q_TILE:(g + 1) * Q_TILE, 0:n_sel] = not_sel
    qa_ref[:, n_sel:n_sel + HEAD_DIM] = qs
    if qa_ref.shape[1] > n_sel + HEAD_DIM:
        qa_ref[:, n_sel + HEAD_DIM:] = jnp.zeros((rows, qa_ref.shape[1] - n_sel - HEAD_DIM), BF16)

    mx_ref[...] = jnp.full(mx_ref.shape, NEG_INF, F32)

    def score_tiles(j, n_tiles, causal_last):
        for t in range(n_tiles):
            k0 = pl.multiple_of((j + t) * K_TILE, K_TILE)
            s = _dot(qa_ref[...], kaug_ref[0, 0, :, pl.ds(k0, K_TILE)])
            if causal_last and t == n_tiles - 1:
                kpos = k0 + lax.broadcasted_iota(jnp.int32, (rows, K_TILE), 1)
                s = jnp.where(kpos <= row_pos(K_TILE), s, NEG_INF)
            s_ref[j + t] = s
            part = s[:, 0:LANES]
            for c in range(1, K_TILE // LANES):
                part = jnp.maximum(part, s[:, c * LANES:(c + 1) * LANES])
            mx_ref[...] = jnp.maximum(mx_ref[...], part)

    def acc_tiles(j, n_tiles):
        m_b = jnp.tile(mx_ref[...], (1, K_TILE // LANES))
        for t in range(n_tiles):
            k0 = pl.multiple_of((j + t) * K_TILE, K_TILE)
            pe = jnp.exp(s_ref[j + t] - m_b).astype(BF16)
            acc_ref[...] += _dot_t(pe, vaug_ref[0, 0, :, pl.ds(k0, K_TILE)])

    def pairs(n, fn):
        def body(jj, carry):
            fn(2 * jj, 2)
            return carry
        lax.fori_loop(0, n // 2, body, 0)

    n_full = (i * Q_TILE) // K_TILE
    pairs(n_full, lambda j, n: score_tiles(j, n, False))

    @pl.when(n_full % 2 == 1)
    def _():
        score_tiles(n_full - 1, 2, True)

    @pl.when(n_full % 2 == 0)
    def _():
        score_tiles(n_full, 1, True)

    mx_ref[...] = jnp.broadcast_to(jnp.max(mx_ref[...], axis=1, keepdims=True), mx_ref.shape)
    acc_ref[...] = jnp.zeros(acc_ref.shape, F32)
    pairs(n_full + 1, acc_tiles)

    @pl.when(n_full % 2 == 0)
    def _():
        acc_tiles(n_full, 1)

    acc = acc_ref[...]
    o_s = acc[:, 0:HEAD_DIM] / acc[:, HEAD_DIM:HEAD_DIM + 1]

    gt = gt_ref[...]
    outs = []
    for g in range(GROUP):
        r = slice(g * Q_TILE, (g + 1) * Q_TILE)
        outs.append(gt[:, g:g + 1] * o_c[r] + gt[:, GROUP + g:GROUP + g + 1] * o_s[r]
                    + gt[:, 2 * GROUP + g:2 * GROUP + g + 1] * o_w[r])
    o_ref[...] = jnp.concatenate(outs, axis=1)


def _nsa_prompt(q2d, kaug, vaug, kw, vwa, kcv4, gt2d, m01, batch, seq, n_cmp, n_sel):
    nq = seq // Q_TILE
    rows = GROUP * Q_TILE
    n_cols = kcv4.shape[-1]
    per_head = lambda a: pl.BlockSpec((1, 1) + a.shape[2:], lambda b, h, i: (b, h, 0, 0))
    return pl.pallas_call(
        functools.partial(_nsa_prompt_kernel, n_cmp=n_cmp, n_sel=n_sel),
        grid=(batch, KV_HEADS, nq),
        in_specs=[pl.BlockSpec((Q_TILE, GROUP * HEAD_DIM), lambda b, h, i: (b * nq + i, h)),
                  per_head(kaug), per_head(vaug), per_head(kw), per_head(vwa),
                  pl.BlockSpec((1, 1, HEAD_DIM, n_cols), lambda b, h, i: (b, h, 0, 0)),
                  pl.BlockSpec((1, 1, HEAD_DIM, n_cols), lambda b, h, i: (b, KV_HEADS + h, 0, 0)),
                  pl.BlockSpec((Q_TILE, LANES), lambda b, h, i: (b * nq + i, h)),
                  pl.BlockSpec(m01.shape, lambda b, h, i: (0, 0))],
        out_specs=pl.BlockSpec((Q_TILE, GROUP * HEAD_DIM), lambda b, h, i: (b * nq + i, h)),
        out_shape=jax.ShapeDtypeStruct((batch * seq, W_ATTN), F32),
        scratch_shapes=[pltpu.VMEM((rows, kaug.shape[2]), BF16), pltpu.VMEM((rows, LANES), F32),
                        pltpu.VMEM((rows, LANES), F32), pltpu.VMEM((seq // K_TILE, rows, K_TILE), F32)],
        compiler_params=_cparams(("parallel", "parallel", "arbitrary")),
        name="nsa_prompt",
    )(q2d, kaug, vaug, kw, vwa, kcv4, kcv4, gt2d, m01)


def _new_token_scores(qf, nk, tq, ts):
    cols = []
    for t in range(ts):
        s = jnp.sum(qf * nk[t:t + 1, :], axis=1, keepdims=True)
        cols.append(jnp.where(tq >= t, s, NEG_INF))
    return cols


def _bf16_round(x):
    return x.astype(BF16).astype(F32)


def _nsa_sample_a_kernel(q_ref, a_ref, b_ref, win_ref, new_ref, gt_ref, m01_ref, part_ref, idx_ref, *,
                         past, ts, n_cmp):
    rows = GROUP * ts
    kcv = a_ref[0] + _shift_left_lanes(b_ref[0])
    win = win_ref[0]
    new = new_ref[0]
    n_cols = kcv.shape[1]
    w_hist = win.shape[1]
    n_blk_pad = m01_ref.shape[0]
    tq = lax.broadcasted_iota(jnp.int32, (rows, 1), 0) % ts
    qpos = past + tq
    scores = []
    for h in range(KV_HEADS):
        qh = q_ref[0, h]
        ksl = slice(h * HEAD_DIM, (h + 1) * HEAD_DIM)
        vsl = slice(V_OFF + h * HEAD_DIM, V_OFF + (h + 1) * HEAD_DIM)
        s = _dot(qh, kcv[ksl].astype(BF16))
        valid = _cmp_valid(lax.broadcasted_iota(jnp.int32, (rows, n_cols), 1), qpos, n_cmp)
        s = jnp.where(valid, s, NEG_INF)
        e = jnp.where(valid, jnp.exp(s - jnp.max(s, axis=1, keepdims=True)), 0.0)
        l = jnp.sum(e, axis=1, keepdims=True)
        p = e / jnp.where(l > 0.0, l, 1.0)
        o_c = _dot_t(p.astype(BF16), kcv[vsl].astype(BF16))
        p_grp = p[0:ts]
        for g in range(1, GROUP):
            p_grp = p_grp + p[g * ts:(g + 1) * ts]
        scores.append(sum(_dot_t(part, m01_ref[...]) for part in _split3(p_grp)))

        s1 = _dot(qh, win[ksl].astype(BF16))
        rel = qpos - (past - w_hist + lax.broadcasted_iota(jnp.int32, (rows, w_hist), 1))
        valid = (rel >= 0) & (rel <= jnp.minimum(qpos, WINDOW - 1))
        s1 = jnp.where(valid, s1, NEG_INF)
        s2 = _new_token_scores(qh.astype(F32), _bf16_round(new[:, ksl]), tq, ts)
        m = jnp.max(s1, axis=1, keepdims=True)
        for c in s2:
            m = jnp.maximum(m, c)
        e1 = jnp.where(valid, jnp.exp(s1 - m), 0.0)
        l = jnp.sum(e1, axis=1, keepdims=True)
        o_w = _dot_t(e1.astype(BF16), win[vsl].astype(BF16))
        nv = _bf16_round(new[:, vsl])
        for t in range(ts):
            e2 = jnp.exp(s2[t] - m)
            l = l + e2
            o_w = o_w + _bf16_round(e2) * nv[t:t + 1, :]
        gt = gt_ref[0, h]
        part_ref[0, h] = gt[:, 0:1] * o_c + gt[:, 2:3] * (o_w / l)

    sc = jnp.concatenate(scores, axis=0)
    nr = KV_HEADS * ts
    blk = lax.broadcasted_iota(jnp.int32, (nr, n_blk_pad), 1)
    cur = (past + (lax.broadcasted_iota(jnp.int32, (nr, 1), 0) % ts)) // SEL_BLOCK
    sc = _force_scores(sc, blk, cur)
    lane = lax.broadcasted_iota(jnp.int32, (nr, LANES), 1)
    idx = jnp.zeros((nr, LANES), jnp.int32)
    for k in range(N_SEL):
        mx = jnp.max(sc, axis=1, keepdims=True)
        jm = jnp.min(jnp.where(sc == mx, blk, n_blk_pad), axis=1, keepdims=True)
        idx = jnp.where(lane == k, jm, idx)
        sc = jnp.where(blk == jm, LOWEST, sc)
    idx_ref[0] = idx


def _nsa_sample_a(q4, a_t, b_t, win_t, new_w, gt4, m01, past, ts, n_cmp):
    bsz = q4.shape[0]
    rows = GROUP * ts
    per_b = lambda a: pl.BlockSpec((1,) + a.shape[1:], lambda b: (b,) + (0,) * (a.ndim - 1))
    return pl.pallas_call(
        functools.partial(_nsa_sample_a_kernel, past=past, ts=ts, n_cmp=n_cmp),
        grid=(bsz,),
        in_specs=[per_b(q4), per_b(a_t), per_b(b_t), per_b(win_t), per_b(new_w), per_b(gt4),
                  pl.BlockSpec(m01.shape, lambda b: (0, 0))],
        out_specs=[pl.BlockSpec((1, KV_HEADS, rows, HEAD_DIM), lambda b: (b, 0, 0, 0)),
                   pl.BlockSpec((1, KV_HEADS * ts, LANES), lambda b: (b, 0, 0))],
        out_shape=[jax.ShapeDtypeStruct((bsz, KV_HEADS, rows, HEAD_DIM), F32),
                   jax.ShapeDtypeStruct((bsz, KV_HEADS * ts, LANES), jnp.int32)],
        compiler_params=_cparams(("parallel",)),
        name="nsa_sample_scores",
    )(q4, a_t, b_t, win_t, new_w, gt4, m01)


def _nsa_sample_b_kernel(idx_ref, pt_ref, *refs, past, ts, n_past_blocks, page):
    del pt_ref
    k_blocks = refs[:N_SEL]
    v_blocks = refs[N_SEL:2 * N_SEL]
    q_ref, new_ref, gt_ref, part_ref, o_ref = refs[2 * N_SEL:]
    r = pl.program_id(0)
    t = r % ts
    head0 = ((r // ts) % KV_HEADS) == 0
    bpp = page // SEL_BLOCK
    q = q_ref[0]
    lane = lax.broadcasted_iota(jnp.int32, (GROUP, page), 1)

    def pick(x, off):
        return jnp.where(head0, x[:, off:off + HEAD_DIM], x[:, off + HEAD_DIM:off + 2 * HEAD_DIM])

    s_all = _dot(q, jnp.concatenate([k_blocks[k][0].astype(BF16) for k in range(N_SEL)], axis=1))
    s_list = []
    has_new = False
    for k in range(N_SEL):
        j = idx_ref[r * N_SEL + k]
        is_past = j < n_past_blocks
        has_new = jnp.logical_or(has_new, jnp.logical_not(is_past))
        off = (j % bpp) * SEL_BLOCK
        page_start = (j // bpp) * page
        lo = jnp.where(is_past, off, page)
        hi = jnp.minimum(off + SEL_BLOCK, past + t + 1 - page_start)
        s_list.append(jnp.where((lane >= lo) & (lane < hi), s_all[:, k * page:(k + 1) * page], NEG_INF))
    new = new_ref[0]
    tq = jnp.where(has_new, t, -1) + jnp.zeros((GROUP, 1), jnp.int32)
    s_new = _new_token_scores(q.astype(F32), _bf16_round(pick(new, 0)), tq, ts)
    m = s_new[0]
    for c in s_new[1:]:
        m = jnp.maximum(m, c)
    for s in s_list:
        m = jnp.maximum(m, jnp.max(s, axis=1, keepdims=True))
    nv = _bf16_round(pick(new, V_OFF))
    l = jnp.zeros((GROUP, 1), F32)
    o = jnp.zeros((GROUP, HEAD_DIM), F32)
    for tt in range(ts):
        e2 = jnp.exp(s_new[tt] - m)
        l = l + e2
        o = o + _bf16_round(e2) * nv[tt:tt + 1, :]
    e_all = jnp.exp(jnp.concatenate(s_list, axis=1) - m)
    l = l + jnp.sum(e_all, axis=1, keepdims=True)
    o = o + _dot_t(e_all.astype(BF16),
                   jnp.concatenate([v_blocks[k][0].astype(BF16) for k in range(N_SEL)], axis=1))
    gt = gt_ref[0]
    o_ref[0] = part_ref[0] + gt[:, 1:2] * (o / l)


def _nsa_sample_b(idx_flat, pt_flat, cache_slabs, q_rows, new_rows, gt_rows, part_rows, past, ts, n_pages, page):
    n_rows = q_rows.shape[0]
    n_past_blocks = past // SEL_BLOCK
    bpp = page // SEL_BLOCK
    slabs_per_page = 2 * KV_HEADS

    def slab_spec(k, is_v):
        def imap(r, idx, pt):
            j = jnp.minimum(idx[r * N_SEL + k], n_past_blocks - 1)
            b = r // (KV_HEADS * ts)
            h = (r // ts) % KV_HEADS
            return (pt[b * n_pages + j // bpp] * slabs_per_page + is_v * KV_HEADS + h, 0, 0)
        return pl.BlockSpec((1, HEAD_DIM, page), imap)

    row3 = lambda a: pl.BlockSpec((1,) + a.shape[1:], lambda r, idx, pt: (r, 0, 0))
    grid_spec = pltpu.PrefetchScalarGridSpec(
        num_scalar_prefetch=2,
        grid=(n_rows,),
        in_specs=[slab_spec(k, 0) for k in range(N_SEL)] + [slab_spec(k, 1) for k in range(N_SEL)] + [
            row3(q_rows),
            pl.BlockSpec((1,) + new_rows.shape[1:], lambda r, idx, pt: (r // (KV_HEADS * ts), 0, 0)),
            row3(gt_rows), row3(part_rows)],
        out_specs=row3(part_rows),
    )
    return pl.pallas_call(
        functools.partial(_nsa_sample_b_kernel, past=past, ts=ts, n_past_blocks=n_past_blocks, page=page),
        grid_spec=grid_spec,
        out_shape=jax.ShapeDtypeStruct(part_rows.shape, F32),
        compiler_params=_cparams(("arbitrary",)),
        name="nsa_sample_select",
    )(idx_flat, pt_flat, *([cache_slabs] * (2 * N_SEL)), q_rows, new_rows, gt_rows, part_rows)


def _channel_major(x):
    n, rows = x.shape[:2]
    return x.transpose(0, 2, 3, 4, 1).reshape(n, KV_W, rows)


def _sample_attention(q_tm, kvs_tm, kvw_tm, gt_tm, cache_c, cache_s, state_win, page_table, cmp_consts, bsz, ts):
    n_phys, page = cache_c.shape[0], cache_c.shape[1]
    n_pages = page_table.shape[1]
    past = n_pages * page
    w_hist = state_win.shape[1]
    assert ts < CMP_STRIDE and page % SEL_BLOCK == 0 and ts <= WINDOW and w_hist <= past
    total = past + ts
    n_cmp = total // CMP_STRIDE - 1
    n_sel_blocks = -(-total // SEL_BLOCK)
    a_t, b_t = _compress_paged(_channel_major(cache_c), page_table, *cmp_consts)
    m01 = _sel_matrix(_round_up(n_sel_blocks, LANES), a_t.shape[2])
    q4 = q_tm.reshape(ts, bsz, KV_HEADS, GROUP, HEAD_DIM).transpose(1, 2, 3, 0, 4).reshape(
        bsz, KV_HEADS, GROUP * ts, HEAD_DIM)
    g5 = gt_tm.reshape(ts, bsz, KV_HEADS, LANES)[..., :3 * GROUP].reshape(ts, bsz, KV_HEADS, 3, GROUP)
    g5 = g5.transpose(1, 2, 4, 0, 3).reshape(bsz, KV_HEADS, GROUP * ts, 3)
    gt4 = jnp.pad(g5, ((0, 0), (0, 0), (0, 0), (0, LANES - 3)))
    to_bt = lambda a: a.reshape(ts, bsz, KV_W).transpose(1, 0, 2)
    part, idx = _nsa_sample_a(q4, a_t, b_t, _channel_major(state_win), to_bt(kvw_tm), gt4, m01, past, ts, n_cmp)
    to_rows = lambda a: a.reshape(bsz, KV_HEADS, GROUP, ts, a.shape[-1]).transpose(0, 1, 3, 2, 4).reshape(
        bsz * KV_HEADS * ts, GROUP, a.shape[-1])
    slabs = _channel_major(cache_s).reshape(n_phys * 2 * KV_HEADS, HEAD_DIM, page)
    o_rows = _nsa_sample_b(idx[:, :, :N_SEL].reshape(-1), page_table.reshape(-1), slabs,
                           to_rows(q4), to_bt(kvs_tm), to_rows(gt4), to_rows(part), past, ts, n_pages, page)
    return o_rows.reshape(bsz, KV_HEADS, ts, GROUP, HEAD_DIM).transpose(2, 0, 1, 3, 4).reshape(ts * bsz, W_ATTN)


def _mix_out_kernel(x_ref, a_ref, b_ref, bg_ref, w_ref, g_ref, beta_ref, o_ref):
    a = a_ref[...].astype(BF16)
    b = (b_ref[...] * bg_ref[...]).astype(BF16)
    d = _dot(a, w_ref[0:W_CONV, :]) + _dot(b, w_ref[W_CONV:W_CONV + W_ATTN, :])
    o_ref[...] = _layer_norm(DEEPNORM_ALPHA * x_ref[...] + d, g_ref[...], beta_ref[...])


def _mix_out(x2d, a2d, b2d, bg2d, w_out, ln_g, ln_b, tm):
    n = x2d.shape[0]
    row = lambda w: pl.BlockSpec((tm, w), lambda i: (i, 0))
    vec = pl.BlockSpec((1, D_MODEL), lambda i: (0, 0))
    return pl.pallas_call(
        _mix_out_kernel,
        grid=(n // tm,),
        in_specs=[row(D_MODEL), row(W_CONV), row(W_ATTN), row(W_ATTN),
                  pl.BlockSpec((W_CONV + W_ATTN, D_MODEL), lambda i: (0, 0)), vec, vec],
        out_specs=row(D_MODEL),
        out_shape=jax.ShapeDtypeStruct((n, D_MODEL), F32),
        compiler_params=_cparams(("parallel",)),
        name="mix_out",
    )(x2d, a2d, b2d, bg2d, w_out, ln_g, ln_b)


POOL_HALO = 16


def _pool_tail(x, d_groups, gate, wg_ref, sc_ref, wo_ref, g_ref, beta_ref):
    mixed = jnp.concatenate([_dot(d_groups[gi].astype(BF16), wg_ref[gi]) for gi in range(len(POOL_WINDOWS))], axis=1)
    h = (mixed * sc_ref[...] * _silu(gate)).astype(BF16)
    return _layer_norm(DEEPNORM_ALPHA * x + _dot(h, wo_ref[...]), g_ref[...], beta_ref[...])


def _pool_prompt_kernel(x_ref, wi_ref, wg_ref, sc_ref, wo_ref, g_ref, beta_ref, o_ref, tail_ref, ext_ref, *, tm):
    i = pl.program_id(1)

    @pl.when(i == 0)
    def _():
        ext_ref[0:POOL_HALO, :] = jnp.zeros((POOL_HALO, W_POOL), F32)

    x = x_ref[...]
    xb = x.astype(BF16)
    v = _dot(xb, wi_ref[:, 0:W_POOL])
    gate = _dot(xb, wi_ref[:, W_POOL:2 * W_POOL])
    ext_ref[POOL_HALO:POOL_HALO + tm, :] = v
    pos = i * tm + lax.broadcasted_iota(jnp.int32, (tm, 1), 0)
    d_groups = []
    for gi, w in enumerate(POOL_WINDOWS):
        c = slice(gi * POOL_GROUP_W, (gi + 1) * POOL_GROUP_W)
        win_sum = v[:, c]
        for k in range(1, w):
            win_sum = win_sum + ext_ref[POOL_HALO - k:POOL_HALO - k + tm, c]
        cnt = jnp.minimum(pos + 1, w).astype(F32)
        d_groups.append(win_sum / cnt - v[:, c])
    o_ref[...] = _pool_tail(x, d_groups, gate, wg_ref, sc_ref, wo_ref, g_ref, beta_ref)
    ext_ref[0:POOL_HALO, :] = ext_ref[tm:tm + POOL_HALO, :]
    tail_ref[0] = ext_ref[0:POOL_HALO, :]


def _pool_prompt(x2d, w_in, w_grp, scale, w_out, ln_g, ln_b, batch, seq, tm):
    nt = seq // tm
    const = lambda a: pl.BlockSpec(a.shape, lambda b, i: (0,) * a.ndim)
    row = pl.BlockSpec((tm, D_MODEL), lambda b, i: (b * nt + i, 0))
    return pl.pallas_call(
        functools.partial(_pool_prompt_kernel, tm=tm),
        grid=(batch, nt),
        in_specs=[row, const(w_in), const(w_grp), const(scale), const(w_out), const(ln_g), const(ln_b)],
        out_specs=[row, pl.BlockSpec((1, POOL_HALO, W_POOL), lambda b, i: (b, 0, 0))],
        out_shape=[jax.ShapeDtypeStruct((batch * seq, D_MODEL), F32),
                   jax.ShapeDtypeStruct((batch, POOL_HALO, W_POOL), F32)],
        scratch_shapes=[pltpu.VMEM((tm + POOL_HALO, W_POOL), F32)],
        compiler_params=_cparams(("arbitrary", "arbitrary")),
        name="pool_prompt",
    )(x2d, w_in, w_grp, scale, w_out, ln_g, ln_b)


def _pool_sample_kernel(x_ref, st_ref, wi_ref, wg_ref, sc_ref, wo_ref, g_ref, beta_ref, o_ref, v_ref, *,
                        ts, bsz, pos0):
    x = x_ref[...]
    xb = x.astype(BF16)
    v = _dot(xb, wi_ref[:, 0:W_POOL])
    gate = _dot(xb, wi_ref[:, W_POOL:2 * W_POOL])
    v_ref[...] = v
    n_hist = st_ref.shape[0]

    def ext(r):
        return st_ref[r] if r < n_hist else v[(r - n_hist) * bsz:(r - n_hist + 1) * bsz, :]

    d_groups = []
    for gi, w in enumerate(POOL_WINDOWS):
        c = slice(gi * POOL_GROUP_W, (gi + 1) * POOL_GROUP_W)
        per_t = []
        for t in range(ts):
            win_sum = ext(n_hist + t)[:, c]
            for k in range(1, w):
                win_sum = win_sum + ext(n_hist + t - k)[:, c]
            per_t.append(win_sum / float(min(pos0 + t + 1, w)) - ext(n_hist + t)[:, c])
        d_groups.append(jnp.concatenate(per_t, axis=0))
    o_ref[...] = _pool_tail(x, d_groups, gate, wg_ref, sc_ref, wo_ref, g_ref, beta_ref)


def _pool_sample(x_tm, state_tm, w_in, w_grp, scale, w_out, ln_g, ln_b, ts, bsz, pos0):
    return pl.pallas_call(
        functools.partial(_pool_sample_kernel, ts=ts, bsz=bsz, pos0=pos0),
        out_shape=[jax.ShapeDtypeStruct((ts * bsz, D_MODEL), F32), jax.ShapeDtypeStruct((ts * bsz, W_POOL), F32)],
        compiler_params=pltpu.CompilerParams(vmem_limit_bytes=VMEM_LIMIT),
        name="pool_sample",
    )(x_tm, state_tm, w_in, w_grp, scale, w_out, ln_g, ln_b)


def kernel(x_prompt, x_sample, cache_kv_cmp, cache_kv_sel, state_win_kv, state_conv, state_pool, page_table,
           w_in_even, w_cmp, conv_w, conv_b, conv_ln_g, conv_ln_b, w_out_even,
           w_in_odd, w_pool_grp, pool_scale, w_out_odd, ln_g, ln_b):
    batch, seq, _ = x_prompt.shape
    bsz, ts, _ = x_sample.shape
    past = page_table.shape[1] * cache_kv_cmp.shape[2]
    kv_shape = (2, KV_HEADS, HEAD_DIM)
    vec = lambda a: a.reshape(1, -1)
    assert seq >= CONV_WIDTH - 1 and seq >= POOL_MAX - 1 and seq >= WINDOW + Q_TILE and seq % K_TILE == 0

    def token_major(x_t):
        n, _, rows = x_t.shape
        return x_t.reshape((n,) + kv_shape + (rows,)).transpose(0, 4, 1, 2, 3)[None]

    w_pad = _pad_even_weights(w_in_even[0])
    cmp_lanes = min(CMP_LANES, seq, past)
    cmp_consts = _compress_consts(w_cmp[0].reshape(CMP_BLOCK, KV_W), cmp_lanes)
    w_oe = w_out_even[0].astype(BF16)
    cw, cb, cg, cbeta = conv_w[0], vec(conv_b[0]), vec(conv_ln_g[0]), vec(conv_ln_b[0])
    g0, b0 = vec(ln_g[0]), vec(ln_b[0])

    xp = x_prompt.reshape(batch * seq, D_MODEL)
    u, sg, q, bg, gt, kvc_t, kvs_t, kvw_t, kaug, vaug, kw, vwa = _proj_prompt(xp, w_pad, batch, seq, 256)
    a_out = _conv_prompt(u, sg, cw, cb, cg, cbeta, batch, seq, 256)
    n_chunks = seq // CMP_STRIDE
    n_sel = -(-seq // SEL_BLOCK)
    kcv = _compress_prompt(kvc_t, *cmp_consts)
    b_raw = _nsa_prompt(q, kaug, vaug, kw, vwa, kcv.reshape(batch, 2 * KV_HEADS, HEAD_DIM, n_chunks), gt,
                        _sel_matrix(n_sel, n_chunks), batch, seq, n_chunks - 1, n_sel)
    xp1 = _mix_out(xp, a_out, b_raw, bg, w_oe, g0, b0, 256)

    n_win_p = min(WINDOW, seq)
    kvc_p = token_major(kvc_t)
    kvs_p = token_major(kvs_t)
    win_p = token_major(kvw_t[:, :, seq - n_win_p:])
    conv_p = u.reshape(batch, seq, W_CONV)[:, seq - (CONV_WIDTH - 1):][None]

    xs = x_sample.transpose(1, 0, 2).reshape(ts * bsz, D_MODEL)
    us, sgs, qs, bgs, gts, kvcs, kvss, kvws = _proj_sample(xs, w_pad)
    conv_ext = jnp.concatenate([state_conv[0].transpose(1, 0, 2), us.reshape(ts, bsz, W_CONV)], axis=0)
    a_out_s = _conv_sample(conv_ext, sgs.reshape(ts, bsz, W_CONV), cw, cb, cg, cbeta).reshape(ts * bsz, W_CONV)
    b_raw_s = _sample_attention(qs, kvss, kvws, gts, cache_kv_cmp[0], cache_kv_sel[0], state_win_kv[0], page_table,
                                cmp_consts, bsz, ts)
    xs1 = _mix_out(xs, a_out_s, b_raw_s, bgs, w_oe, g0, b0, ts * bsz)

    to_bt = lambda a: a.reshape(ts, bsz, -1).transpose(1, 0, 2)
    kvc_s = to_bt(kvcs).reshape((1, bsz, ts) + kv_shape)
    kvs_s = to_bt(kvss).reshape((1, bsz, ts) + kv_shape)
    win_ext = jnp.concatenate([state_win_kv[0], to_bt(kvws).reshape((bsz, ts) + kv_shape)], axis=1)
    w_len = win_ext.shape[1]
    win_s = win_ext[:, w_len - min(WINDOW, w_len):][None]
    conv_s = conv_ext[-(CONV_WIDTH - 1):].transpose(1, 0, 2)[None]

    wi_o = w_in_odd[0].astype(BF16)
    wg_o = w_pool_grp[0].astype(BF16)
    wo_o = w_out_odd[0].astype(BF16)
    sc_o = vec(pool_scale[0])
    g1, b1 = vec(ln_g[1]), vec(ln_b[1])
    n_keep = POOL_MAX - 1
    yp, tail_p = _pool_prompt(xp1, wi_o, wg_o, sc_o, wo_o, g1, b1, batch, seq, 256)
    pool_p = tail_p[:, POOL_HALO - n_keep:][None]
    st_tm = state_pool[0].transpose(1, 0, 2)
    ys, vs_new = _pool_sample(xs1, st_tm, wi_o, wg_o, sc_o, wo_o, g1, b1, ts, bsz, past)
    pool_ext = jnp.concatenate([st_tm, vs_new.reshape(ts, bsz, W_POOL)], axis=0)
    pool_s = pool_ext[-n_keep:].transpose(1, 0, 2)[None]

    y_prompt = yp.reshape(batch, seq, D_MODEL)
    y_sample = ys.reshape(ts, bsz, D_MODEL).transpose(1, 0, 2)
    return (y_prompt, y_sample, kvc_p, kvs_p, win_p, conv_p, pool_p, kvc_s, kvs_s, win_s, conv_s, pool_s)
```

```python
import functools

import jax
import jax.numpy as jnp
import numpy as np
from jax import lax
from jax.experimental import pallas as pl
from jax.experimental.pallas import tpu as pltpu

F32 = jnp.float32
BF16 = jnp.bfloat16

D_MODEL = 1024
W_CONV = 512
CONV_WIDTH = 31
N_HEADS = 8
HEAD_DIM = 64
KV_HEADS = 2
GROUP = N_HEADS // KV_HEADS
W_ATTN = N_HEADS * HEAD_DIM
KV_W = 2 * KV_HEADS * HEAD_DIM
V_OFF = KV_HEADS * HEAD_DIM
CMP_STRIDE = 16
CMP_BLOCK = 2 * CMP_STRIDE
SEL_BLOCK = 64
N_SEL = 16
WINDOW = 512
W_POOL = 1024
POOL_WINDOWS = (2, 4, 8, 16)
POOL_GROUP_W = W_POOL // len(POOL_WINDOWS)
POOL_MAX = 16
LN_EPS = 1e-5
NEG_INF = -1e30
SEL_FORCE = 1e9
DEPTH = 2
DEEPNORM_ALPHA = (2 * DEPTH) ** 0.25
EVEN_SPLITS = (W_CONV, W_CONV, W_CONV, W_ATTN, KV_W, KV_W, KV_W, 3 * N_HEADS, W_ATTN)

LANES = 128
SUBLANES = 8
E_PAD = 3 * W_CONV + 2 * W_ATTN + 3 * KV_W + KV_HEADS * LANES
VMEM_LIMIT = 56 * 1024 * 1024
Q_TILE = 128
K_TILE = 512
TILES_PER_TRIP = 4
CMP_LANES = 2048
PER_SEL = SEL_BLOCK // CMP_STRIDE
LOWEST = -3.0e38


def _cparams(sem):
    return pltpu.CompilerParams(dimension_semantics=sem, vmem_limit_bytes=VMEM_LIMIT)


def _round_up(x, m):
    return m * (-(-x // m))


def _sigmoid(x):
    return 1.0 / (1.0 + jnp.exp(-x))


def _silu(x):
    return x * _sigmoid(x)


def _layer_norm(z, g, b):
    mu = jnp.mean(z, axis=-1, keepdims=True)
    zc = z - mu
    var = jnp.mean(zc * zc, axis=-1, keepdims=True)
    return zc * lax.rsqrt(var + LN_EPS) * g + b


def _dot_t(a, b):
    return lax.dot_general(a, b, (((1,), (1,)), ((), ())), preferred_element_type=F32)


def _dot(a, b):
    return jnp.dot(a, b, preferred_element_type=F32)


def _split3(x):
    hi = x.astype(BF16)
    r1 = x - hi.astype(F32)
    mid = r1.astype(BF16)
    lo = (r1 - mid.astype(F32)).astype(BF16)
    return hi, mid, lo


def _shift_left_lanes(x):
    n = x.shape[1]
    col = lax.broadcasted_iota(jnp.int32, x.shape, 1)
    return jnp.where(col < n - 1, pltpu.roll(x, n - 1, 1), 0.0)


KV_COL0 = 3 * W_CONV + W_ATTN


def _proj_common(xb, w_ref, u_ref, sg_ref, q_ref, bg_ref, gt_ref):
    def mm(lo, hi):
        return _dot(xb, w_ref[:, lo:hi])

    o = 0
    a_val = mm(o, o + W_CONV); o += W_CONV
    a_glu = mm(o, o + W_CONV); o += W_CONV
    u_ref[...] = a_val * _sigmoid(a_glu)
    sg_ref[...] = _silu(mm(o, o + W_CONV)); o += W_CONV
    q_ref[...] = (mm(o, o + W_ATTN) * (HEAD_DIM ** -0.5)).astype(BF16); o += W_ATTN
    o += 3 * KV_W
    bg_ref[...] = _silu(mm(o, o + W_ATTN)); o += W_ATTN
    gt_ref[...] = _sigmoid(mm(o, o + KV_HEADS * LANES))


def _proj_sample_kernel(x_ref, w_ref, u_ref, sg_ref, q_ref, bg_ref, gt_ref, kvc_ref, kvs_ref, kvw_ref):
    xb = x_ref[...].astype(BF16)
    _proj_common(xb, w_ref, u_ref, sg_ref, q_ref, bg_ref, gt_ref)
    for k, ref in enumerate((kvc_ref, kvs_ref, kvw_ref)):
        ref[...] = _dot(xb, w_ref[:, KV_COL0 + k * KV_W:KV_COL0 + (k + 1) * KV_W])


def _proj_prompt_kernel(x_ref, w_ref, wkv_t_ref, u_ref, sg_ref, q_ref, bg_ref, gt_ref, kvc_t_ref, kvs_t_ref,
                        kvw_t_ref, kaug_ref, vaug_ref, kw_ref, vwa_ref, *, tm, nt, n_sel):
    xb = x_ref[...].astype(BF16)
    _proj_common(xb, w_ref, u_ref, sg_ref, q_ref, bg_ref, gt_ref)
    kvc_t_ref[0] = _dot_t(wkv_t_ref[0:KV_W, :], xb)
    ks_t = _dot_t(wkv_t_ref[KV_W:2 * KV_W, :], xb)
    kvs_t_ref[0] = ks_t
    kw_t = _dot_t(wkv_t_ref[2 * KV_W:3 * KV_W, :], xb)
    kvw_t_ref[0] = kw_t
    key = (pl.program_id(0) % nt) * tm + lax.broadcasted_iota(jnp.int32, (n_sel, tm), 1)
    blk = lax.broadcasted_iota(jnp.int32, (n_sel, tm), 0)
    onehot = jnp.where(key // SEL_BLOCK == blk, NEG_INF, 0.0).astype(BF16)
    ones_row = jnp.where(lax.broadcasted_iota(jnp.int32, (LANES - HEAD_DIM, tm), 0) == 0, 1.0, 0.0).astype(BF16)
    n_zero = kaug_ref.shape[2] - n_sel - HEAD_DIM
    for h in range(KV_HEADS):
        k_rows = slice(h * HEAD_DIM, (h + 1) * HEAD_DIM)
        v_rows = slice(V_OFF + h * HEAD_DIM, V_OFF + (h + 1) * HEAD_DIM)
        kaug_ref[0, h, 0:n_sel, :] = onehot
        kaug_ref[0, h, n_sel:n_sel + HEAD_DIM, :] = ks_t[k_rows].astype(BF16)
        if n_zero:
            kaug_ref[0, h, n_sel + HEAD_DIM:, :] = jnp.zeros((n_zero, tm), BF16)
        vaug_ref[0, h, 0:HEAD_DIM, :] = ks_t[v_rows].astype(BF16)
        vaug_ref[0, h, HEAD_DIM:, :] = ones_row
        kw_ref[0, h] = kw_t[k_rows].astype(BF16)
        vwa_ref[0, h, 0:HEAD_DIM, :] = kw_t[v_rows].astype(BF16)
        vwa_ref[0, h, HEAD_DIM:, :] = ones_row


def _proj_row_specs(n, tm):
    row = lambda w: pl.BlockSpec((tm, w), lambda i: (i, 0))
    widths = (W_CONV, W_CONV, W_ATTN, W_ATTN, KV_HEADS * LANES)
    dtypes = (F32, F32, BF16, F32, F32)
    return row, [row(w) for w in widths], [jax.ShapeDtypeStruct((n, w), d) for w, d in zip(widths, dtypes)]


def _proj_sample(x2d, w_pad):
    n = x2d.shape[0]
    row, specs, shapes = _proj_row_specs(n, n)
    return pl.pallas_call(
        _proj_sample_kernel,
        grid=(1,),
        in_specs=[row(D_MODEL), pl.BlockSpec((D_MODEL, E_PAD), lambda i: (0, 0))],
        out_specs=specs + [row(KV_W)] * 3,
        out_shape=shapes + [jax.ShapeDtypeStruct((n, KV_W), F32)] * 3,
        compiler_params=_cparams(("arbitrary",)),
        name="proj_sample",
    )(x2d, w_pad)


def _proj_prompt(x2d, w_pad, batch, seq, tm):
    n = x2d.shape[0]
    wkv_t = w_pad[:, KV_COL0:KV_COL0 + 3 * KV_W].T
    nt = seq // tm
    n_sel = -(-seq // SEL_BLOCK)
    k_rows = _round_up(n_sel + HEAD_DIM, LANES)
    row, specs, shapes = _proj_row_specs(n, tm)
    chan = pl.BlockSpec((1, KV_W, tm), lambda i: (i // nt, 0, i % nt))
    head = lambda r: pl.BlockSpec((1, KV_HEADS, r, tm), lambda i: (i // nt, 0, 0, i % nt))
    hshape = lambda r: jax.ShapeDtypeStruct((batch, KV_HEADS, r, seq), BF16)
    return pl.pallas_call(
        functools.partial(_proj_prompt_kernel, tm=tm, nt=nt, n_sel=n_sel),
        grid=(n // tm,),
        in_specs=[row(D_MODEL), pl.BlockSpec((D_MODEL, E_PAD), lambda i: (0, 0)),
                  pl.BlockSpec((3 * KV_W, D_MODEL), lambda i: (0, 0))],
        out_specs=specs + [chan] * 3 + [head(k_rows), head(LANES), head(HEAD_DIM), head(LANES)],
        out_shape=shapes + [jax.ShapeDtypeStruct((batch, KV_W, seq), F32)] * 3 + [
            hshape(k_rows), hshape(LANES), hshape(HEAD_DIM), hshape(LANES)],
        compiler_params=_cparams(("parallel",)),
        name="proj_prompt",
    )(x2d, w_pad, wkv_t)


def _pad_even_weights(w):
    offs = np.cumsum(EVEN_SPLITS)[:-1].tolist()
    a_val, a_glu, a_gate, wq, wkc, wks, wkw, wg, wbg = jnp.split(w, offs, axis=1)
    wg = wg.reshape(D_MODEL, KV_HEADS, GROUP, 3).transpose(0, 1, 3, 2).reshape(D_MODEL, KV_HEADS, 3 * GROUP)
    wg = jnp.pad(wg, ((0, 0), (0, 0), (0, LANES - 3 * GROUP))).reshape(D_MODEL, KV_HEADS * LANES)
    return jnp.concatenate([a_val, a_glu, a_gate, wq, wkc, wks, wkw, wbg, wg], axis=1).astype(BF16)


CONV_HALO = 32
CONV_CHUNK = 32


def _conv_prompt_kernel(u_ref, sg_ref, w_ref, cb_ref, g_ref, b_ref, o_ref, ext_ref, sh_ref, *, tt):
    @pl.when(pl.program_id(1) == 0)
    def _():
        ext_ref[0:CONV_HALO, :] = jnp.zeros((CONV_HALO, W_CONV), F32)

    ext_ref[CONV_HALO:CONV_HALO + tt, :] = u_ref[...]
    base = CONV_HALO - (CONV_WIDTH - 1)
    n_sh = sh_ref.shape[1]
    for s in range(1, SUBLANES):
        sh_ref[s - 1] = ext_ref[s:s + n_sh, :]

    def window(start):
        s, off = start % SUBLANES, start - start % SUBLANES
        return ext_ref[off:off + CONV_CHUNK, :] if s == 0 else sh_ref[s - 1, off:off + CONV_CHUNK, :]

    for c in range(tt // CONV_CHUNK):
        r0 = c * CONV_CHUNK
        acc = jnp.zeros((CONV_CHUNK, W_CONV), F32) + cb_ref[...]
        for k in range(CONV_WIDTH):
            acc = acc + w_ref[k:k + 1, :] * window(base + r0 + k)
        y = _layer_norm(acc, g_ref[...], b_ref[...])
        o_ref[r0:r0 + CONV_CHUNK, :] = _silu(y) * sg_ref[r0:r0 + CONV_CHUNK, :]
    ext_ref[0:CONV_HALO, :] = ext_ref[tt:tt + CONV_HALO, :]


def _conv_prompt(u2d, sg2d, conv_w, conv_b, ln_g, ln_b, batch, seq, tt):
    nt = seq // tt
    row = pl.BlockSpec((tt, W_CONV), lambda b, i: (b * nt + i, 0))
    vec = pl.BlockSpec((1, W_CONV), lambda b, i: (0, 0))
    return pl.pallas_call(
        functools.partial(_conv_prompt_kernel, tt=tt),
        grid=(batch, nt),
        in_specs=[row, row, pl.BlockSpec((CONV_WIDTH, W_CONV), lambda b, i: (0, 0)), vec, vec, vec],
        out_specs=row,
        out_shape=jax.ShapeDtypeStruct((batch * seq, W_CONV), F32),
        scratch_shapes=[pltpu.VMEM((tt + CONV_HALO, W_CONV), F32),
                        pltpu.VMEM((SUBLANES - 1, tt + CONV_HALO - SUBLANES, W_CONV), F32)],
        compiler_params=_cparams(("arbitrary", "arbitrary")),
        name="conv_prompt",
    )(u2d, sg2d, conv_w, conv_b, ln_g, ln_b)


def _conv_sample_kernel(ext_ref, sg_ref, w_ref, cb_ref, g_ref, b_ref, o_ref, *, ts):
    for t in range(ts):
        acc = jnp.zeros(ext_ref.shape[1:], F32) + cb_ref[...]
        for k in range(CONV_WIDTH):
            acc = acc + w_ref[k:k + 1, :] * ext_ref[t + k]
        y = _layer_norm(acc, g_ref[...], b_ref[...])
        o_ref[t] = _silu(y) * sg_ref[t]


def _conv_sample(ext, sg, conv_w, conv_b, ln_g, ln_b):
    ts, bsz, _ = sg.shape
    return pl.pallas_call(
        functools.partial(_conv_sample_kernel, ts=ts),
        out_shape=jax.ShapeDtypeStruct((ts, bsz, W_CONV), F32),
        name="conv_sample",
    )(ext, sg, conv_w, conv_b, ln_g, ln_b)


def _chunk_sums(x, w1_ref, w2_ref, s_ref):
    reps = x.shape[1] // LANES

    def seg(w_ref):
        y = x * jnp.tile(w_ref[...], (1, reps))
        hi = y.astype(BF16)
        lo = (y - hi.astype(F32)).astype(BF16)
        return _dot(hi, s_ref[...]) + _dot(lo, s_ref[...])

    return seg(w1_ref), seg(w2_ref)


def _compress_prompt_kernel(x_ref, w1_ref, w2_ref, s_ref, o_ref, *, lanes):
    firsts, seconds = [], []
    for c in range(x_ref.shape[2] // lanes):
        a, b = _chunk_sums(x_ref[0, :, c * lanes:(c + 1) * lanes], w1_ref, w2_ref, s_ref)
        firsts.append(a)
        seconds.append(b)
    o_ref[0] = jnp.concatenate(firsts, axis=1) + _shift_left_lanes(jnp.concatenate(seconds, axis=1))


def _compress_prompt(kvc_t, w1t, w2t, seg):
    batch, _, seq = kvc_t.shape
    lanes = seg.shape[0]
    const = lambda a: pl.BlockSpec(a.shape, lambda b: (0, 0))
    return pl.pallas_call(
        functools.partial(_compress_prompt_kernel, lanes=lanes),
        grid=(batch,),
        in_specs=[pl.BlockSpec((1, KV_W, seq), lambda b: (b, 0, 0)), const(w1t), const(w2t), const(seg)],
        out_specs=pl.BlockSpec((1, KV_W, seq // CMP_STRIDE), lambda b: (b, 0, 0)),
        out_shape=jax.ShapeDtypeStruct((batch, KV_W, seq // CMP_STRIDE), F32),
        compiler_params=_cparams(("parallel",)),
        name="compress_prompt",
    )(kvc_t, w1t, w2t, seg)


def _compress_paged_kernel(pt_ref, *refs, n_pages_step):
    del pt_ref
    pages = refs[:n_pages_step]
    w1_ref, w2_ref, s_ref, a_ref, b_ref = refs[n_pages_step:]
    x = jnp.concatenate([p[0] for p in pages], axis=1)
    a_ref[0], b_ref[0] = _chunk_sums(x, w1_ref, w2_ref, s_ref)


def _compress_paged(cache_t, page_table, w1t, w2t, seg):
    _, _, page = cache_t.shape
    bsz, n_pages = page_table.shape
    lanes = seg.shape[0]
    pps = lanes // page
    n_steps = n_pages // pps
    n_out = lanes // CMP_STRIDE

    def page_spec(k):
        return pl.BlockSpec((1, KV_W, page), lambda b, s, pt: (pt[b, s * pps + k], 0, 0))

    const = lambda a: pl.BlockSpec(a.shape, lambda b, s, pt: (0, 0))
    out_spec = pl.BlockSpec((1, KV_W, n_out), lambda b, s, pt: (b, 0, s))
    out_shape = jax.ShapeDtypeStruct((bsz, KV_W, n_steps * n_out), F32)
    grid_spec = pltpu.PrefetchScalarGridSpec(
        num_scalar_prefetch=1,
        grid=(bsz, n_steps),
        in_specs=[page_spec(k) for k in range(pps)] + [const(w1t), const(w2t), const(seg)],
        out_specs=[out_spec, out_spec],
    )
    return pl.pallas_call(
        functools.partial(_compress_paged_kernel, n_pages_step=pps),
        grid_spec=grid_spec,
        out_shape=[out_shape, out_shape],
        compiler_params=_cparams(("parallel", "arbitrary")),
        name="compress_paged",
    )(page_table, *([cache_t] * pps), w1t, w2t, seg)


def _compress_consts(w_cmp2d, lanes):
    w1t = jnp.tile(w_cmp2d[:CMP_STRIDE].T, (1, LANES // CMP_STRIDE))
    w2t = jnp.tile(w_cmp2d[CMP_STRIDE:].T, (1, LANES // CMP_STRIDE))
    seg = jnp.asarray(np.arange(lanes)[:, None] // CMP_STRIDE == np.arange(lanes // CMP_STRIDE)[None, :], BF16)
    return w1t, w2t, seg


def _pick_top_block(sc, sel, blk):
    mx = jnp.max(sc, axis=0, keepdims=True)
    jm = jnp.min(jnp.where(sc == mx, blk, sc.shape[0]), axis=0, keepdims=True)
    pick = blk == jm
    return jnp.where(pick, LOWEST, sc), jnp.where(pick, 1.0, sel)


def _force_scores(score, blk, cur):
    forced = (blk == 0) | ((blk >= cur - 1) & (blk <= cur))
    return jnp.where(forced, SEL_FORCE, jnp.where(blk <= cur, score, -SEL_FORCE))


def _sel_matrix(n_sel_blocks, n_cols):
    n = np.arange(n_cols)
    m = (n[None, :] // PER_SEL == np.arange(n_sel_blocks)[:, None]) & (n[None, :] % PER_SEL < PER_SEL - 1)
    return jnp.asarray(m, BF16)


def _cmp_valid(n, qpos, n_cmp):
    return n * CMP_STRIDE + (CMP_BLOCK - 1) <= jnp.minimum(qpos, (n_cmp - 1) * CMP_STRIDE + CMP_BLOCK - 1)


def _nsa_prompt_kernel(q_ref, kaug_ref, vaug_ref, kw_ref, vwa_ref, kc_ref, vc_ref, gt_ref, m01_ref, o_ref,
                       qa_ref, acc_ref, mx_ref, s_ref, *, n_cmp, n_sel):
    i = pl.program_id(2)
    rows = GROUP * Q_TILE
    q = q_ref[...]
    qs = jnp.concatenate([q[:, g * HEAD_DIM:(g + 1) * HEAD_DIM] for g in range(GROUP)], axis=0)
    n_cols = kc_ref.shape[3]

    def row_pos(width):
        return i * Q_TILE + (lax.broadcasted_iota(jnp.int32, (rows, width), 0) & (Q_TILE - 1))

    s = _dot(qs, kc_ref[0, 0].astype(BF16))
    valid = _cmp_valid(lax.broadcasted_iota(jnp.int32, (rows, n_cols), 1), row_pos(n_cols), n_cmp)
    s = jnp.where(valid, s, NEG_INF)
    e = jnp.where(valid, jnp.exp(s - jnp.max(s, axis=1, keepdims=True)), 0.0)
    l = jnp.sum(e, axis=1, keepdims=True)
    p = e / jnp.where(l > 0.0, l, 1.0)
    o_c = _dot_t(p.astype(BF16), vc_ref[0, 0].astype(BF16))

    p_grp = p[0:Q_TILE]
    for g in range(1, GROUP):
        p_grp = p_grp + p[g * Q_TILE:(g + 1) * Q_TILE]
    score_t = sum(_dot_t(m01_ref[...], part) for part in _split3(p_grp))
    blk = lax.broadcasted_iota(jnp.int32, (n_sel, Q_TILE), 0)
    tok = i * Q_TILE + lax.broadcasted_iota(jnp.int32, (n_sel, Q_TILE), 1)
    sc = _force_scores(score_t, blk, tok // SEL_BLOCK)
    sel_t = jnp.zeros(sc.shape, F32)
    n_pick = min(N_SEL, n_sel)

    w_keys = WINDOW + Q_TILE
    w0 = pl.multiple_of(jnp.maximum(i * Q_TILE - WINDOW, 0), Q_TILE)
    rel = (i * Q_TILE + lax.broadcasted_iota(jnp.int32, (Q_TILE, w_keys), 0)
           - (w0 + lax.broadcasted_iota(jnp.int32, (Q_TILE, w_keys), 1)))
    w_bias = jnp.where((rel >= 0) & (rel < WINDOW), 0.0, NEG_INF)
    s_w_all = _dot(qs, kw_ref[0, 0, :, pl.ds(w0, w_keys)])
    e_w_parts = []
    done = 0
    for g in range(GROUP):
        s_w = s_w_all[g * Q_TILE:(g + 1) * Q_TILE] + w_bias
        e_w_parts.append(jnp.exp(s_w - jnp.max(s_w, axis=1, keepdims=True)).astype(BF16))
        upto = n_pick * (g + 1) // GROUP
        for _ in range(done, upto):
            sc, sel_t = _pick_top_block(sc, sel_t, blk)
        done = upto
    acc_w = _dot_t(jnp.concatenate(e_w_parts, axis=0), vwa_ref[0, 0, :, pl.ds(w0, w_keys)])
    o_w = acc_w[:, 0:HEAD_DIM] / acc_w[:, HEAD_DIM:HEAD_DIM + 1]
    not_sel = (1.0 - sel_t).T.astype(BF16)
    for g in range(GROUP):
        qa_ref[g * Q_TILE:(g + 1) * Q_TILE, 0:n_sel] = not_sel
    qa_ref[:, n_sel:n_sel + HEAD_DIM] = qs
    if qa_ref.shape[1] > n_sel + HEAD_DIM:
        qa_ref[:, n_sel + HEAD_DIM:] = jnp.zeros((rows, qa_ref.shape[1] - n_sel - HEAD_DIM), BF16)

    mx_ref[...] = jnp.full(mx_ref.shape, NEG_INF, F32)

    def score_tiles(j, n_tiles, causal_last):
        for t in range(n_tiles):
            k0 = pl.multiple_of((j + t) * K_TILE, K_TILE)
            s = _dot(qa_ref[...], kaug_ref[0, 0, :, pl.ds(k0, K_TILE)])
            if causal_last and t == n_tiles - 1:
                kpos = k0 + lax.broadcasted_iota(jnp.int32, (rows, K_TILE), 1)
                s = jnp.where(kpos <= row_pos(K_TILE), s, NEG_INF)
            s_ref[j + t] = s
            part = s[:, 0:LANES]
            for c in range(1, K_TILE // LANES):
                part = jnp.maximum(part, s[:, c * LANES:(c + 1) * LANES])
            mx_ref[...] = jnp.maximum(mx_ref[...], part)

    def acc_tiles(j, n_tiles):
        m_b = jnp.tile(mx_ref[...], (1, K_TILE // LANES))
        for t in range(n_tiles):
            k0 = pl.multiple_of((j + t) * K_TILE, K_TILE)
            pe = jnp.exp(s_ref[j + t] - m_b).astype(BF16)
            acc_ref[...] += _dot_t(pe, vaug_ref[0, 0, :, pl.ds(k0, K_TILE)])

    def grouped(n, fn, tail_fn, tail_min):
        def body(jj, carry):
            fn(TILES_PER_TRIP * jj, TILES_PER_TRIP)
            return carry
        lax.fori_loop(0, n // TILES_PER_TRIP, body, 0)
        for r in range(tail_min, TILES_PER_TRIP):
            @pl.when(n % TILES_PER_TRIP == r)
            def _(r=r):
                tail_fn(n - r, r)

    n_full = (i * Q_TILE) // K_TILE
    grouped(n_full, lambda j, n: score_tiles(j, n, False), lambda j, r: score_tiles(j, r + 1, True), 0)

    mx_ref[...] = jnp.broadcast_to(jnp.max(mx_ref[...], axis=1, keepdims=True), mx_ref.shape)
    acc_ref[...] = jnp.zeros(acc_ref.shape, F32)
    grouped(n_full + 1, acc_tiles, acc_tiles, 1)

    acc = acc_ref[...]
    o_s = acc[:, 0:HEAD_DIM] / acc[:, HEAD_DIM:HEAD_DIM + 1]

    gt = gt_ref[...]
    outs = []
    for g in range(GROUP):
        r = slice(g * Q_TILE, (g + 1) * Q_TILE)
        outs.append(gt[:, g:g + 1] * o_c[r] + gt[:, GROUP + g:GROUP + g + 1] * o_s[r]
                    + gt[:, 2 * GROUP + g:2 * GROUP + g + 1] * o_w[r])
    o_ref[...] = jnp.concatenate(outs, axis=1)


def _nsa_prompt(q2d, kaug, vaug, kw, vwa, kcv4, gt2d, m01, batch, seq, n_cmp, n_sel):
    nq = seq // Q_TILE
    rows = GROUP * Q_TILE
    n_cols = kcv4.shape[-1]
    per_head = lambda a: pl.BlockSpec((1, 1) + a.shape[2:], lambda b, h, i: (b, h, 0, 0))
    return pl.pallas_call(
        functools.partial(_nsa_prompt_kernel, n_cmp=n_cmp, n_sel=n_sel),
        grid=(batch, KV_HEADS, nq),
        in_specs=[pl.BlockSpec((Q_TILE, GROUP * HEAD_DIM), lambda b, h, i: (b * nq + i, h)),
                  per_head(kaug), per_head(vaug), per_head(kw), per_head(vwa),
                  pl.BlockSpec((1, 1, HEAD_DIM, n_cols), lambda b, h, i: (b, h, 0, 0)),
                  pl.BlockSpec((1, 1, HEAD_DIM, n_cols), lambda b, h, i: (b, KV_HEADS + h, 0, 0)),
                  pl.BlockSpec((Q_TILE, LANES), lambda b, h, i: (b * nq + i, h)),
                  pl.BlockSpec(m01.shape, lambda b, h, i: (0, 0))],
        out_specs=pl.BlockSpec((Q_TILE, GROUP * HEAD_DIM), lambda b, h, i: (b * nq + i, h)),
        out_shape=jax.ShapeDtypeStruct((batch * seq, W_ATTN), F32),
        scratch_shapes=[pltpu.VMEM((rows, kaug.shape[2]), BF16), pltpu.VMEM((rows, LANES), F32),
                        pltpu.VMEM((rows, LANES), F32), pltpu.VMEM((seq // K_TILE, rows, K_TILE), F32)],
        compiler_params=_cparams(("parallel", "parallel", "arbitrary")),
        name="nsa_prompt",
    )(q2d, kaug, vaug, kw, vwa, kcv4, kcv4, gt2d, m01)


def _new_token_scores(qf, nk, tq, ts):
    cols = []
    for t in range(ts):
        s = jnp.sum(qf * nk[t:t + 1, :], axis=1, keepdims=True)
        cols.append(jnp.where(tq >= t, s, NEG_INF))
    return cols


def _bf16_round(x):
    return x.astype(BF16).astype(F32)


def _nsa_sample_a_kernel(q_ref, a_ref, b_ref, win_ref, new_ref, gt_ref, m01_ref, part_ref, idx_ref, *,
                         past, ts, n_cmp):
    for bi in range(q_ref.shape[0]):
        _nsa_sample_a_one(bi, q_ref, a_ref, b_ref, win_ref, new_ref, gt_ref, m01_ref, part_ref, idx_ref,
                          past=past, ts=ts, n_cmp=n_cmp)


def _nsa_sample_a_one(bi, q_ref, a_ref, b_ref, win_ref, new_ref, gt_ref, m01_ref, part_ref, idx_ref, *,
                      past, ts, n_cmp):
    rows = GROUP * ts
    kcv = a_ref[bi] + _shift_left_lanes(b_ref[bi])
    win = win_ref[bi]
    new = new_ref[bi]
    n_cols = kcv.shape[1]
    w_hist = win.shape[1]
    n_blk_pad = m01_ref.shape[0]
    tq = lax.broadcasted_iota(jnp.int32, (rows, 1), 0) % ts
    qpos = past + tq
    scores = []
    for h in range(KV_HEADS):
        qh = q_ref[bi, h]
        ksl = slice(h * HEAD_DIM, (h + 1) * HEAD_DIM)
        vsl = slice(V_OFF + h * HEAD_DIM, V_OFF + (h + 1) * HEAD_DIM)
        s = _dot(qh, kcv[ksl].astype(BF16))
        valid = _cmp_valid(lax.broadcasted_iota(jnp.int32, (rows, n_cols), 1), qpos, n_cmp)
        s = jnp.where(valid, s, NEG_INF)
        e = jnp.where(valid, jnp.exp(s - jnp.max(s, axis=1, keepdims=True)), 0.0)
        l = jnp.sum(e, axis=1, keepdims=True)
        p = e / jnp.where(l > 0.0, l, 1.0)
        o_c = _dot_t(p.astype(BF16), kcv[vsl].astype(BF16))
        p_grp = p[0:ts]
        for g in range(1, GROUP):
            p_grp = p_grp + p[g * ts:(g + 1) * ts]
        scores.append(sum(_dot_t(part, m01_ref[...]) for part in _split3(p_grp)))

        s1 = _dot(qh, win[ksl].astype(BF16))
        rel = qpos - (past - w_hist + lax.broadcasted_iota(jnp.int32, (rows, w_hist), 1))
        valid = (rel >= 0) & (rel <= jnp.minimum(qpos, WINDOW - 1))
        s1 = jnp.where(valid, s1, NEG_INF)
        s2 = _new_token_scores(qh.astype(F32), _bf16_round(new[:, ksl]), tq, ts)
        m = jnp.max(s1, axis=1, keepdims=True)
        for c in s2:
            m = jnp.maximum(m, c)
        e1 = jnp.where(valid, jnp.exp(s1 - m), 0.0)
        l = jnp.sum(e1, axis=1, keepdims=True)
        o_w = _dot_t(e1.astype(BF16), win[vsl].astype(BF16))
        nv = _bf16_round(new[:, vsl])
        for t in range(ts):
            e2 = jnp.exp(s2[t] - m)
            l = l + e2
            o_w = o_w + _bf16_round(e2) * nv[t:t + 1, :]
        gt = gt_ref[bi, h]
        part_ref[bi, h] = gt[:, 0:1] * o_c + gt[:, 2:3] * (o_w / l)

    sc = jnp.concatenate(scores, axis=0)
    nr = KV_HEADS * ts
    blk = lax.broadcasted_iota(jnp.int32, (nr, n_blk_pad), 1)
    cur = (past + (lax.broadcasted_iota(jnp.int32, (nr, 1), 0) % ts)) // SEL_BLOCK
    sc = _force_scores(sc, blk, cur)
    lane = lax.broadcasted_iota(jnp.int32, (nr, LANES), 1)
    idx = jnp.zeros((nr, LANES), jnp.int32)
    for k in range(N_SEL):
        mx = jnp.max(sc, axis=1, keepdims=True)
        jm = jnp.min(jnp.where(sc == mx, blk, n_blk_pad), axis=1, keepdims=True)
        idx = jnp.where(lane == k, jm, idx)
        sc = jnp.where(blk == jm, LOWEST, sc)
    idx_ref[bi] = idx


def _nsa_sample_a(q4, a_t, b_t, win_t, new_w, gt4, m01, past, ts, n_cmp):
    bsz = q4.shape[0]
    rows = GROUP * ts
    nb = max(d for d in (4, 2, 1) if bsz % d == 0)
    per_b = lambda a: pl.BlockSpec((nb,) + a.shape[1:], lambda b: (b,) + (0,) * (a.ndim - 1))
    return pl.pallas_call(
        functools.partial(_nsa_sample_a_kernel, past=past, ts=ts, n_cmp=n_cmp),
        grid=(bsz // nb,),
        in_specs=[per_b(q4), per_b(a_t), per_b(b_t), per_b(win_t), per_b(new_w), per_b(gt4),
                  pl.BlockSpec(m01.shape, lambda b: (0, 0))],
        out_specs=[pl.BlockSpec((nb, KV_HEADS, rows, HEAD_DIM), lambda b: (b, 0, 0, 0)),
                   pl.BlockSpec((nb, KV_HEADS * ts, LANES), lambda b: (b, 0, 0))],
        out_shape=[jax.ShapeDtypeStruct((bsz, KV_HEADS, rows, HEAD_DIM), F32),
                   jax.ShapeDtypeStruct((bsz, KV_HEADS * ts, LANES), jnp.int32)],
        compiler_params=_cparams(("parallel",)),
        name="nsa_sample_scores",
    )(q4, a_t, b_t, win_t, new_w, gt4, m01)


def _nsa_sample_b_kernel(idx_ref, pt_ref, *refs, past, ts, n_past_blocks, page):
    del pt_ref
    kv_blocks = refs[:N_SEL]
    q_ref, new_ref, gt_ref, part_ref, o_ref = refs[N_SEL:]
    r = pl.program_id(0)
    t = r % ts
    head0 = ((r // ts) % KV_HEADS) == 0
    bpp = page // SEL_BLOCK
    q = q_ref[0]
    lane = lax.broadcasted_iota(jnp.int32, (GROUP, page), 1)

    def pick(x, off):
        return jnp.where(head0, x[:, off:off + HEAD_DIM], x[:, off + HEAD_DIM:off + 2 * HEAD_DIM])

    s_all = _dot(q, jnp.concatenate([kv_blocks[k][0, 0, 0].astype(BF16) for k in range(N_SEL)], axis=1))
    s_list = []
    has_new = False
    for k in range(N_SEL):
        j = idx_ref[r * N_SEL + k]
        is_past = j < n_past_blocks
        has_new = jnp.logical_or(has_new, jnp.logical_not(is_past))
        off = (j % bpp) * SEL_BLOCK
        page_start = (j // bpp) * page
        lo = jnp.where(is_past, off, page)
        hi = jnp.minimum(off + SEL_BLOCK, past + t + 1 - page_start)
        s_list.append(jnp.where((lane >= lo) & (lane < hi), s_all[:, k * page:(k + 1) * page], NEG_INF))
    new = new_ref[0]
    tq = jnp.where(has_new, t, -1) + jnp.zeros((GROUP, 1), jnp.int32)
    s_new = _new_token_scores(q.astype(F32), _bf16_round(pick(new, 0)), tq, ts)
    m = s_new[0]
    for c in s_new[1:]:
        m = jnp.maximum(m, c)
    for s in s_list:
        m = jnp.maximum(m, jnp.max(s, axis=1, keepdims=True))
    nv = _bf16_round(pick(new, V_OFF))
    l = jnp.zeros((GROUP, 1), F32)
    o = jnp.zeros((GROUP, HEAD_DIM), F32)
    for tt in range(ts):
        e2 = jnp.exp(s_new[tt] - m)
        l = l + e2
        o = o + _bf16_round(e2) * nv[tt:tt + 1, :]
    e_all = jnp.exp(jnp.concatenate(s_list, axis=1) - m)
    l = l + jnp.sum(e_all, axis=1, keepdims=True)
    o = o + _dot_t(e_all.astype(BF16),
                   jnp.concatenate([kv_blocks[k][0, 1, 0].astype(BF16) for k in range(N_SEL)], axis=1))
    gt = gt_ref[0]
    o_ref[0] = part_ref[0] + gt[:, 1:2] * (o / l)


def _nsa_sample_b(idx_flat, page_flat, cache_slabs, q_rows, new_rows, gt_rows, part_rows, past, ts, page):
    n_rows = q_rows.shape[0]
    n_past_blocks = past // SEL_BLOCK

    def slab_spec(k):
        return pl.BlockSpec((1, 2, 1, HEAD_DIM, page),
                            lambda r, idx, pg: (pg[r * N_SEL + k], 0, (r // ts) % KV_HEADS, 0, 0))

    row3 = lambda a: pl.BlockSpec((1,) + a.shape[1:], lambda r, idx, pt: (r, 0, 0))
    grid_spec = pltpu.PrefetchScalarGridSpec(
        num_scalar_prefetch=2,
        grid=(n_rows,),
        in_specs=[slab_spec(k) for k in range(N_SEL)] + [
            row3(q_rows),
            pl.BlockSpec((1,) + new_rows.shape[1:], lambda r, idx, pt: (r // (KV_HEADS * ts), 0, 0)),
            row3(gt_rows), row3(part_rows)],
        out_specs=row3(part_rows),
    )
    return pl.pallas_call(
        functools.partial(_nsa_sample_b_kernel, past=past, ts=ts, n_past_blocks=n_past_blocks, page=page),
        grid_spec=grid_spec,
        out_shape=jax.ShapeDtypeStruct(part_rows.shape, F32),
        compiler_params=_cparams(("arbitrary",)),
        name="nsa_sample_select",
    )(idx_flat, page_flat, *([cache_slabs] * N_SEL), q_rows, new_rows, gt_rows, part_rows)


def _channel_major(x):
    n, rows = x.shape[:2]
    return x.transpose(0, 2, 3, 4, 1).reshape(n, KV_W, rows)


def _sample_attention(q_tm, kvs_tm, kvw_tm, gt_tm, cache_c, cache_s, state_win, page_table, cmp_consts, bsz, ts):
    n_phys, page = cache_c.shape[0], cache_c.shape[1]
    n_pages = page_table.shape[1]
    past = n_pages * page
    w_hist = state_win.shape[1]
    assert ts < CMP_STRIDE and page % SEL_BLOCK == 0 and ts <= WINDOW and w_hist <= past
    total = past + ts
    n_cmp = total // CMP_STRIDE - 1
    n_sel_blocks = -(-total // SEL_BLOCK)
    a_t, b_t = _compress_paged(_channel_major(cache_c), page_table, *cmp_consts)
    m01 = _sel_matrix(_round_up(n_sel_blocks, LANES), a_t.shape[2])
    q4 = q_tm.reshape(ts, bsz, KV_HEADS, GROUP, HEAD_DIM).transpose(1, 2, 3, 0, 4).reshape(
        bsz, KV_HEADS, GROUP * ts, HEAD_DIM)
    g5 = gt_tm.reshape(ts, bsz, KV_HEADS, LANES)[..., :3 * GROUP].reshape(ts, bsz, KV_HEADS, 3, GROUP)
    g5 = g5.transpose(1, 2, 4, 0, 3).reshape(bsz, KV_HEADS, GROUP * ts, 3)
    gt4 = jnp.pad(g5, ((0, 0), (0, 0), (0, 0), (0, LANES - 3)))
    to_bt = lambda a: a.reshape(ts, bsz, KV_W).transpose(1, 0, 2)
    part, idx = _nsa_sample_a(q4, a_t, b_t, _channel_major(state_win), to_bt(kvw_tm), gt4, m01, past, ts, n_cmp)
    to_rows = lambda a: a.reshape(bsz, KV_HEADS, GROUP, ts, a.shape[-1]).transpose(0, 1, 3, 2, 4).reshape(
        bsz * KV_HEADS * ts, GROUP, a.shape[-1])
    slabs = _channel_major(cache_s).reshape(n_phys, 2, KV_HEADS, HEAD_DIM, page)
    idx16 = idx[:, :, :N_SEL]
    pages = jnp.take_along_axis(page_table[:, None, :],
                                jnp.minimum(idx16, past // SEL_BLOCK - 1) // (page // SEL_BLOCK), axis=2)
    o_rows = _nsa_sample_b(idx16.reshape(-1), pages.reshape(-1), slabs,
                           to_rows(q4), to_bt(kvs_tm), to_rows(gt4), to_rows(part), past, ts, page)
    return o_rows.reshape(bsz, KV_HEADS, ts, GROUP, HEAD_DIM).transpose(2, 0, 1, 3, 4).reshape(ts * bsz, W_ATTN)


def _mix_out_kernel(x_ref, a_ref, b_ref, bg_ref, w_ref, g_ref, beta_ref, o_ref):
    a = a_ref[...].astype(BF16)
    b = (b_ref[...] * bg_ref[...]).astype(BF16)
    d = _dot(a, w_ref[0:W_CONV, :]) + _dot(b, w_ref[W_CONV:W_CONV + W_ATTN, :])
    o_ref[...] = _layer_norm(DEEPNORM_ALPHA * x_ref[...] + d, g_ref[...], beta_ref[...])


def _mix_out(x2d, a2d, b2d, bg2d, w_out, ln_g, ln_b, tm):
    n = x2d.shape[0]
    row = lambda w: pl.BlockSpec((tm, w), lambda i: (i, 0))
    vec = pl.BlockSpec((1, D_MODEL), lambda i: (0, 0))
    return pl.pallas_call(
        _mix_out_kernel,
        grid=(n // tm,),
        in_specs=[row(D_MODEL), row(W_CONV), row(W_ATTN), row(W_ATTN),
                  pl.BlockSpec((W_CONV + W_ATTN, D_MODEL), lambda i: (0, 0)), vec, vec],
        out_specs=row(D_MODEL),
        out_shape=jax.ShapeDtypeStruct((n, D_MODEL), F32),
        compiler_params=_cparams(("parallel",)),
        name="mix_out",
    )(x2d, a2d, b2d, bg2d, w_out, ln_g, ln_b)


POOL_HALO = 16


def _pool_tail(x, d_groups, gate, wg_ref, sc_ref, wo_ref, g_ref, beta_ref):
    mixed = jnp.concatenate([_dot(d_groups[gi].astype(BF16), wg_ref[gi]) for gi in range(len(POOL_WINDOWS))], axis=1)
    h = (mixed * sc_ref[...] * _silu(gate)).astype(BF16)
    return _layer_norm(DEEPNORM_ALPHA * x + _dot(h, wo_ref[...]), g_ref[...], beta_ref[...])


def _pool_prompt_kernel(x_ref, wi_ref, wg_ref, sc_ref, wo_ref, g_ref, beta_ref, o_ref, tail_ref, ext_ref, *, tm):
    i = pl.program_id(1)

    @pl.when(i == 0)
    def _():
        ext_ref[0:POOL_HALO, :] = jnp.zeros((POOL_HALO, W_POOL), F32)

    x = x_ref[...]
    xb = x.astype(BF16)
    v = _dot(xb, wi_ref[:, 0:W_POOL])
    gate = _dot(xb, wi_ref[:, W_POOL:2 * W_POOL])
    ext_ref[POOL_HALO:POOL_HALO + tm, :] = v
    pos = i * tm + lax.broadcasted_iota(jnp.int32, (tm, 1), 0)
    d_groups = []
    for gi, w in enumerate(POOL_WINDOWS):
        c = slice(gi * POOL_GROUP_W, (gi + 1) * POOL_GROUP_W)
        win_sum = v[:, c]
        for k in range(1, w):
            win_sum = win_sum + ext_ref[POOL_HALO - k:POOL_HALO - k + tm, c]
        cnt = jnp.minimum(pos + 1, w).astype(F32)
        d_groups.append(win_sum / cnt - v[:, c])
    o_ref[...] = _pool_tail(x, d_groups, gate, wg_ref, sc_ref, wo_ref, g_ref, beta_ref)
    ext_ref[0:POOL_HALO, :] = ext_ref[tm:tm + POOL_HALO, :]
    tail_ref[0] = ext_ref[0:POOL_HALO, :]


def _pool_prompt(x2d, w_in, w_grp, scale, w_out, ln_g, ln_b, batch, seq, tm):
    nt = seq // tm
    const = lambda a: pl.BlockSpec(a.shape, lambda b, i: (0,) * a.ndim)
    row = pl.BlockSpec((tm, D_MODEL), lambda b, i: (b * nt + i, 0))
    return pl.pallas_call(
        functools.partial(_pool_prompt_kernel, tm=tm),
        grid=(batch, nt),
        in_specs=[row, const(w_in), const(w_grp), const(scale), const(w_out), const(ln_g), const(ln_b)],
        out_specs=[row, pl.BlockSpec((1, POOL_HALO, W_POOL), lambda b, i: (b, 0, 0))],
        out_shape=[jax.ShapeDtypeStruct((batch * seq, D_MODEL), F32),
                   jax.ShapeDtypeStruct((batch, POOL_HALO, W_POOL), F32)],
        scratch_shapes=[pltpu.VMEM((tm + POOL_HALO, W_POOL), F32)],
        compiler_params=_cparams(("arbitrary", "arbitrary")),
        name="pool_prompt",
    )(x2d, w_in, w_grp, scale, w_out, ln_g, ln_b)


def _pool_sample_kernel(x_ref, st_ref, wi_ref, wg_ref, sc_ref, wo_ref, g_ref, beta_ref, o_ref, v_ref, *,
                        ts, bsz, pos0):
    x = x_ref[...]
    xb = x.astype(BF16)
    v = _dot(xb, wi_ref[:, 0:W_POOL])
    gate = _dot(xb, wi_ref[:, W_POOL:2 * W_POOL])
    v_ref[...] = v
    n_hist = st_ref.shape[0]

    def ext(r):
        return st_ref[r] if r < n_hist else v[(r - n_hist) * bsz:(r - n_hist + 1) * bsz, :]

    d_groups = []
    for gi, w in enumerate(POOL_WINDOWS):
        c = slice(gi * POOL_GROUP_W, (gi + 1) * POOL_GROUP_W)
        per_t = []
        for t in range(ts):
            win_sum = ext(n_hist + t)[:, c]
            for k in range(1, w):
                win_sum = win_sum + ext(n_hist + t - k)[:, c]
            per_t.append(win_sum / float(min(pos0 + t + 1, w)) - ext(n_hist + t)[:, c])
        d_groups.append(jnp.concatenate(per_t, axis=0))
    o_ref[...] = _pool_tail(x, d_groups, gate, wg_ref, sc_ref, wo_ref, g_ref, beta_ref)


def _pool_sample(x_tm, state_tm, w_in, w_grp, scale, w_out, ln_g, ln_b, ts, bsz, pos0):
    return pl.pallas_call(
        functools.partial(_pool_sample_kernel, ts=ts, bsz=bsz, pos0=pos0),
        out_shape=[jax.ShapeDtypeStruct((ts * bsz, D_MODEL), F32), jax.ShapeDtypeStruct((ts * bsz, W_POOL), F32)],
        compiler_params=pltpu.CompilerParams(vmem_limit_bytes=VMEM_LIMIT),
        name="pool_sample",
    )(x_tm, state_tm, w_in, w_grp, scale, w_out, ln_g, ln_b)


def kernel(x_prompt, x_sample, cache_kv_cmp, cache_kv_sel, state_win_kv, state_conv, state_pool, page_table,
           w_in_even, w_cmp, conv_w, conv_b, conv_ln_g, conv_ln_b, w_out_even,
           w_in_odd, w_pool_grp, pool_scale, w_out_odd, ln_g, ln_b):
    batch, seq, _ = x_prompt.shape
    bsz, ts, _ = x_sample.shape
    past = page_table.shape[1] * cache_kv_cmp.shape[2]
    kv_shape = (2, KV_HEADS, HEAD_DIM)
    vec = lambda a: a.reshape(1, -1)
    assert seq >= CONV_WIDTH - 1 and seq >= POOL_MAX - 1 and seq >= WINDOW + Q_TILE and seq % K_TILE == 0

    def token_major(x_t):
        n, _, rows = x_t.shape
        return x_t.reshape((n,) + kv_shape + (rows,)).transpose(0, 4, 1, 2, 3)[None]

    w_pad = _pad_even_weights(w_in_even[0])
    cmp_lanes = min(CMP_LANES, seq, past)
    cmp_consts = _compress_consts(w_cmp[0].reshape(CMP_BLOCK, KV_W), cmp_lanes)
    w_oe = w_out_even[0].astype(BF16)
    cw, cb, cg, cbeta = conv_w[0], vec(conv_b[0]), vec(conv_ln_g[0]), vec(conv_ln_b[0])
    g0, b0 = vec(ln_g[0]), vec(ln_b[0])

    xp = x_prompt.reshape(batch * seq, D_MODEL)
    u, sg, q, bg, gt, kvc_t, kvs_t, kvw_t, kaug, vaug, kw, vwa = _proj_prompt(xp, w_pad, batch, seq, 256)
    a_out = _conv_prompt(u, sg, cw, cb, cg, cbeta, batch, seq, 256)
    n_chunks = seq // CMP_STRIDE
    n_sel = -(-seq // SEL_BLOCK)
    kcv = _compress_prompt(kvc_t, *cmp_consts)
    b_raw = _nsa_prompt(q, kaug, vaug, kw, vwa, kcv.reshape(batch, 2 * KV_HEADS, HEAD_DIM, n_chunks), gt,
                        _sel_matrix(n_sel, n_chunks), batch, seq, n_chunks - 1, n_sel)
    xp1 = _mix_out(xp, a_out, b_raw, bg, w_oe, g0, b0, 256)

    n_win_p = min(WINDOW, seq)
    kvc_p = token_major(kvc_t)
    kvs_p = token_major(kvs_t)
    win_p = token_major(kvw_t[:, :, seq - n_win_p:])
    conv_p = u.reshape(batch, seq, W_CONV)[:, seq - (CONV_WIDTH - 1):][None]

    xs = x_sample.transpose(1, 0, 2).reshape(ts * bsz, D_MODEL)
    us, sgs, qs, bgs, gts, kvcs, kvss, kvws = _proj_sample(xs, w_pad)
    conv_ext = jnp.concatenate([state_conv[0].transpose(1, 0, 2), us.reshape(ts, bsz, W_CONV)], axis=0)
    a_out_s = _conv_sample(conv_ext, sgs.reshape(ts, bsz, W_CONV), cw, cb, cg, cbeta).reshape(ts * bsz, W_CONV)
    b_raw_s = _sample_attention(qs, kvss, kvws, gts, cache_kv_cmp[0], cache_kv_sel[0], state_win_kv[0], page_table,
                                cmp_consts, bsz, ts)
    xs1 = _mix_out(xs, a_out_s, b_raw_s, bgs, w_oe, g0, b0, ts * bsz)

    to_bt = lambda a: a.reshape(ts, bsz, -1).transpose(1, 0, 2)
    kvc_s = to_bt(kvcs).reshape((1, bsz, ts) + kv_shape)
    kvs_s = to_bt(kvss).reshape((1, bsz, ts) + kv_shape)
    win_ext = jnp.concatenate([state_win_kv[0], to_bt(kvws).reshape((bsz, ts) + kv_shape)], axis=1)
    w_len = win_ext.shape[1]
    win_s = win_ext[:, w_len - min(WINDOW, w_len):][None]
    conv_s = conv_ext[-(CONV_WIDTH - 1):].transpose(1, 0, 2)[None]

    wi_o = w_in_odd[0].astype(BF16)
    wg_o = w_pool_grp[0].astype(BF16)
    wo_o = w_out_odd[0].astype(BF16)
    sc_o = vec(pool_scale[0])
    g1, b1 = vec(ln_g[1]), vec(ln_b[1])
    n_keep = POOL_MAX - 1
    yp, tail_p = _pool_prompt(xp1, wi_o, wg_o, sc_o, wo_o, g1, b1, batch, seq, 256)
    pool_p = tail_p[:, POOL_HALO - n_keep:][None]
    st_tm = state_pool[0].transpose(1, 0, 2)
    ys, vs_new = _pool_sample(xs1, st_tm, wi_o, wg_o, sc_o, wo_o, g1, b1, ts, bsz, past)
    pool_ext = jnp.concatenate([st_tm, vs_new.reshape(ts, bsz, W_POOL)], axis=0)
    pool_s = pool_ext[-n_keep:].transpose(1, 0, 2)[None]

    y_prompt = yp.reshape(batch, seq, D_MODEL)
    y_sample = ys.reshape(ts, bsz, D_MODEL).transpose(1, 0, 2)
    return (y_prompt, y_sample, kvc_p, kvs_p, win_p, conv_p, pool_p, kvc_s, kvs_s, win_s, conv_s, pool_s)
```

```python
import functools

import jax
import jax.numpy as jnp
import numpy as np
from jax import lax
from jax.experimental import pallas as pl
from jax.experimental.pallas import tpu as pltpu

F32 = jnp.float32
BF16 = jnp.bfloat16

D_MODEL = 1024
W_CONV = 512
CONV_WIDTH = 31
N_HEADS = 8
HEAD_DIM = 64
KV_HEADS = 2
GROUP = N_HEADS // KV_HEADS
W_ATTN = N_HEADS * HEAD_DIM
KV_W = 2 * KV_HEADS * HEAD_DIM
V_OFF = KV_HEADS * HEAD_DIM
CMP_STRIDE = 16
CMP_BLOCK = 2 * CMP_STRIDE
SEL_BLOCK = 64
N_SEL = 16
WINDOW = 512
W_POOL = 1024
POOL_WINDOWS = (2, 4, 8, 16)
POOL_GROUP_W = W_POOL // len(POOL_WINDOWS)
POOL_MAX = 16
LN_EPS = 1e-5
NEG_INF = -1e30
SEL_FORCE = 1e9
DEPTH = 2
DEEPNORM_ALPHA = (2 * DEPTH) ** 0.25
EVEN_SPLITS = (W_CONV, W_CONV, W_CONV, W_ATTN, KV_W, KV_W, KV_W, 3 * N_HEADS, W_ATTN)

LANES = 128
SUBLANES = 8
E_PAD = 3 * W_CONV + 2 * W_ATTN + 3 * KV_W + KV_HEADS * LANES
VMEM_LIMIT = 56 * 1024 * 1024
Q_TILE = 128
K_TILE = 512
TILES_PER_TRIP = 4
CMP_LANES = 2048
CMP_PIECE = 512
PER_SEL = SEL_BLOCK // CMP_STRIDE
LOWEST = -3.0e38
MASKED_ROW_FLOOR = 0.5 * NEG_INF


def _cparams(sem):
    return pltpu.CompilerParams(dimension_semantics=sem, vmem_limit_bytes=VMEM_LIMIT)


def _round_up(x, m):
    return m * (-(-x // m))


def _sigmoid(x):
    return 1.0 / (1.0 + jnp.exp(-x))


def _silu(x):
    return x * _sigmoid(x)


def _layer_norm(z, g, b):
    mu = jnp.mean(z, axis=-1, keepdims=True)
    zc = z - mu
    var = jnp.mean(zc * zc, axis=-1, keepdims=True)
    return zc * lax.rsqrt(var + LN_EPS) * g + b


def _dot_t(a, b):
    return lax.dot_general(a, b, (((1,), (1,)), ((), ())), preferred_element_type=F32)


def _dot(a, b):
    return jnp.dot(a, b, preferred_element_type=F32)


def _split3(x):
    hi = x.astype(BF16)
    r1 = x - hi.astype(F32)
    mid = r1.astype(BF16)
    lo = (r1 - mid.astype(F32)).astype(BF16)
    return hi, mid, lo


KV_COL0 = 3 * W_CONV + W_ATTN


def _proj_common(xb, w_ref, u_ref, sg_ref, q_ref, bg_ref, gt_ref):
    def mm(lo, hi):
        return _dot(xb, w_ref[:, lo:hi])

    o = 0
    a_val = mm(o, o + W_CONV); o += W_CONV
    a_glu = mm(o, o + W_CONV); o += W_CONV
    u_ref[...] = a_val * _sigmoid(a_glu)
    sg_ref[...] = _silu(mm(o, o + W_CONV)); o += W_CONV
    q_ref[...] = (mm(o, o + W_ATTN) * (HEAD_DIM ** -0.5)).astype(BF16); o += W_ATTN
    o += 3 * KV_W
    bg_ref[...] = _silu(mm(o, o + W_ATTN)); o += W_ATTN
    gt_ref[...] = _sigmoid(mm(o, o + KV_HEADS * LANES))


def _proj_sample_kernel(x_ref, w_ref, u_ref, sg_ref, q_ref, bg_ref, gt_ref, kvc_ref, kvs_ref, kvw_ref):
    xb = x_ref[...].astype(BF16)
    _proj_common(xb, w_ref, u_ref, sg_ref, q_ref, bg_ref, gt_ref)
    for k, ref in enumerate((kvc_ref, kvs_ref, kvw_ref)):
        ref[...] = _dot(xb, w_ref[:, KV_COL0 + k * KV_W:KV_COL0 + (k + 1) * KV_W])


def _proj_prompt_kernel(x_ref, w_ref, wkv_t_ref, u_ref, sg_ref, q_ref, bg_ref, gt_ref, kvc_t_ref, kvs_t_ref,
                        kvw_t_ref, kaug_ref, vaug_ref, kw_ref, vwa_ref, *, tm, nt, n_sel):
    xb = x_ref[...].astype(BF16)
    _proj_common(xb, w_ref, u_ref, sg_ref, q_ref, bg_ref, gt_ref)
    kvc_t_ref[0] = _dot_t(wkv_t_ref[0:KV_W, :], xb)
    ks_t = _dot_t(wkv_t_ref[KV_W:2 * KV_W, :], xb)
    kvs_t_ref[0] = ks_t
    kw_t = _dot_t(wkv_t_ref[2 * KV_W:3 * KV_W, :], xb)
    kvw_t_ref[0] = kw_t
    key = (pl.program_id(0) % nt) * tm + lax.broadcasted_iota(jnp.int32, (n_sel, tm), 1)
    blk = lax.broadcasted_iota(jnp.int32, (n_sel, tm), 0)
    onehot = jnp.where(key // SEL_BLOCK == blk, NEG_INF, 0.0).astype(BF16)
    ones_row = jnp.where(lax.broadcasted_iota(jnp.int32, (LANES - HEAD_DIM, tm), 0) == 0, 1.0, 0.0).astype(BF16)
    n_zero = kaug_ref.shape[2] - n_sel - HEAD_DIM
    for h in range(KV_HEADS):
        k_rows = slice(h * HEAD_DIM, (h + 1) * HEAD_DIM)
        v_rows = slice(V_OFF + h * HEAD_DIM, V_OFF + (h + 1) * HEAD_DIM)
        kaug_ref[0, h, 0:n_sel, :] = onehot
        kaug_ref[0, h, n_sel:n_sel + HEAD_DIM, :] = ks_t[k_rows].astype(BF16)
        if n_zero:
            kaug_ref[0, h, n_sel + HEAD_DIM:, :] = jnp.zeros((n_zero, tm), BF16)
        vaug_ref[0, h, 0:HEAD_DIM, :] = ks_t[v_rows].astype(BF16)
        vaug_ref[0, h, HEAD_DIM:, :] = ones_row
        kw_ref[0, h] = kw_t[k_rows].astype(BF16)
        vwa_ref[0, h, 0:HEAD_DIM, :] = kw_t[v_rows].astype(BF16)
        vwa_ref[0, h, HEAD_DIM:, :] = ones_row


def _proj_row_specs(n, tm):
    row = lambda w: pl.BlockSpec((tm, w), lambda i: (i, 0))
    widths = (W_CONV, W_CONV, W_ATTN, W_ATTN, KV_HEADS * LANES)
    dtypes = (F32, F32, BF16, F32, F32)
    return row, [row(w) for w in widths], [jax.ShapeDtypeStruct((n, w), d) for w, d in zip(widths, dtypes)]


def _proj_sample(x2d, w_pad):
    n = x2d.shape[0]
    row, specs, shapes = _proj_row_specs(n, n)
    return pl.pallas_call(
        _proj_sample_kernel,
        grid=(1,),
        in_specs=[row(D_MODEL), pl.BlockSpec((D_MODEL, E_PAD), lambda i: (0, 0))],
        out_specs=specs + [row(KV_W)] * 3,
        out_shape=shapes + [jax.ShapeDtypeStruct((n, KV_W), F32)] * 3,
        compiler_params=_cparams(("arbitrary",)),
        name="proj_sample",
    )(x2d, w_pad)


def _proj_prompt(x2d, w_pad, batch, seq, tm):
    n = x2d.shape[0]
    wkv_t = w_pad[:, KV_COL0:KV_COL0 + 3 * KV_W].T
    nt = seq // tm
    n_sel = -(-seq // SEL_BLOCK)
    k_rows = _round_up(n_sel + HEAD_DIM, LANES)
    row, specs, shapes = _proj_row_specs(n, tm)
    chan = pl.BlockSpec((1, KV_W, tm), lambda i: (i // nt, 0, i % nt))
    head = lambda r: pl.BlockSpec((1, KV_HEADS, r, tm), lambda i: (i // nt, 0, 0, i % nt))
    hshape = lambda r: jax.ShapeDtypeStruct((batch, KV_HEADS, r, seq), BF16)
    return pl.pallas_call(
        functools.partial(_proj_prompt_kernel, tm=tm, nt=nt, n_sel=n_sel),
        grid=(n // tm,),
        in_specs=[row(D_MODEL), pl.BlockSpec((D_MODEL, E_PAD), lambda i: (0, 0)),
                  pl.BlockSpec((3 * KV_W, D_MODEL), lambda i: (0, 0))],
        out_specs=specs + [chan] * 3 + [head(k_rows), head(LANES), head(HEAD_DIM), head(LANES)],
        out_shape=shapes + [jax.ShapeDtypeStruct((batch, KV_W, seq), F32)] * 3 + [
            hshape(k_rows), hshape(LANES), hshape(HEAD_DIM), hshape(LANES)],
        compiler_params=_cparams(("parallel",)),
        name="proj_prompt",
    )(x2d, w_pad, wkv_t)


def _pad_even_weights(w):
    offs = np.cumsum(EVEN_SPLITS)[:-1].tolist()
    a_val, a_glu, a_gate, wq, wkc, wks, wkw, wg, wbg = jnp.split(w, offs, axis=1)
    wg = wg.reshape(D_MODEL, KV_HEADS, GROUP, 3).transpose(0, 1, 3, 2).reshape(D_MODEL, KV_HEADS, 3 * GROUP)
    wg = jnp.pad(wg, ((0, 0), (0, 0), (0, LANES - 3 * GROUP))).reshape(D_MODEL, KV_HEADS * LANES)
    return jnp.concatenate([a_val, a_glu, a_gate, wq, wkc, wks, wkw, wbg, wg], axis=1).astype(BF16)


CONV_HALO = 32
CONV_CHUNK = 32


def _conv_prompt_kernel(u_ref, sg_ref, w_ref, cb_ref, g_ref, b_ref, o_ref, ext_ref, sh_ref, *, tt):
    @pl.when(pl.program_id(1) == 0)
    def _():
        ext_ref[0:CONV_HALO, :] = jnp.zeros((CONV_HALO, W_CONV), F32)

    ext_ref[CONV_HALO:CONV_HALO + tt, :] = u_ref[...]
    base = CONV_HALO - (CONV_WIDTH - 1)
    n_sh = sh_ref.shape[1]
    for s in range(1, SUBLANES):
        sh_ref[s - 1] = ext_ref[s:s + n_sh, :]

    def window(start):
        s, off = start % SUBLANES, start - start % SUBLANES
        return ext_ref[off:off + CONV_CHUNK, :] if s == 0 else sh_ref[s - 1, off:off + CONV_CHUNK, :]

    for c in range(tt // CONV_CHUNK):
        r0 = c * CONV_CHUNK
        acc = jnp.zeros((CONV_CHUNK, W_CONV), F32) + cb_ref[...]
        for k in range(CONV_WIDTH):
            acc = acc + w_ref[k:k + 1, :] * window(base + r0 + k)
        y = _layer_norm(acc, g_ref[...], b_ref[...])
        o_ref[r0:r0 + CONV_CHUNK, :] = (_silu(y) * sg_ref[r0:r0 + CONV_CHUNK, :]).astype(o_ref.dtype)
    ext_ref[0:CONV_HALO, :] = ext_ref[tt:tt + CONV_HALO, :]


def _conv_prompt(u2d, sg2d, conv_w, conv_b, ln_g, ln_b, batch, seq, tt):
    nt = seq // tt
    row = pl.BlockSpec((tt, W_CONV), lambda b, i: (b * nt + i, 0))
    vec = pl.BlockSpec((1, W_CONV), lambda b, i: (0, 0))
    return pl.pallas_call(
        functools.partial(_conv_prompt_kernel, tt=tt),
        grid=(batch, nt),
        in_specs=[row, row, pl.BlockSpec((CONV_WIDTH, W_CONV), lambda b, i: (0, 0)), vec, vec, vec],
        out_specs=row,
        out_shape=jax.ShapeDtypeStruct((batch * seq, W_CONV), BF16),
        scratch_shapes=[pltpu.VMEM((tt + CONV_HALO, W_CONV), F32),
                        pltpu.VMEM((SUBLANES - 1, tt + CONV_HALO - SUBLANES, W_CONV), F32)],
        compiler_params=_cparams(("arbitrary", "arbitrary")),
        name="conv_prompt",
    )(u2d, sg2d, conv_w, conv_b, ln_g, ln_b)


def _conv_sample_kernel(ext_ref, sg_ref, w_ref, cb_ref, g_ref, b_ref, o_ref, *, ts):
    for t in range(ts):
        acc = jnp.zeros(ext_ref.shape[1:], F32) + cb_ref[...]
        for k in range(CONV_WIDTH):
            acc = acc + w_ref[k:k + 1, :] * ext_ref[t + k]
        y = _layer_norm(acc, g_ref[...], b_ref[...])
        o_ref[t] = _silu(y) * sg_ref[t]


def _conv_sample(ext, sg, conv_w, conv_b, ln_g, ln_b):
    ts, bsz, _ = sg.shape
    return pl.pallas_call(
        functools.partial(_conv_sample_kernel, ts=ts),
        out_shape=jax.ShapeDtypeStruct((ts, bsz, W_CONV), F32),
        name="conv_sample",
    )(ext, sg, conv_w, conv_b, ln_g, ln_b)


def _compress_span(x_ext, w1_ref, w2_ref, s_ref):
    lanes = x_ext.shape[1] - LANES
    piece = min(lanes, CMP_PIECE)
    reps = piece // LANES
    out = None
    for c in range(lanes // piece):
        xp = x_ext[:, c * piece:(c + 1) * piece + LANES]
        later = pltpu.roll(xp, piece + LANES - CMP_STRIDE, 1)[:, 0:piece]
        z = xp[:, 0:piece] * jnp.tile(w1_ref[...], (1, reps)) + later * jnp.tile(w2_ref[...], (1, reps))
        hi = z.astype(BF16)
        lo = (z - hi.astype(F32)).astype(BF16)
        seg = s_ref[c * piece:(c + 1) * piece, :]
        part = _dot(hi, seg) + _dot(lo, seg)
        out = part if out is None else out + part
    return out


def _compress_prompt_kernel(x_ref, w1_ref, w2_ref, s_ref, o_ref, *, lanes):
    seq = x_ref.shape[2]
    parts = []
    for c in range(seq // lanes):
        if (c + 1) * lanes + LANES <= seq:
            x_ext = x_ref[0, :, c * lanes:(c + 1) * lanes + LANES]
        else:
            x_ext = jnp.concatenate([x_ref[0, :, c * lanes:(c + 1) * lanes], jnp.zeros((KV_W, LANES), F32)], axis=1)
        parts.append(_compress_span(x_ext, w1_ref, w2_ref, s_ref))
    o_ref[0] = jnp.concatenate(parts, axis=1)


def _compress_prompt(kvc_t, w1t, w2t, seg):
    batch, _, seq = kvc_t.shape
    lanes = seg.shape[0]
    const = lambda a: pl.BlockSpec(a.shape, lambda b: (0, 0))
    return pl.pallas_call(
        functools.partial(_compress_prompt_kernel, lanes=lanes),
        grid=(batch,),
        in_specs=[pl.BlockSpec((1, KV_W, seq), lambda b: (b, 0, 0)), const(w1t), const(w2t), const(seg)],
        out_specs=pl.BlockSpec((1, KV_W, seq // CMP_STRIDE), lambda b: (b, 0, 0)),
        out_shape=jax.ShapeDtypeStruct((batch, KV_W, seq // CMP_STRIDE), F32),
        compiler_params=_cparams(("parallel",)),
        name="compress_prompt",
    )(kvc_t, w1t, w2t, seg)


def _compress_paged_kernel(pt_ref, *refs, n_pages_step):
    del pt_ref
    pages = refs[:n_pages_step + 1]
    w1_ref, w2_ref, s_ref, o_ref = refs[n_pages_step + 1:]
    x_ext = jnp.concatenate([p[0] for p in pages[:-1]] + [pages[-1][0, :, 0:LANES]], axis=1)
    o_ref[0] = _compress_span(x_ext, w1_ref, w2_ref, s_ref)


def _compress_paged(cache_t, page_table, w1t, w2t, seg):
    _, _, page = cache_t.shape
    bsz, n_pages = page_table.shape
    lanes = seg.shape[0]
    pps = lanes // page
    assert page >= LANES and lanes % page == 0 and n_pages % pps == 0
    n_steps = n_pages // pps
    n_out = lanes // CMP_STRIDE

    def page_spec(k):
        return pl.BlockSpec((1, KV_W, page),
                            lambda b, s, pt: (pt[b, jnp.minimum(s * pps + k, n_pages - 1)], 0, 0))

    const = lambda a: pl.BlockSpec(a.shape, lambda b, s, pt: (0, 0))
    grid_spec = pltpu.PrefetchScalarGridSpec(
        num_scalar_prefetch=1,
        grid=(bsz, n_steps),
        in_specs=[page_spec(k) for k in range(pps + 1)] + [const(w1t), const(w2t), const(seg)],
        out_specs=pl.BlockSpec((1, KV_W, n_out), lambda b, s, pt: (b, 0, s)),
    )
    return pl.pallas_call(
        functools.partial(_compress_paged_kernel, n_pages_step=pps),
        grid_spec=grid_spec,
        out_shape=jax.ShapeDtypeStruct((bsz, KV_W, n_steps * n_out), F32),
        compiler_params=_cparams(("parallel", "arbitrary")),
        name="compress_paged",
    )(page_table, *([cache_t] * (pps + 1)), w1t, w2t, seg)


def _compress_consts(w_cmp2d, lanes):
    w1t = jnp.tile(w_cmp2d[:CMP_STRIDE].T, (1, LANES // CMP_STRIDE))
    w2t = jnp.tile(w_cmp2d[CMP_STRIDE:].T, (1, LANES // CMP_STRIDE))
    seg = jnp.asarray(np.arange(lanes)[:, None] // CMP_STRIDE == np.arange(lanes // CMP_STRIDE)[None, :], BF16)
    return w1t, w2t, seg


def _pick_top_block(sc, sel, blk):
    mx = jnp.max(sc, axis=0, keepdims=True)
    jm = jnp.min(jnp.where(sc == mx, blk, sc.shape[0]), axis=0, keepdims=True)
    pick = blk == jm
    return jnp.where(pick, LOWEST, sc), jnp.where(pick, 1.0, sel)


def _force_scores(score, blk, cur):
    forced = (blk == 0) | ((blk >= cur - 1) & (blk <= cur))
    return jnp.where(forced, SEL_FORCE, jnp.where(blk <= cur, score, -SEL_FORCE))


def _sel_matrix(n_sel_blocks, n_cols):
    n = np.arange(n_cols)
    m = (n[None, :] // PER_SEL == np.arange(n_sel_blocks)[:, None]) & (n[None, :] % PER_SEL < PER_SEL - 1)
    return jnp.asarray(m, BF16)


def _cmp_valid(n, qpos, n_cmp):
    return n * CMP_STRIDE + (CMP_BLOCK - 1) <= jnp.minimum(qpos, (n_cmp - 1) * CMP_STRIDE + CMP_BLOCK - 1)


def _nsa_prompt_kernel(q_ref, kaug_ref, vaug_ref, kw_ref, vwa_ref, kc_ref, vc_ref, gt_ref, bg_ref, m01_ref, o_ref,
                       qa_ref, acc_ref, mx_ref, s_ref, *, n_cmp, n_sel):
    i = pl.program_id(2)
    rows = GROUP * Q_TILE
    q = q_ref[...]
    qs = jnp.concatenate([q[:, g * HEAD_DIM:(g + 1) * HEAD_DIM] for g in range(GROUP)], axis=0)
    n_cols = kc_ref.shape[3]

    def row_pos(width):
        return i * Q_TILE + (lax.broadcasted_iota(jnp.int32, (rows, width), 0) & (Q_TILE - 1))

    tpos = i * Q_TILE + lax.broadcasted_iota(jnp.int32, (Q_TILE, n_cols), 0)
    c_bias = jnp.where(_cmp_valid(lax.broadcasted_iota(jnp.int32, (Q_TILE, n_cols), 1), tpos, n_cmp), 0.0, NEG_INF)
    s_all = _dot(qs, kc_ref[0, 0].astype(BF16))
    p_parts = []
    p_grp = None
    for g in range(GROUP):
        s = s_all[g * Q_TILE:(g + 1) * Q_TILE] + c_bias
        e = jnp.exp(s - jnp.maximum(jnp.max(s, axis=1, keepdims=True), MASKED_ROW_FLOOR))
        l = jnp.sum(e, axis=1, keepdims=True)
        p = e * (1.0 / jnp.where(l > 0.0, l, 1.0))
        p_parts.append(p.astype(BF16))
        p_grp = p if p_grp is None else p_grp + p
    o_c = _dot_t(jnp.concatenate(p_parts, axis=0), vc_ref[0, 0].astype(BF16))

    score_t = sum(_dot_t(m01_ref[...], part) for part in _split3(p_grp))
    blk = lax.broadcasted_iota(jnp.int32, (n_sel, Q_TILE), 0)
    tok = i * Q_TILE + lax.broadcasted_iota(jnp.int32, (n_sel, Q_TILE), 1)
    sc = _force_scores(score_t, blk, tok // SEL_BLOCK)
    sel_t = jnp.zeros(sc.shape, F32)
    n_pick = min(N_SEL, n_sel)

    w_keys = WINDOW + Q_TILE
    w0 = pl.multiple_of(jnp.maximum(i * Q_TILE - WINDOW, 0), Q_TILE)
    rel = (i * Q_TILE + lax.broadcasted_iota(jnp.int32, (Q_TILE, w_keys), 0)
           - (w0 + lax.broadcasted_iota(jnp.int32, (Q_TILE, w_keys), 1)))
    w_bias = jnp.where((rel >= 0) & (rel < WINDOW), 0.0, NEG_INF)
    s_w_all = _dot(qs, kw_ref[0, 0, :, pl.ds(w0, w_keys)])
    e_w_parts = []
    done = 0
    for g in range(GROUP):
        s_w = s_w_all[g * Q_TILE:(g + 1) * Q_TILE] + w_bias
        e_w_parts.append(jnp.exp(s_w - jnp.max(s_w, axis=1, keepdims=True)).astype(BF16))
        upto = n_pick * (g + 1) // GROUP
        for _ in range(done, upto):
            sc, sel_t = _pick_top_block(sc, sel_t, blk)
        done = upto
    acc_w = _dot_t(jnp.concatenate(e_w_parts, axis=0), vwa_ref[0, 0, :, pl.ds(w0, w_keys)])
    o_w = acc_w[:, 0:HEAD_DIM] / acc_w[:, HEAD_DIM:HEAD_DIM + 1]
    not_sel = (1.0 - sel_t).T.astype(BF16)
    for g in range(GROUP):
        qa_ref[g * Q_TILE:(g + 1) * Q_TILE, 0:n_sel] = not_sel
    qa_ref[:, n_sel:n_sel + HEAD_DIM] = qs
    if qa_ref.shape[1] > n_sel + HEAD_DIM:
        qa_ref[:, n_sel + HEAD_DIM:] = jnp.zeros((rows, qa_ref.shape[1] - n_sel - HEAD_DIM), BF16)

    mx_ref[...] = jnp.full(mx_ref.shape, NEG_INF, F32)

    def score_tiles(j, n_tiles, causal_last):
        for t in range(n_tiles):
            k0 = pl.multiple_of((j + t) * K_TILE, K_TILE)
            s = _dot(qa_ref[...], kaug_ref[0, 0, :, pl.ds(k0, K_TILE)])
            if causal_last and t == n_tiles - 1:
                kpos = k0 + lax.broadcasted_iota(jnp.int32, (rows, K_TILE), 1)
                s = jnp.where(kpos <= row_pos(K_TILE), s, NEG_INF)
            s_ref[j + t] = s
            part = s[:, 0:LANES]
            for c in range(1, K_TILE // LANES):
                part = jnp.maximum(part, s[:, c * LANES:(c + 1) * LANES])
            mx_ref[...] = jnp.maximum(mx_ref[...], part)

    def acc_tiles(j, n_tiles):
        m_b = jnp.tile(mx_ref[...], (1, K_TILE // LANES))
        for t in range(n_tiles):
            k0 = pl.multiple_of((j + t) * K_TILE, K_TILE)
            pe = jnp.exp(s_ref[j + t] - m_b).astype(BF16)
            acc_ref[...] += _dot_t(pe, vaug_ref[0, 0, :, pl.ds(k0, K_TILE)])

    def grouped(n, fn, tail_fn, tail_min):
        def body(jj, carry):
            fn(TILES_PER_TRIP * jj, TILES_PER_TRIP)
            return carry
        lax.fori_loop(0, n // TILES_PER_TRIP, body, 0)
        for r in range(tail_min, TILES_PER_TRIP):
            @pl.when(n % TILES_PER_TRIP == r)
            def _(r=r):
                tail_fn(n - r, r)

    n_full = (i * Q_TILE) // K_TILE
    grouped(n_full, lambda j, n: score_tiles(j, n, False), lambda j, r: score_tiles(j, r + 1, True), 0)

    mx_ref[...] = jnp.broadcast_to(jnp.max(mx_ref[...], axis=1, keepdims=True), mx_ref.shape)
    acc_ref[...] = jnp.zeros(acc_ref.shape, F32)
    grouped(n_full + 1, acc_tiles, acc_tiles, 1)

    acc = acc_ref[...]
    o_s = acc[:, 0:HEAD_DIM] / acc[:, HEAD_DIM:HEAD_DIM + 1]

    gt = gt_ref[...]
    outs = []
    for g in range(GROUP):
        r = slice(g * Q_TILE, (g + 1) * Q_TILE)
        outs.append(gt[:, g:g + 1] * o_c[r] + gt[:, GROUP + g:GROUP + g + 1] * o_s[r]
                    + gt[:, 2 * GROUP + g:2 * GROUP + g + 1] * o_w[r])
    o_ref[...] = (jnp.concatenate(outs, axis=1) * bg_ref[...]).astype(o_ref.dtype)


def _nsa_prompt(q2d, kaug, vaug, kw, vwa, kcv4, gt2d, bg2d, m01, batch, seq, n_cmp, n_sel):
    nq = seq // Q_TILE
    rows = GROUP * Q_TILE
    n_cols = kcv4.shape[-1]
    per_head = lambda a: pl.BlockSpec((1, 1) + a.shape[2:], lambda b, h, i: (b, h, 0, 0))
    return pl.pallas_call(
        functools.partial(_nsa_prompt_kernel, n_cmp=n_cmp, n_sel=n_sel),
        grid=(batch, KV_HEADS, nq),
        in_specs=[pl.BlockSpec((Q_TILE, GROUP * HEAD_DIM), lambda b, h, i: (b * nq + i, h)),
                  per_head(kaug), per_head(vaug), per_head(kw), per_head(vwa),
                  pl.BlockSpec((1, 1, HEAD_DIM, n_cols), lambda b, h, i: (b, h, 0, 0)),
                  pl.BlockSpec((1, 1, HEAD_DIM, n_cols), lambda b, h, i: (b, KV_HEADS + h, 0, 0)),
                  pl.BlockSpec((Q_TILE, LANES), lambda b, h, i: (b * nq + i, h)),
                  pl.BlockSpec((Q_TILE, GROUP * HEAD_DIM), lambda b, h, i: (b * nq + i, h)),
                  pl.BlockSpec(m01.shape, lambda b, h, i: (0, 0))],
        out_specs=pl.BlockSpec((Q_TILE, GROUP * HEAD_DIM), lambda b, h, i: (b * nq + i, h)),
        out_shape=jax.ShapeDtypeStruct((batch * seq, W_ATTN), BF16),
        scratch_shapes=[pltpu.VMEM((rows, kaug.shape[2]), BF16), pltpu.VMEM((rows, LANES), F32),
                        pltpu.VMEM((rows, LANES), F32), pltpu.VMEM((seq // K_TILE, rows, K_TILE), F32)],
        compiler_params=_cparams(("parallel", "parallel", "arbitrary")),
        name="nsa_prompt",
    )(q2d, kaug, vaug, kw, vwa, kcv4, kcv4, gt2d, bg2d, m01)


def _new_token_scores(qf, nk, tq, ts):
    cols = []
    for t in range(ts):
        s = jnp.sum(qf * nk[t:t + 1, :], axis=1, keepdims=True)
        cols.append(jnp.where(tq >= t, s, NEG_INF))
    return cols


def _bf16_round(x):
    return x.astype(BF16).astype(F32)


def _nsa_sample_a_kernel(q_ref, kcv_ref, win_ref, new_ref, gt_ref, m01_ref, part_ref, idx_ref, *,
                         past, ts, n_cmp):
    for bi in range(q_ref.shape[0]):
        _nsa_sample_a_one(bi, q_ref, kcv_ref, win_ref, new_ref, gt_ref, m01_ref, part_ref, idx_ref,
                          past=past, ts=ts, n_cmp=n_cmp)


def _nsa_sample_a_one(bi, q_ref, kcv_ref, win_ref, new_ref, gt_ref, m01_ref, part_ref, idx_ref, *,
                      past, ts, n_cmp):
    rows = GROUP * ts
    kcv = kcv_ref[bi]
    win = win_ref[bi]
    new = new_ref[bi]
    n_cols = kcv.shape[1]
    w_hist = win.shape[1]
    n_blk_pad = m01_ref.shape[0]
    tq = lax.broadcasted_iota(jnp.int32, (rows, 1), 0) % ts
    qpos = past + tq
    scores = []
    for h in range(KV_HEADS):
        qh = q_ref[bi, h]
        ksl = slice(h * HEAD_DIM, (h + 1) * HEAD_DIM)
        vsl = slice(V_OFF + h * HEAD_DIM, V_OFF + (h + 1) * HEAD_DIM)
        s = _dot(qh, kcv[ksl].astype(BF16))
        valid = _cmp_valid(lax.broadcasted_iota(jnp.int32, (rows, n_cols), 1), qpos, n_cmp)
        s = jnp.where(valid, s, NEG_INF)
        e = jnp.where(valid, jnp.exp(s - jnp.max(s, axis=1, keepdims=True)), 0.0)
        l = jnp.sum(e, axis=1, keepdims=True)
        p = e / jnp.where(l > 0.0, l, 1.0)
        o_c = _dot_t(p.astype(BF16), kcv[vsl].astype(BF16))
        p_grp = p[0:ts]
        for g in range(1, GROUP):
            p_grp = p_grp + p[g * ts:(g + 1) * ts]
        scores.append(sum(_dot_t(part, m01_ref[...]) for part in _split3(p_grp)))

        s1 = _dot(qh, win[ksl].astype(BF16))
        rel = qpos - (past - w_hist + lax.broadcasted_iota(jnp.int32, (rows, w_hist), 1))
        valid = (rel >= 0) & (rel <= jnp.minimum(qpos, WINDOW - 1))
        s1 = jnp.where(valid, s1, NEG_INF)
        s2 = _new_token_scores(qh.astype(F32), _bf16_round(new[:, ksl]), tq, ts)
        m = jnp.max(s1, axis=1, keepdims=True)
        for c in s2:
            m = jnp.maximum(m, c)
        e1 = jnp.where(valid, jnp.exp(s1 - m), 0.0)
        l = jnp.sum(e1, axis=1, keepdims=True)
        o_w = _dot_t(e1.astype(BF16), win[vsl].astype(BF16))
        nv = _bf16_round(new[:, vsl])
        for t in range(ts):
            e2 = jnp.exp(s2[t] - m)
            l = l + e2
            o_w = o_w + _bf16_round(e2) * nv[t:t + 1, :]
        gt = gt_ref[bi, h]
        part_ref[bi, h] = gt[:, 0:1] * o_c + gt[:, 2:3] * (o_w / l)

    sc = jnp.concatenate(scores, axis=0)
    nr = KV_HEADS * ts
    blk = lax.broadcasted_iota(jnp.int32, (nr, n_blk_pad), 1)
    cur = (past + (lax.broadcasted_iota(jnp.int32, (nr, 1), 0) % ts)) // SEL_BLOCK
    sc = _force_scores(sc, blk, cur)
    lane = lax.broadcasted_iota(jnp.int32, (nr, LANES), 1)
    idx = jnp.zeros((nr, LANES), jnp.int32)
    for k in range(N_SEL):
        mx = jnp.max(sc, axis=1, keepdims=True)
        jm = jnp.min(jnp.where(sc == mx, blk, n_blk_pad), axis=1, keepdims=True)
        idx = jnp.where(lane == k, jm, idx)
        sc = jnp.where(blk == jm, LOWEST, sc)
    idx_ref[bi] = idx


def _nsa_sample_a(q4, kcv_t, win_t, new_w, gt4, m01, past, ts, n_cmp):
    bsz = q4.shape[0]
    rows = GROUP * ts
    nb = max(d for d in (4, 2, 1) if bsz % d == 0)
    per_b = lambda a: pl.BlockSpec((nb,) + a.shape[1:], lambda b: (b,) + (0,) * (a.ndim - 1))
    return pl.pallas_call(
        functools.partial(_nsa_sample_a_kernel, past=past, ts=ts, n_cmp=n_cmp),
        grid=(bsz // nb,),
        in_specs=[per_b(q4), per_b(kcv_t), per_b(win_t), per_b(new_w), per_b(gt4),
                  pl.BlockSpec(m01.shape, lambda b: (0, 0))],
        out_specs=[pl.BlockSpec((nb, KV_HEADS, rows, HEAD_DIM), lambda b: (b, 0, 0, 0)),
                   pl.BlockSpec((nb, KV_HEADS * ts, LANES), lambda b: (b, 0, 0))],
        out_shape=[jax.ShapeDtypeStruct((bsz, KV_HEADS, rows, HEAD_DIM), F32),
                   jax.ShapeDtypeStruct((bsz, KV_HEADS * ts, LANES), jnp.int32)],
        compiler_params=_cparams(("parallel",)),
        name="nsa_sample_scores",
    )(q4, kcv_t, win_t, new_w, gt4, m01)


def _nsa_sample_b_kernel(idx_ref, pt_ref, *refs, past, ts, n_past_blocks, page):
    del pt_ref
    kv_blocks = refs[:N_SEL]
    q_ref, new_ref, gt_ref, part_ref, o_ref = refs[N_SEL:]
    r = pl.program_id(0)
    t = r % ts
    head0 = ((r // ts) % KV_HEADS) == 0
    bpp = page // SEL_BLOCK
    q = q_ref[0]
    lane = lax.broadcasted_iota(jnp.int32, (GROUP, page), 1)

    def pick(x, off):
        return jnp.where(head0, x[:, off:off + HEAD_DIM], x[:, off + HEAD_DIM:off + 2 * HEAD_DIM])

    s_all = _dot(q, jnp.concatenate([kv_blocks[k][0, 0, 0].astype(BF16) for k in range(N_SEL)], axis=1))
    s_list = []
    has_new = False
    for k in range(N_SEL):
        j = idx_ref[r * N_SEL + k]
        is_past = j < n_past_blocks
        has_new = jnp.logical_or(has_new, jnp.logical_not(is_past))
        off = (j % bpp) * SEL_BLOCK
        page_start = (j // bpp) * page
        lo = jnp.where(is_past, off, page)
        hi = jnp.minimum(off + SEL_BLOCK, past + t + 1 - page_start)
        s_list.append(jnp.where((lane >= lo) & (lane < hi), s_all[:, k * page:(k + 1) * page], NEG_INF))
    new = new_ref[0]
    tq = jnp.where(has_new, t, -1) + jnp.zeros((GROUP, 1), jnp.int32)
    s_new = _new_token_scores(q.astype(F32), _bf16_round(pick(new, 0)), tq, ts)
    m = s_new[0]
    for c in s_new[1:]:
        m = jnp.maximum(m, c)
    for s in s_list:
        m = jnp.maximum(m, jnp.max(s, axis=1, keepdims=True))
    nv = _bf16_round(pick(new, V_OFF))
    l = jnp.zeros((GROUP, 1), F32)
    o = jnp.zeros((GROUP, HEAD_DIM), F32)
    for tt in range(ts):
        e2 = jnp.exp(s_new[tt] - m)
        l = l + e2
        o = o + _bf16_round(e2) * nv[tt:tt + 1, :]
    e_all = jnp.exp(jnp.concatenate(s_list, axis=1) - m)
    l = l + jnp.sum(e_all, axis=1, keepdims=True)
    o = o + _dot_t(e_all.astype(BF16),
                   jnp.concatenate([kv_blocks[k][0, 1, 0].astype(BF16) for k in range(N_SEL)], axis=1))
    gt = gt_ref[0]
    o_ref[0] = part_ref[0] + gt[:, 1:2] * (o / l)


def _nsa_sample_b(idx_flat, page_flat, cache_slabs, q_rows, new_rows, gt_rows, part_rows, past, ts, page):
    n_rows = q_rows.shape[0]
    n_past_blocks = past // SEL_BLOCK

    def slab_spec(k):
        return pl.BlockSpec((1, 2, 1, HEAD_DIM, page),
                            lambda r, idx, pg: (pg[r * N_SEL + k], 0, (r // ts) % KV_HEADS, 0, 0))

    row3 = lambda a: pl.BlockSpec((1,) + a.shape[1:], lambda r, idx, pt: (r, 0, 0))
    grid_spec = pltpu.PrefetchScalarGridSpec(
        num_scalar_prefetch=2,
        grid=(n_rows,),
        in_specs=[slab_spec(k) for k in range(N_SEL)] + [
            row3(q_rows),
            pl.BlockSpec((1,) + new_rows.shape[1:], lambda r, idx, pt: (r // (KV_HEADS * ts), 0, 0)),
            row3(gt_rows), row3(part_rows)],
        out_specs=row3(part_rows),
    )
    return pl.pallas_call(
        functools.partial(_nsa_sample_b_kernel, past=past, ts=ts, n_past_blocks=n_past_blocks, page=page),
        grid_spec=grid_spec,
        out_shape=jax.ShapeDtypeStruct(part_rows.shape, F32),
        compiler_params=_cparams(("arbitrary",)),
        name="nsa_sample_select",
    )(idx_flat, page_flat, *([cache_slabs] * N_SEL), q_rows, new_rows, gt_rows, part_rows)


def _channel_major(x):
    n, rows = x.shape[:2]
    return x.transpose(0, 2, 3, 4, 1).reshape(n, KV_W, rows)


def _sample_attention(q_tm, kvs_tm, kvw_tm, gt_tm, cache_c, cache_s, state_win, page_table, cmp_consts, bsz, ts):
    n_phys, page = cache_c.shape[0], cache_c.shape[1]
    n_pages = page_table.shape[1]
    past = n_pages * page
    w_hist = state_win.shape[1]
    assert ts < CMP_STRIDE and page % SEL_BLOCK == 0 and ts <= WINDOW and w_hist <= past
    total = past + ts
    n_cmp = total // CMP_STRIDE - 1
    n_sel_blocks = -(-total // SEL_BLOCK)
    kcv_t = _compress_paged(_channel_major(cache_c), page_table, *cmp_consts)
    m01 = _sel_matrix(_round_up(n_sel_blocks, LANES), kcv_t.shape[2])
    q4 = q_tm.reshape(ts, bsz, KV_HEADS, GROUP, HEAD_DIM).transpose(1, 2, 3, 0, 4).reshape(
        bsz, KV_HEADS, GROUP * ts, HEAD_DIM)
    g5 = gt_tm.reshape(ts, bsz, KV_HEADS, LANES)[..., :3 * GROUP].reshape(ts, bsz, KV_HEADS, 3, GROUP)
    g5 = g5.transpose(1, 2, 4, 0, 3).reshape(bsz, KV_HEADS, GROUP * ts, 3)
    gt4 = jnp.pad(g5, ((0, 0), (0, 0), (0, 0), (0, LANES - 3)))
    to_bt = lambda a: a.reshape(ts, bsz, KV_W).transpose(1, 0, 2)
    part, idx = _nsa_sample_a(q4, kcv_t, _channel_major(state_win), to_bt(kvw_tm), gt4, m01, past, ts, n_cmp)
    to_rows = lambda a: a.reshape(bsz, KV_HEADS, GROUP, ts, a.shape[-1]).transpose(0, 1, 3, 2, 4).reshape(
        bsz * KV_HEADS * ts, GROUP, a.shape[-1])
    slabs = _channel_major(cache_s).reshape(n_phys, 2, KV_HEADS, HEAD_DIM, page)
    idx16 = idx[:, :, :N_SEL]
    pages = jnp.take_along_axis(page_table[:, None, :],
                                jnp.minimum(idx16, past // SEL_BLOCK - 1) // (page // SEL_BLOCK), axis=2)
    o_rows = _nsa_sample_b(idx16.reshape(-1), pages.reshape(-1), slabs,
                           to_rows(q4), to_bt(kvs_tm), to_rows(gt4), to_rows(part), past, ts, page)
    return o_rows.reshape(bsz, KV_HEADS, ts, GROUP, HEAD_DIM).transpose(2, 0, 1, 3, 4).reshape(ts * bsz, W_ATTN)


def _mix_out_kernel(x_ref, a_ref, b_ref, *refs, gated):
    if gated:
        bg_ref, w_ref, g_ref, beta_ref, o_ref = refs
        b = (b_ref[...] * bg_ref[...]).astype(BF16)
    else:
        w_ref, g_ref, beta_ref, o_ref = refs
        b = b_ref[...]
    d = _dot(a_ref[...].astype(BF16), w_ref[0:W_CONV, :]) + _dot(b, w_ref[W_CONV:W_CONV + W_ATTN, :])
    o_ref[...] = _layer_norm(DEEPNORM_ALPHA * x_ref[...] + d, g_ref[...], beta_ref[...])


def _mix_out(x2d, a2d, b2d, bg2d, w_out, ln_g, ln_b, tm):
    n = x2d.shape[0]
    row = lambda w: pl.BlockSpec((tm, w), lambda i: (i, 0))
    vec = pl.BlockSpec((1, D_MODEL), lambda i: (0, 0))
    gate = [] if bg2d is None else [bg2d]
    return pl.pallas_call(
        functools.partial(_mix_out_kernel, gated=bg2d is not None),
        grid=(n // tm,),
        in_specs=[row(D_MODEL), row(W_CONV), row(W_ATTN)] + [row(W_ATTN)] * len(gate) + [
            pl.BlockSpec((W_CONV + W_ATTN, D_MODEL), lambda i: (0, 0)), vec, vec],
        out_specs=row(D_MODEL),
        out_shape=jax.ShapeDtypeStruct((n, D_MODEL), F32),
        compiler_params=_cparams(("parallel",)),
        name="mix_out",
    )(x2d, a2d, b2d, *gate, w_out, ln_g, ln_b)


POOL_HALO = 16


def _pool_tail(x, d_groups, gate, wg_ref, sc_ref, wo_ref, g_ref, beta_ref):
    mixed = jnp.concatenate([_dot(d_groups[gi].astype(BF16), wg_ref[gi]) for gi in range(len(POOL_WINDOWS))], axis=1)
    h = (mixed * sc_ref[...] * _silu(gate)).astype(BF16)
    return _layer_norm(DEEPNORM_ALPHA * x + _dot(h, wo_ref[...]), g_ref[...], beta_ref[...])


def _pool_prompt_kernel(x_ref, wi_ref, wg_ref, sc_ref, wo_ref, g_ref, beta_ref, o_ref, tail_ref, ext_ref, *, tm):
    i = pl.program_id(1)

    @pl.when(i == 0)
    def _():
        ext_ref[0:POOL_HALO, :] = jnp.zeros((POOL_HALO, W_POOL), F32)

    x = x_ref[...]
    xb = x.astype(BF16)
    v = _dot(xb, wi_ref[:, 0:W_POOL])
    gate = _dot(xb, wi_ref[:, W_POOL:2 * W_POOL])
    ext_ref[POOL_HALO:POOL_HALO + tm, :] = v
    pos = i * tm + lax.broadcasted_iota(jnp.int32, (tm, 1), 0)
    d_groups = []
    for gi, w in enumerate(POOL_WINDOWS):
        c = slice(gi * POOL_GROUP_W, (gi + 1) * POOL_GROUP_W)
        win_sum = v[:, c]
        for k in range(1, w):
            win_sum = win_sum + ext_ref[POOL_HALO - k:POOL_HALO - k + tm, c]
        cnt = jnp.minimum(pos + 1, w).astype(F32)
        d_groups.append(win_sum / cnt - v[:, c])
    o_ref[...] = _pool_tail(x, d_groups, gate, wg_ref, sc_ref, wo_ref, g_ref, beta_ref)
    ext_ref[0:POOL_HALO, :] = ext_ref[tm:tm + POOL_HALO, :]
    tail_ref[0] = ext_ref[0:POOL_HALO, :]


def _pool_prompt(x2d, w_in, w_grp, scale, w_out, ln_g, ln_b, batch, seq, tm):
    nt = seq // tm
    const = lambda a: pl.BlockSpec(a.shape, lambda b, i: (0,) * a.ndim)
    row = pl.BlockSpec((tm, D_MODEL), lambda b, i: (b * nt + i, 0))
    return pl.pallas_call(
        functools.partial(_pool_prompt_kernel, tm=tm),
        grid=(batch, nt),
        in_specs=[row, const(w_in), const(w_grp), const(scale), const(w_out), const(ln_g), const(ln_b)],
        out_specs=[row, pl.BlockSpec((1, POOL_HALO, W_POOL), lambda b, i: (b, 0, 0))],
        out_shape=[jax.ShapeDtypeStruct((batch * seq, D_MODEL), F32),
                   jax.ShapeDtypeStruct((batch, POOL_HALO, W_POOL), F32)],
        scratch_shapes=[pltpu.VMEM((tm + POOL_HALO, W_POOL), F32)],
        compiler_params=_cparams(("arbitrary", "arbitrary")),
        name="pool_prompt",
    )(x2d, w_in, w_grp, scale, w_out, ln_g, ln_b)


def _pool_sample_kernel(x_ref, st_ref, wi_ref, wg_ref, sc_ref, wo_ref, g_ref, beta_ref, o_ref, v_ref, *,
                        ts, bsz, pos0):
    x = x_ref[...]
    xb = x.astype(BF16)
    v = _dot(xb, wi_ref[:, 0:W_POOL])
    gate = _dot(xb, wi_ref[:, W_POOL:2 * W_POOL])
    v_ref[...] = v
    n_hist = st_ref.shape[0]

    def ext(r):
        return st_ref[r] if r < n_hist else v[(r - n_hist) * bsz:(r - n_hist + 1) * bsz, :]

    d_groups = []
    for gi, w in enumerate(POOL_WINDOWS):
        c = slice(gi * POOL_GROUP_W, (gi + 1) * POOL_GROUP_W)
        per_t = []
        for t in range(ts):
            win_sum = ext(n_hist + t)[:, c]
            for k in range(1, w):
                win_sum = win_sum + ext(n_hist + t - k)[:, c]
            per_t.append(win_sum / float(min(pos0 + t + 1, w)) - ext(n_hist + t)[:, c])
        d_groups.append(jnp.concatenate(per_t, axis=0))
    o_ref[...] = _pool_tail(x, d_groups, gate, wg_ref, sc_ref, wo_ref, g_ref, beta_ref)


def _pool_sample(x_tm, state_tm, w_in, w_grp, scale, w_out, ln_g, ln_b, ts, bsz, pos0):
    return pl.pallas_call(
        functools.partial(_pool_sample_kernel, ts=ts, bsz=bsz, pos0=pos0),
        out_shape=[jax.ShapeDtypeStruct((ts * bsz, D_MODEL), F32), jax.ShapeDtypeStruct((ts * bsz, W_POOL), F32)],
        compiler_params=pltpu.CompilerParams(vmem_limit_bytes=VMEM_LIMIT),
        name="pool_sample",
    )(x_tm, state_tm, w_in, w_grp, scale, w_out, ln_g, ln_b)


def kernel(x_prompt, x_sample, cache_kv_cmp, cache_kv_sel, state_win_kv, state_conv, state_pool, page_table,
           w_in_even, w_cmp, conv_w, conv_b, conv_ln_g, conv_ln_b, w_out_even,
           w_in_odd, w_pool_grp, pool_scale, w_out_odd, ln_g, ln_b):
    batch, seq, _ = x_prompt.shape
    bsz, ts, _ = x_sample.shape
    past = page_table.shape[1] * cache_kv_cmp.shape[2]
    kv_shape = (2, KV_HEADS, HEAD_DIM)
    vec = lambda a: a.reshape(1, -1)
    assert seq >= CONV_WIDTH - 1 and seq >= POOL_MAX - 1 and seq >= WINDOW + Q_TILE and seq % K_TILE == 0

    def token_major(x_t):
        n, _, rows = x_t.shape
        return x_t.reshape((n,) + kv_shape + (rows,)).transpose(0, 4, 1, 2, 3)[None]

    w_pad = _pad_even_weights(w_in_even[0])
    cmp_lanes = min(CMP_LANES, seq, past)
    cmp_consts = _compress_consts(w_cmp[0].reshape(CMP_BLOCK, KV_W), cmp_lanes)
    w_oe = w_out_even[0].astype(BF16)
    cw, cb, cg, cbeta = conv_w[0], vec(conv_b[0]), vec(conv_ln_g[0]), vec(conv_ln_b[0])
    g0, b0 = vec(ln_g[0]), vec(ln_b[0])

    xp = x_prompt.reshape(batch * seq, D_MODEL)
    u, sg, q, bg, gt, kvc_t, kvs_t, kvw_t, kaug, vaug, kw, vwa = _proj_prompt(xp, w_pad, batch, seq, 256)
    a_out = _conv_prompt(u, sg, cw, cb, cg, cbeta, batch, seq, 256)
    n_chunks = seq // CMP_STRIDE
    n_sel = -(-seq // SEL_BLOCK)
    kcv = _compress_prompt(kvc_t, *cmp_consts)
    b_out = _nsa_prompt(q, kaug, vaug, kw, vwa, kcv.reshape(batch, 2 * KV_HEADS, HEAD_DIM, n_chunks), gt, bg,
                        _sel_matrix(n_sel, n_chunks), batch, seq, n_chunks - 1, n_sel)
    xp1 = _mix_out(xp, a_out, b_out, None, w_oe, g0, b0, 256)

    n_win_p = min(WINDOW, seq)
    kvc_p = token_major(kvc_t)
    kvs_p = token_major(kvs_t)
    win_p = token_major(kvw_t[:, :, seq - n_win_p:])
    conv_p = u.reshape(batch, seq, W_CONV)[:, seq - (CONV_WIDTH - 1):][None]

    xs = x_sample.transpose(1, 0, 2).reshape(ts * bsz, D_MODEL)
    us, sgs, qs, bgs, gts, kvcs, kvss, kvws = _proj_sample(xs, w_pad)
    conv_ext = jnp.concatenate([state_conv[0].transpose(1, 0, 2), us.reshape(ts, bsz, W_CONV)], axis=0)
    a_out_s = _conv_sample(conv_ext, sgs.reshape(ts, bsz, W_CONV), cw, cb, cg, cbeta).reshape(ts * bsz, W_CONV)
    b_raw_s = _sample_attention(qs, kvss, kvws, gts, cache_kv_cmp[0], cache_kv_sel[0], state_win_kv[0], page_table,
                                cmp_consts, bsz, ts)
    xs1 = _mix_out(xs, a_out_s, b_raw_s, bgs, w_oe, g0, b0, ts * bsz)

    to_bt = lambda a: a.reshape(ts, bsz, -1).transpose(1, 0, 2)
    kvc_s = to_bt(kvcs).reshape((1, bsz, ts) + kv_shape)
    kvs_s = to_bt(kvss).reshape((1, bsz, ts) + kv_shape)
    win_ext = jnp.concatenate([state_win_kv[0], to_bt(kvws).reshape((bsz, ts) + kv_shape)], axis=1)
    w_len = win_ext.shape[1]
    win_s = win_ext[:, w_len - min(WINDOW, w_len):][None]
    conv_s = conv_ext[-(CONV_WIDTH - 1):].transpose(1, 0, 2)[None]

    wi_o = w_in_odd[0].astype(BF16)
    wg_o = w_pool_grp[0].astype(BF16)
    wo_o = w_out_odd[0].astype(BF16)
    sc_o = vec(pool_scale[0])
    g1, b1 = vec(ln_g[1]), vec(ln_b[1])
    n_keep = POOL_MAX - 1
    yp, tail_p = _pool_prompt(xp1, wi_o, wg_o, sc_o, wo_o, g1, b1, batch, seq, 256)
    pool_p = tail_p[:, POOL_HALO - n_keep:][None]
    st_tm = state_pool[0].transpose(1, 0, 2)
    ys, vs_new = _pool_sample(xs1, st_tm, wi_o, wg_o, sc_o, wo_o, g1, b1, ts, bsz, past)
    pool_ext = jnp.concatenate([st_tm, vs_new.reshape(ts, bsz, W_POOL)], axis=0)
    pool_s = pool_ext[-n_keep:].transpose(1, 0, 2)[None]

    y_prompt = yp.reshape(batch, seq, D_MODEL)
    y_sample = ys.reshape(ts, bsz, D_MODEL).transpose(1, 0, 2)
    return (y_prompt, y_sample, kvc_p, kvs_p, win_p, conv_p, pool_p, kvc_s, kvs_s, win_s, conv_s, pool_s)
```

```python
import functools

import jax
import jax.numpy as jnp
import numpy as np
from jax import lax
from jax.experimental import pallas as pl
from jax.experimental.pallas import tpu as pltpu

F32 = jnp.float32
BF16 = jnp.bfloat16

D_MODEL = 1024
W_CONV = 512
CONV_WIDTH = 31
N_HEADS = 8
HEAD_DIM = 64
KV_HEADS = 2
GROUP = N_HEADS // KV_HEADS
W_ATTN = N_HEADS * HEAD_DIM
KV_W = 2 * KV_HEADS * HEAD_DIM
V_OFF = KV_HEADS * HEAD_DIM
CMP_STRIDE = 16
CMP_BLOCK = 2 * CMP_STRIDE
SEL_BLOCK = 64
N_SEL = 16
WINDOW = 512
W_POOL = 1024
POOL_WINDOWS = (2, 4, 8, 16)
POOL_GROUP_W = W_POOL // len(POOL_WINDOWS)
POOL_MAX = 16
LN_EPS = 1e-5
NEG_INF = -1e30
SEL_FORCE = 1e9
DEPTH = 2
DEEPNORM_ALPHA = (2 * DEPTH) ** 0.25
EVEN_SPLITS = (W_CONV, W_CONV, W_CONV, W_ATTN, KV_W, KV_W, KV_W, 3 * N_HEADS, W_ATTN)

LANES = 128
SUBLANES = 8
E_PAD = 3 * W_CONV + 2 * W_ATTN + 3 * KV_W + KV_HEADS * LANES
VMEM_LIMIT = 56 * 1024 * 1024
Q_TILE = 256
K_TILE = 512
TILES_PER_TRIP = 4
CMP_LANES = 2048
CMP_PIECE = 256
PER_SEL = SEL_BLOCK // CMP_STRIDE
LOWEST = -3.0e38
MASKED_ROW_FLOOR = 0.5 * NEG_INF


def _cparams(sem):
    return pltpu.CompilerParams(dimension_semantics=sem, vmem_limit_bytes=VMEM_LIMIT)


def _round_up(x, m):
    return m * (-(-x // m))


def _sigmoid(x):
    return 1.0 / (1.0 + jnp.exp(-x))


def _silu(x):
    return x * _sigmoid(x)


def _layer_norm(z, g, b):
    mu = jnp.mean(z, axis=-1, keepdims=True)
    zc = z - mu
    var = jnp.mean(zc * zc, axis=-1, keepdims=True)
    return zc * lax.rsqrt(var + LN_EPS) * g + b


def _dot_t(a, b):
    return lax.dot_general(a, b, (((1,), (1,)), ((), ())), preferred_element_type=F32)


def _dot(a, b):
    return jnp.dot(a, b, preferred_element_type=F32)


def _split3(x):
    hi = x.astype(BF16)
    r1 = x - hi.astype(F32)
    mid = r1.astype(BF16)
    lo = (r1 - mid.astype(F32)).astype(BF16)
    return hi, mid, lo


KV_COL0 = 3 * W_CONV + W_ATTN


def _proj_common(xb, w_ref, u_ref, sg_ref, q_ref, bg_ref, gt_ref):
    def mm(lo, hi):
        return _dot(xb, w_ref[:, lo:hi])

    o = 0
    a_val = mm(o, o + W_CONV); o += W_CONV
    a_glu = mm(o, o + W_CONV); o += W_CONV
    u_ref[...] = a_val * _sigmoid(a_glu)
    sg_ref[...] = _silu(mm(o, o + W_CONV)); o += W_CONV
    q_ref[...] = (mm(o, o + W_ATTN) * (HEAD_DIM ** -0.5)).astype(BF16); o += W_ATTN
    o += 3 * KV_W
    bg_ref[...] = _silu(mm(o, o + W_ATTN)); o += W_ATTN
    gt_ref[...] = _sigmoid(mm(o, o + KV_HEADS * LANES))


def _proj_sample_kernel(x_ref, w_ref, u_ref, sg_ref, q_ref, bg_ref, gt_ref, kvc_ref, kvs_ref, kvw_ref):
    xb = x_ref[...].astype(BF16)
    _proj_common(xb, w_ref, u_ref, sg_ref, q_ref, bg_ref, gt_ref)
    for k, ref in enumerate((kvc_ref, kvs_ref, kvw_ref)):
        ref[...] = _dot(xb, w_ref[:, KV_COL0 + k * KV_W:KV_COL0 + (k + 1) * KV_W])


def _proj_prompt_kernel(x_ref, w_ref, wkv_t_ref, u_ref, sg_ref, q_ref, bg_ref, gt_ref, kvc_t_ref, kvs_t_ref,
                        kvw_t_ref, kaug_ref, vaug_ref, kw_ref, vwa_ref, *, tm, nt, n_sel):
    xb = x_ref[...].astype(BF16)
    _proj_common(xb, w_ref, u_ref, sg_ref, q_ref, bg_ref, gt_ref)
    kvc_t_ref[0] = _dot_t(wkv_t_ref[0:KV_W, :], xb)
    ks_t = _dot_t(wkv_t_ref[KV_W:2 * KV_W, :], xb)
    kvs_t_ref[0] = ks_t
    kw_t = _dot_t(wkv_t_ref[2 * KV_W:3 * KV_W, :], xb)
    kvw_t_ref[0] = kw_t
    key = (pl.program_id(0) % nt) * tm + lax.broadcasted_iota(jnp.int32, (n_sel, tm), 1)
    blk = lax.broadcasted_iota(jnp.int32, (n_sel, tm), 0)
    onehot = jnp.where(key // SEL_BLOCK == blk, NEG_INF, 0.0).astype(BF16)
    ones_row = jnp.where(lax.broadcasted_iota(jnp.int32, (LANES - HEAD_DIM, tm), 0) == 0, 1.0, 0.0).astype(BF16)
    n_zero = kaug_ref.shape[2] - n_sel - HEAD_DIM
    for h in range(KV_HEADS):
        k_rows = slice(h * HEAD_DIM, (h + 1) * HEAD_DIM)
        v_rows = slice(V_OFF + h * HEAD_DIM, V_OFF + (h + 1) * HEAD_DIM)
        kaug_ref[0, h, 0:n_sel, :] = onehot
        kaug_ref[0, h, n_sel:n_sel + HEAD_DIM, :] = ks_t[k_rows].astype(BF16)
        if n_zero:
            kaug_ref[0, h, n_sel + HEAD_DIM:, :] = jnp.zeros((n_zero, tm), BF16)
        vaug_ref[0, h, 0:HEAD_DIM, :] = ks_t[v_rows].astype(BF16)
        vaug_ref[0, h, HEAD_DIM:, :] = ones_row
        kw_ref[0, h] = kw_t[k_rows].astype(BF16)
        vwa_ref[0, h, 0:HEAD_DIM, :] = kw_t[v_rows].astype(BF16)
        vwa_ref[0, h, HEAD_DIM:, :] = ones_row


def _proj_row_specs(n, tm):
    row = lambda w: pl.BlockSpec((tm, w), lambda i: (i, 0))
    widths = (W_CONV, W_CONV, W_ATTN, W_ATTN, KV_HEADS * LANES)
    dtypes = (F32, F32, BF16, F32, F32)
    return row, [row(w) for w in widths], [jax.ShapeDtypeStruct((n, w), d) for w, d in zip(widths, dtypes)]


def _proj_sample(x2d, w_pad):
    n = x2d.shape[0]
    row, specs, shapes = _proj_row_specs(n, n)
    return pl.pallas_call(
        _proj_sample_kernel,
        grid=(1,),
        in_specs=[row(D_MODEL), pl.BlockSpec((D_MODEL, E_PAD), lambda i: (0, 0))],
        out_specs=specs + [row(KV_W)] * 3,
        out_shape=shapes + [jax.ShapeDtypeStruct((n, KV_W), F32)] * 3,
        compiler_params=_cparams(("arbitrary",)),
        name="proj_sample",
    )(x2d, w_pad)


def _proj_prompt(x2d, w_pad, batch, seq, tm):
    n = x2d.shape[0]
    wkv_t = w_pad[:, KV_COL0:KV_COL0 + 3 * KV_W].T
    nt = seq // tm
    n_sel = -(-seq // SEL_BLOCK)
    k_rows = _round_up(n_sel + HEAD_DIM, LANES)
    row, specs, shapes = _proj_row_specs(n, tm)
    chan = pl.BlockSpec((1, KV_W, tm), lambda i: (i // nt, 0, i % nt))
    head = lambda r: pl.BlockSpec((1, KV_HEADS, r, tm), lambda i: (i // nt, 0, 0, i % nt))
    hshape = lambda r: jax.ShapeDtypeStruct((batch, KV_HEADS, r, seq), BF16)
    return pl.pallas_call(
        functools.partial(_proj_prompt_kernel, tm=tm, nt=nt, n_sel=n_sel),
        grid=(n // tm,),
        in_specs=[row(D_MODEL), pl.BlockSpec((D_MODEL, E_PAD), lambda i: (0, 0)),
                  pl.BlockSpec((3 * KV_W, D_MODEL), lambda i: (0, 0))],
        out_specs=specs + [chan] * 3 + [head(k_rows), head(LANES), head(HEAD_DIM), head(LANES)],
        out_shape=shapes + [jax.ShapeDtypeStruct((batch, KV_W, seq), F32)] * 3 + [
            hshape(k_rows), hshape(LANES), hshape(HEAD_DIM), hshape(LANES)],
        compiler_params=_cparams(("parallel",)),
        name="proj_prompt",
    )(x2d, w_pad, wkv_t)


def _pad_even_weights(w):
    offs = np.cumsum(EVEN_SPLITS)[:-1].tolist()
    a_val, a_glu, a_gate, wq, wkc, wks, wkw, wg, wbg = jnp.split(w, offs, axis=1)
    wg = wg.reshape(D_MODEL, KV_HEADS, GROUP, 3).transpose(0, 1, 3, 2).reshape(D_MODEL, KV_HEADS, 3 * GROUP)
    wg = jnp.pad(wg, ((0, 0), (0, 0), (0, LANES - 3 * GROUP))).reshape(D_MODEL, KV_HEADS * LANES)
    return jnp.concatenate([a_val, a_glu, a_gate, wq, wkc, wks, wkw, wbg, wg], axis=1).astype(BF16)


CONV_HALO = 32
CONV_CHUNK = 32


def _conv_prompt_kernel(u_ref, sg_ref, w_ref, cb_ref, g_ref, b_ref, o_ref, ext_ref, sh_ref, *, tt):
    @pl.when(pl.program_id(1) == 0)
    def _():
        ext_ref[0:CONV_HALO, :] = jnp.zeros((CONV_HALO, W_CONV), F32)

    ext_ref[CONV_HALO:CONV_HALO + tt, :] = u_ref[...]
    base = CONV_HALO - (CONV_WIDTH - 1)
    n_sh = sh_ref.shape[1]
    for s in range(1, SUBLANES):
        sh_ref[s - 1] = ext_ref[s:s + n_sh, :]

    def window(start):
        s, off = start % SUBLANES, start - start % SUBLANES
        return ext_ref[off:off + CONV_CHUNK, :] if s == 0 else sh_ref[s - 1, off:off + CONV_CHUNK, :]

    for c in range(tt // CONV_CHUNK):
        r0 = c * CONV_CHUNK
        acc = jnp.zeros((CONV_CHUNK, W_CONV), F32) + cb_ref[...]
        for k in range(CONV_WIDTH):
            acc = acc + w_ref[k:k + 1, :] * window(base + r0 + k)
        y = _layer_norm(acc, g_ref[...], b_ref[...])
        o_ref[r0:r0 + CONV_CHUNK, :] = (_silu(y) * sg_ref[r0:r0 + CONV_CHUNK, :]).astype(o_ref.dtype)
    ext_ref[0:CONV_HALO, :] = ext_ref[tt:tt + CONV_HALO, :]


def _conv_prompt(u2d, sg2d, conv_w, conv_b, ln_g, ln_b, batch, seq, tt):
    nt = seq // tt
    row = pl.BlockSpec((tt, W_CONV), lambda b, i: (b * nt + i, 0))
    vec = pl.BlockSpec((1, W_CONV), lambda b, i: (0, 0))
    return pl.pallas_call(
        functools.partial(_conv_prompt_kernel, tt=tt),
        grid=(batch, nt),
        in_specs=[row, row, pl.BlockSpec((CONV_WIDTH, W_CONV), lambda b, i: (0, 0)), vec, vec, vec],
        out_specs=row,
        out_shape=jax.ShapeDtypeStruct((batch * seq, W_CONV), BF16),
        scratch_shapes=[pltpu.VMEM((tt + CONV_HALO, W_CONV), F32),
                        pltpu.VMEM((SUBLANES - 1, tt + CONV_HALO - SUBLANES, W_CONV), F32)],
        compiler_params=_cparams(("arbitrary", "arbitrary")),
        name="conv_prompt",
    )(u2d, sg2d, conv_w, conv_b, ln_g, ln_b)


def _conv_sample_kernel(ext_ref, sg_ref, w_ref, cb_ref, g_ref, b_ref, o_ref, *, ts):
    for t in range(ts):
        acc = jnp.zeros(ext_ref.shape[1:], F32) + cb_ref[...]
        for k in range(CONV_WIDTH):
            acc = acc + w_ref[k:k + 1, :] * ext_ref[t + k]
        y = _layer_norm(acc, g_ref[...], b_ref[...])
        o_ref[t] = _silu(y) * sg_ref[t]


def _conv_sample(ext, sg, conv_w, conv_b, ln_g, ln_b):
    ts, bsz, _ = sg.shape
    return pl.pallas_call(
        functools.partial(_conv_sample_kernel, ts=ts),
        out_shape=jax.ShapeDtypeStruct((ts, bsz, W_CONV), F32),
        name="conv_sample",
    )(ext, sg, conv_w, conv_b, ln_g, ln_b)


def _compress_span(x_ext, w1_ref, w2_ref, s_ref):
    lanes = x_ext.shape[1] - LANES
    piece = min(lanes, CMP_PIECE)
    reps = piece // LANES
    out = None
    for c in range(lanes // piece):
        xp = x_ext[:, c * piece:(c + 1) * piece + LANES]
        later = pltpu.roll(xp, piece + LANES - CMP_STRIDE, 1)[:, 0:piece]
        z = xp[:, 0:piece] * jnp.tile(w1_ref[...], (1, reps)) + later * jnp.tile(w2_ref[...], (1, reps))
        hi = z.astype(BF16)
        lo = (z - hi.astype(F32)).astype(BF16)
        seg = s_ref[c * piece:(c + 1) * piece, :]
        part = _dot(hi, seg) + _dot(lo, seg)
        out = part if out is None else out + part
    return out


def _compress_prompt_kernel(x_ref, w1_ref, w2_ref, s_ref, o_ref, *, lanes):
    seq = x_ref.shape[2]
    parts = []
    for c in range(seq // lanes):
        if (c + 1) * lanes + LANES <= seq:
            x_ext = x_ref[0, :, c * lanes:(c + 1) * lanes + LANES]
        else:
            x_ext = jnp.concatenate([x_ref[0, :, c * lanes:(c + 1) * lanes], jnp.zeros((KV_W, LANES), F32)], axis=1)
        parts.append(_compress_span(x_ext, w1_ref, w2_ref, s_ref))
    o_ref[0] = jnp.concatenate(parts, axis=1)


def _compress_prompt(kvc_t, w1t, w2t, seg):
    batch, _, seq = kvc_t.shape
    lanes = seg.shape[0]
    const = lambda a: pl.BlockSpec(a.shape, lambda b: (0, 0))
    return pl.pallas_call(
        functools.partial(_compress_prompt_kernel, lanes=lanes),
        grid=(batch,),
        in_specs=[pl.BlockSpec((1, KV_W, seq), lambda b: (b, 0, 0)), const(w1t), const(w2t), const(seg)],
        out_specs=pl.BlockSpec((1, KV_W, seq // CMP_STRIDE), lambda b: (b, 0, 0)),
        out_shape=jax.ShapeDtypeStruct((batch, KV_W, seq // CMP_STRIDE), F32),
        compiler_params=_cparams(("parallel",)),
        name="compress_prompt",
    )(kvc_t, w1t, w2t, seg)


def _compress_paged_kernel(pt_ref, *refs, n_pages_step):
    del pt_ref
    pages = refs[:n_pages_step + 1]
    w1_ref, w2_ref, s_ref, o_ref = refs[n_pages_step + 1:]
    x_ext = jnp.concatenate([p[0] for p in pages[:-1]] + [pages[-1][0, :, 0:LANES]], axis=1)
    o_ref[0] = _compress_span(x_ext, w1_ref, w2_ref, s_ref)


def _compress_paged(cache_t, page_table, w1t, w2t, seg):
    _, _, page = cache_t.shape
    bsz, n_pages = page_table.shape
    lanes = seg.shape[0]
    pps = lanes // page
    assert page >= LANES and lanes % page == 0 and n_pages % pps == 0
    n_steps = n_pages // pps
    n_out = lanes // CMP_STRIDE

    pt_ext = jnp.concatenate([page_table, page_table[:, -1:]], axis=1).reshape(-1)

    def page_spec(k):
        return pl.BlockSpec((1, KV_W, page), lambda b, s, pt: (pt[b * (n_pages + 1) + s * pps + k], 0, 0))

    const = lambda a: pl.BlockSpec(a.shape, lambda b, s, pt: (0, 0))
    grid_spec = pltpu.PrefetchScalarGridSpec(
        num_scalar_prefetch=1,
        grid=(bsz, n_steps),
        in_specs=[page_spec(k) for k in range(pps + 1)] + [const(w1t), const(w2t), const(seg)],
        out_specs=pl.BlockSpec((1, KV_W, n_out), lambda b, s, pt: (b, 0, s)),
    )
    return pl.pallas_call(
        functools.partial(_compress_paged_kernel, n_pages_step=pps),
        grid_spec=grid_spec,
        out_shape=jax.ShapeDtypeStruct((bsz, KV_W, n_steps * n_out), F32),
        compiler_params=_cparams(("parallel", "arbitrary")),
        name="compress_paged",
    )(pt_ext, *([cache_t] * (pps + 1)), w1t, w2t, seg)


def _compress_consts(w_cmp2d, lanes):
    w1t = jnp.tile(w_cmp2d[:CMP_STRIDE].T, (1, LANES // CMP_STRIDE))
    w2t = jnp.tile(w_cmp2d[CMP_STRIDE:].T, (1, LANES // CMP_STRIDE))
    seg = jnp.asarray(np.arange(lanes)[:, None] // CMP_STRIDE == np.arange(lanes // CMP_STRIDE)[None, :], BF16)
    return w1t, w2t, seg


def _pick_top_block(sc, sel, blk):
    mx = jnp.max(sc, axis=0, keepdims=True)
    jm = jnp.min(jnp.where(sc == mx, blk, sc.shape[0]), axis=0, keepdims=True)
    pick = blk == jm
    return jnp.where(pick, LOWEST, sc), jnp.where(pick, 1.0, sel)


def _force_scores(score, blk, cur):
    forced = (blk == 0) | ((blk >= cur - 1) & (blk <= cur))
    return jnp.where(forced, SEL_FORCE, jnp.where(blk <= cur, score, -SEL_FORCE))


def _sel_matrix(n_sel_blocks, n_cols):
    n = np.arange(n_cols)
    m = (n[None, :] // PER_SEL == np.arange(n_sel_blocks)[:, None]) & (n[None, :] % PER_SEL < PER_SEL - 1)
    return jnp.asarray(m, BF16)


def _cmp_valid(n, qpos, n_cmp):
    return n * CMP_STRIDE + (CMP_BLOCK - 1) <= jnp.minimum(qpos, (n_cmp - 1) * CMP_STRIDE + CMP_BLOCK - 1)


def _nsa_prompt_kernel(q_ref, kaug_ref, vaug_ref, kw_ref, vwa_ref, kc_ref, vc_ref, gt_ref, bg_ref, m01_ref, o_ref,
                       qa_ref, acc_ref, mx_ref, s_ref, *, n_cmp, n_sel):
    i = pl.program_id(2)
    rows = GROUP * Q_TILE
    q = q_ref[...]
    qs = jnp.concatenate([q[:, g * HEAD_DIM:(g + 1) * HEAD_DIM] for g in range(GROUP)], axis=0)
    n_cols = kc_ref.shape[3]

    def row_pos(width):
        return i * Q_TILE + (lax.broadcasted_iota(jnp.int32, (rows, width), 0) & (Q_TILE - 1))

    tpos = i * Q_TILE + lax.broadcasted_iota(jnp.int32, (Q_TILE, n_cols), 0)
    c_bias = jnp.where(_cmp_valid(lax.broadcasted_iota(jnp.int32, (Q_TILE, n_cols), 1), tpos, n_cmp), 0.0, NEG_INF)
    s_all = _dot(qs, kc_ref[0, 0].astype(BF16))
    p_parts = []
    p_grp = None
    for g in range(GROUP):
        s = s_all[g * Q_TILE:(g + 1) * Q_TILE] + c_bias
        e = jnp.exp(s - jnp.maximum(jnp.max(s, axis=1, keepdims=True), MASKED_ROW_FLOOR))
        l = jnp.sum(e, axis=1, keepdims=True)
        p = e * (1.0 / jnp.where(l > 0.0, l, 1.0))
        p_parts.append(p.astype(BF16))
        p_grp = p if p_grp is None else p_grp + p
    o_c = _dot_t(jnp.concatenate(p_parts, axis=0), vc_ref[0, 0].astype(BF16))

    score_t = sum(_dot_t(m01_ref[...], part) for part in _split3(p_grp))
    blk = lax.broadcasted_iota(jnp.int32, (n_sel, Q_TILE), 0)
    tok = i * Q_TILE + lax.broadcasted_iota(jnp.int32, (n_sel, Q_TILE), 1)
    sc = _force_scores(score_t, blk, tok // SEL_BLOCK)
    sel_t = jnp.zeros(sc.shape, F32)
    n_pick = min(N_SEL, n_sel)

    w_keys = WINDOW + Q_TILE
    w0 = pl.multiple_of(jnp.maximum(i * Q_TILE - WINDOW, 0), Q_TILE)
    rel = (i * Q_TILE + lax.broadcasted_iota(jnp.int32, (Q_TILE, w_keys), 0)
           - (w0 + lax.broadcasted_iota(jnp.int32, (Q_TILE, w_keys), 1)))
    w_bias = jnp.where((rel >= 0) & (rel < WINDOW), 0.0, NEG_INF)
    s_w_all = _dot(qs, kw_ref[0, 0, :, pl.ds(w0, w_keys)])
    e_w_parts = []
    done = 0
    for g in range(GROUP):
        s_w = s_w_all[g * Q_TILE:(g + 1) * Q_TILE] + w_bias
        e_w_parts.append(jnp.exp(s_w - jnp.max(s_w, axis=1, keepdims=True)).astype(BF16))
        upto = n_pick * (g + 1) // GROUP
        for _ in range(done, upto):
            sc, sel_t = _pick_top_block(sc, sel_t, blk)
        done = upto
    acc_w = _dot_t(jnp.concatenate(e_w_parts, axis=0), vwa_ref[0, 0, :, pl.ds(w0, w_keys)])
    o_w = acc_w[:, 0:HEAD_DIM] / acc_w[:, HEAD_DIM:HEAD_DIM + 1]
    not_sel = (1.0 - sel_t).T.astype(BF16)
    for g in range(GROUP):
        qa_ref[g * Q_TILE:(g + 1) * Q_TILE, 0:n_sel] = not_sel
    qa_ref[:, n_sel:n_sel + HEAD_DIM] = qs
    if qa_ref.shape[1] > n_sel + HEAD_DIM:
        qa_ref[:, n_sel + HEAD_DIM:] = jnp.zeros((rows, qa_ref.shape[1] - n_sel - HEAD_DIM), BF16)

    mx_ref[...] = jnp.full(mx_ref.shape, NEG_INF, F32)

    def score_tiles(j, n_tiles, causal_last):
        for t in range(n_tiles):
            k0 = pl.multiple_of((j + t) * K_TILE, K_TILE)
            s = _dot(qa_ref[...], kaug_ref[0, 0, :, pl.ds(k0, K_TILE)])
            if causal_last and t == n_tiles - 1:
                kpos = k0 + lax.broadcasted_iota(jnp.int32, (rows, K_TILE), 1)
                s = jnp.where(kpos <= row_pos(K_TILE), s, NEG_INF)
            s_ref[j + t] = s
            part = s[:, 0:LANES]
            for c in range(1, K_TILE // LANES):
                part = jnp.maximum(part, s[:, c * LANES:(c + 1) * LANES])
            mx_ref[...] = jnp.maximum(mx_ref[...], part)

    def acc_tiles(j, n_tiles):
        m_b = jnp.tile(mx_ref[...], (1, K_TILE // LANES))
        for t in range(n_tiles):
            k0 = pl.multiple_of((j + t) * K_TILE, K_TILE)
            pe = jnp.exp(s_ref[j + t] - m_b).astype(BF16)
            acc_ref[...] += _dot_t(pe, vaug_ref[0, 0, :, pl.ds(k0, K_TILE)])

    def grouped(n, fn, tail_fn, tail_min):
        def body(jj, carry):
            fn(TILES_PER_TRIP * jj, TILES_PER_TRIP)
            return carry
        lax.fori_loop(0, n // TILES_PER_TRIP, body, 0)
        for r in range(tail_min, TILES_PER_TRIP):
            @pl.when(n % TILES_PER_TRIP == r)
            def _(r=r):
                tail_fn(n - r, r)

    n_full = (i * Q_TILE) // K_TILE
    grouped(n_full, lambda j, n: score_tiles(j, n, False), lambda j, r: score_tiles(j, r + 1, True), 0)

    mx_ref[...] = jnp.broadcast_to(jnp.max(mx_ref[...], axis=1, keepdims=True), mx_ref.shape)
    acc_ref[...] = jnp.zeros(acc_ref.shape, F32)
    grouped(n_full + 1, acc_tiles, acc_tiles, 1)

    acc = acc_ref[...]
    o_s = acc[:, 0:HEAD_DIM] / acc[:, HEAD_DIM:HEAD_DIM + 1]

    gt = gt_ref[...]
    outs = []
    for g in range(GROUP):
        r = slice(g * Q_TILE, (g + 1) * Q_TILE)
        outs.append(gt[:, g:g + 1] * o_c[r] + gt[:, GROUP + g:GROUP + g + 1] * o_s[r]
                    + gt[:, 2 * GROUP + g:2 * GROUP + g + 1] * o_w[r])
    o_ref[...] = (jnp.concatenate(outs, axis=1) * bg_ref[...]).astype(o_ref.dtype)


def _nsa_prompt(q2d, kaug, vaug, kw, vwa, kcv4, gt2d, bg2d, m01, batch, seq, n_cmp, n_sel):
    nq = seq // Q_TILE
    rows = GROUP * Q_TILE
    n_cols = kcv4.shape[-1]
    per_head = lambda a: pl.BlockSpec((1, 1) + a.shape[2:], lambda b, h, i: (b, h, 0, 0),
                                      pipeline_mode=pl.Buffered(1))
    return pl.pallas_call(
        functools.partial(_nsa_prompt_kernel, n_cmp=n_cmp, n_sel=n_sel),
        grid=(batch, KV_HEADS, nq),
        in_specs=[pl.BlockSpec((Q_TILE, GROUP * HEAD_DIM), lambda b, h, i: (b * nq + i, h)),
                  per_head(kaug), per_head(vaug), per_head(kw), per_head(vwa),
                  pl.BlockSpec((1, 1, HEAD_DIM, n_cols), lambda b, h, i: (b, h, 0, 0)),
                  pl.BlockSpec((1, 1, HEAD_DIM, n_cols), lambda b, h, i: (b, KV_HEADS + h, 0, 0)),
                  pl.BlockSpec((Q_TILE, LANES), lambda b, h, i: (b * nq + i, h)),
                  pl.BlockSpec((Q_TILE, GROUP * HEAD_DIM), lambda b, h, i: (b * nq + i, h)),
                  pl.BlockSpec(m01.shape, lambda b, h, i: (0, 0))],
        out_specs=pl.BlockSpec((Q_TILE, GROUP * HEAD_DIM), lambda b, h, i: (b * nq + i, h)),
        out_shape=jax.ShapeDtypeStruct((batch * seq, W_ATTN), BF16),
        scratch_shapes=[pltpu.VMEM((rows, kaug.shape[2]), BF16), pltpu.VMEM((rows, LANES), F32),
                        pltpu.VMEM((rows, LANES), F32), pltpu.VMEM((seq // K_TILE, rows, K_TILE), F32)],
        compiler_params=_cparams(("parallel", "parallel", "arbitrary")),
        name="nsa_prompt",
    )(q2d, kaug, vaug, kw, vwa, kcv4, kcv4, gt2d, bg2d, m01)


def _new_token_scores(qf, nk, tq, ts):
    cols = []
    for t in range(ts):
        s = jnp.sum(qf * nk[t:t + 1, :], axis=1, keepdims=True)
        cols.append(jnp.where(tq >= t, s, NEG_INF))
    return cols


def _bf16_round(x):
    return x.astype(BF16).astype(F32)


def _nsa_sample_a_kernel(q_ref, kcv_ref, win_ref, new_ref, gt_ref, m01_ref, part_ref, idx_ref, *,
                         past, ts, n_cmp):
    for bi in range(q_ref.shape[0]):
        _nsa_sample_a_one(bi, q_ref, kcv_ref, win_ref, new_ref, gt_ref, m01_ref, part_ref, idx_ref,
                          past=past, ts=ts, n_cmp=n_cmp)


def _nsa_sample_a_one(bi, q_ref, kcv_ref, win_ref, new_ref, gt_ref, m01_ref, part_ref, idx_ref, *,
                      past, ts, n_cmp):
    rows = GROUP * ts
    kcv = kcv_ref[bi]
    win = win_ref[bi]
    new = new_ref[bi]
    n_cols = kcv.shape[1]
    w_hist = win.shape[1]
    n_blk_pad = m01_ref.shape[0]
    tq = lax.broadcasted_iota(jnp.int32, (rows, 1), 0) % ts
    qpos = past + tq
    scores = []
    for h in range(KV_HEADS):
        qh = q_ref[bi, h]
        ksl = slice(h * HEAD_DIM, (h + 1) * HEAD_DIM)
        vsl = slice(V_OFF + h * HEAD_DIM, V_OFF + (h + 1) * HEAD_DIM)
        s = _dot(qh, kcv[ksl].astype(BF16))
        valid = _cmp_valid(lax.broadcasted_iota(jnp.int32, (rows, n_cols), 1), qpos, n_cmp)
        s = jnp.where(valid, s, NEG_INF)
        e = jnp.where(valid, jnp.exp(s - jnp.max(s, axis=1, keepdims=True)), 0.0)
        l = jnp.sum(e, axis=1, keepdims=True)
        p = e / jnp.where(l > 0.0, l, 1.0)
        o_c = _dot_t(p.astype(BF16), kcv[vsl].astype(BF16))
        p_grp = p[0:ts]
        for g in range(1, GROUP):
            p_grp = p_grp + p[g * ts:(g + 1) * ts]
        scores.append(sum(_dot_t(part, m01_ref[...]) for part in _split3(p_grp)))

        s1 = _dot(qh, win[ksl].astype(BF16))
        rel = qpos - (past - w_hist + lax.broadcasted_iota(jnp.int32, (rows, w_hist), 1))
        valid = (rel >= 0) & (rel <= jnp.minimum(qpos, WINDOW - 1))
        s1 = jnp.where(valid, s1, NEG_INF)
        s2 = _new_token_scores(qh.astype(F32), _bf16_round(new[:, ksl]), tq, ts)
        m = jnp.max(s1, axis=1, keepdims=True)
        for c in s2:
            m = jnp.maximum(m, c)
        e1 = jnp.where(valid, jnp.exp(s1 - m), 0.0)
        l = jnp.sum(e1, axis=1, keepdims=True)
        o_w = _dot_t(e1.astype(BF16), win[vsl].astype(BF16))
        nv = _bf16_round(new[:, vsl])
        for t in range(ts):
            e2 = jnp.exp(s2[t] - m)
            l = l + e2
            o_w = o_w + _bf16_round(e2) * nv[t:t + 1, :]
        gt = gt_ref[bi, h]
        part_ref[bi, h] = gt[:, 0:1] * o_c + gt[:, 2:3] * (o_w / l)

    sc = jnp.concatenate(scores, axis=0)
    nr = KV_HEADS * ts
    blk = lax.broadcasted_iota(jnp.int32, (nr, n_blk_pad), 1)
    cur = (past + (lax.broadcasted_iota(jnp.int32, (nr, 1), 0) % ts)) // SEL_BLOCK
    sc = _force_scores(sc, blk, cur)
    lane = lax.broadcasted_iota(jnp.int32, (nr, LANES), 1)
    idx = jnp.zeros((nr, LANES), jnp.int32)
    for k in range(N_SEL):
        mx = jnp.max(sc, axis=1, keepdims=True)
        jm = jnp.min(jnp.where(sc == mx, blk, n_blk_pad), axis=1, keepdims=True)
        idx = jnp.where(lane == k, jm, idx)
        sc = jnp.where(blk == jm, LOWEST, sc)
    idx_ref[bi] = idx


def _nsa_sample_a(q4, kcv_t, win_t, new_w, gt4, m01, past, ts, n_cmp):
    bsz = q4.shape[0]
    rows = GROUP * ts
    nb = max(d for d in (4, 2, 1) if bsz % d == 0)
    per_b = lambda a: pl.BlockSpec((nb,) + a.shape[1:], lambda b: (b,) + (0,) * (a.ndim - 1))
    return pl.pallas_call(
        functools.partial(_nsa_sample_a_kernel, past=past, ts=ts, n_cmp=n_cmp),
        grid=(bsz // nb,),
        in_specs=[per_b(q4), per_b(kcv_t), per_b(win_t), per_b(new_w), per_b(gt4),
                  pl.BlockSpec(m01.shape, lambda b: (0, 0))],
        out_specs=[pl.BlockSpec((nb, KV_HEADS, rows, HEAD_DIM), lambda b: (b, 0, 0, 0)),
                   pl.BlockSpec((nb, KV_HEADS * ts, LANES), lambda b: (b, 0, 0))],
        out_shape=[jax.ShapeDtypeStruct((bsz, KV_HEADS, rows, HEAD_DIM), F32),
                   jax.ShapeDtypeStruct((bsz, KV_HEADS * ts, LANES), jnp.int32)],
        compiler_params=_cparams(("parallel",)),
        name="nsa_sample_scores",
    )(q4, kcv_t, win_t, new_w, gt4, m01)


def _nsa_sample_b_kernel(idx_ref, pt_ref, *refs, past, ts, n_past_blocks, page):
    del pt_ref
    kv_blocks = refs[:N_SEL]
    q_ref, new_ref, gt_ref, part_ref, o_ref = refs[N_SEL:]
    r = pl.program_id(0)
    t = r % ts
    head0 = ((r // ts) % KV_HEADS) == 0
    bpp = page // SEL_BLOCK
    q = q_ref[0]
    lane = lax.broadcasted_iota(jnp.int32, (GROUP, page), 1)

    def pick(x, off):
        return jnp.where(head0, x[:, off:off + HEAD_DIM], x[:, off + HEAD_DIM:off + 2 * HEAD_DIM])

    s_all = _dot(q, jnp.concatenate([kv_blocks[k][0, 0, 0].astype(BF16) for k in range(N_SEL)], axis=1))
    s_list = []
    has_new = False
    for k in range(N_SEL):
        j = idx_ref[r * N_SEL + k]
        is_past = j < n_past_blocks
        has_new = jnp.logical_or(has_new, jnp.logical_not(is_past))
        off = (j % bpp) * SEL_BLOCK
        page_start = (j // bpp) * page
        lo = jnp.where(is_past, off, page)
        hi = jnp.minimum(off + SEL_BLOCK, past + t + 1 - page_start)
        s_list.append(jnp.where((lane >= lo) & (lane < hi), s_all[:, k * page:(k + 1) * page], NEG_INF))
    new = new_ref[0]
    tq = jnp.where(has_new, t, -1) + jnp.zeros((GROUP, 1), jnp.int32)
    s_new = _new_token_scores(q.astype(F32), _bf16_round(pick(new, 0)), tq, ts)
    m = s_new[0]
    for c in s_new[1:]:
        m = jnp.maximum(m, c)
    for s in s_list:
        m = jnp.maximum(m, jnp.max(s, axis=1, keepdims=True))
    nv = _bf16_round(pick(new, V_OFF))
    l = jnp.zeros((GROUP, 1), F32)
    o = jnp.zeros((GROUP, HEAD_DIM), F32)
    for tt in range(ts):
        e2 = jnp.exp(s_new[tt] - m)
        l = l + e2
        o = o + _bf16_round(e2) * nv[tt:tt + 1, :]
    e_all = jnp.exp(jnp.concatenate(s_list, axis=1) - m)
    l = l + jnp.sum(e_all, axis=1, keepdims=True)
    o = o + _dot_t(e_all.astype(BF16),
                   jnp.concatenate([kv_blocks[k][0, 1, 0].astype(BF16) for k in range(N_SEL)], axis=1))
    gt = gt_ref[0]
    o_ref[0] = part_ref[0] + gt[:, 1:2] * (o / l)


def _nsa_sample_b(idx_flat, page_flat, cache_slabs, q_rows, new_rows, gt_rows, part_rows, past, ts, page):
    n_rows = q_rows.shape[0]
    n_past_blocks = past // SEL_BLOCK

    def slab_spec(k):
        return pl.BlockSpec((1, 2, 1, HEAD_DIM, page),
                            lambda r, idx, pg: (pg[r * N_SEL + k], 0, (r // ts) % KV_HEADS, 0, 0))

    row3 = lambda a: pl.BlockSpec((1,) + a.shape[1:], lambda r, idx, pt: (r, 0, 0))
    grid_spec = pltpu.PrefetchScalarGridSpec(
        num_scalar_prefetch=2,
        grid=(n_rows,),
        in_specs=[slab_spec(k) for k in range(N_SEL)] + [
            row3(q_rows),
            pl.BlockSpec((1,) + new_rows.shape[1:], lambda r, idx, pt: (r // (KV_HEADS * ts), 0, 0)),
            row3(gt_rows), row3(part_rows)],
        out_specs=row3(part_rows),
    )
    return pl.pallas_call(
        functools.partial(_nsa_sample_b_kernel, past=past, ts=ts, n_past_blocks=n_past_blocks, page=page),
        grid_spec=grid_spec,
        out_shape=jax.ShapeDtypeStruct(part_rows.shape, F32),
        compiler_params=_cparams(("arbitrary",)),
        name="nsa_sample_select",
    )(idx_flat, page_flat, *([cache_slabs] * N_SEL), q_rows, new_rows, gt_rows, part_rows)


def _channel_major(x):
    n, rows = x.shape[:2]
    return x.transpose(0, 2, 3, 4, 1).reshape(n, KV_W, rows)


def _sample_attention(q_tm, kvs_tm, kvw_tm, gt_tm, cache_c, cache_s, state_win, page_table, cmp_consts, bsz, ts):
    n_phys, page = cache_c.shape[0], cache_c.shape[1]
    n_pages = page_table.shape[1]
    past = n_pages * page
    w_hist = state_win.shape[1]
    assert ts < CMP_STRIDE and page % SEL_BLOCK == 0 and ts <= WINDOW and w_hist <= past
    total = past + ts
    n_cmp = total // CMP_STRIDE - 1
    n_sel_blocks = -(-total // SEL_BLOCK)
    kcv_t = _compress_paged(_channel_major(cache_c), page_table, *cmp_consts)
    m01 = _sel_matrix(_round_up(n_sel_blocks, LANES), kcv_t.shape[2])
    q4 = q_tm.reshape(ts, bsz, KV_HEADS, GROUP, HEAD_DIM).transpose(1, 2, 3, 0, 4).reshape(
        bsz, KV_HEADS, GROUP * ts, HEAD_DIM)
    g5 = gt_tm.reshape(ts, bsz, KV_HEADS, LANES)[..., :3 * GROUP].reshape(ts, bsz, KV_HEADS, 3, GROUP)
    g5 = g5.transpose(1, 2, 4, 0, 3).reshape(bsz, KV_HEADS, GROUP * ts, 3)
    gt4 = jnp.pad(g5, ((0, 0), (0, 0), (0, 0), (0, LANES - 3)))
    to_bt = lambda a: a.reshape(ts, bsz, KV_W).transpose(1, 0, 2)
    part, idx = _nsa_sample_a(q4, kcv_t, _channel_major(state_win), to_bt(kvw_tm), gt4, m01, past, ts, n_cmp)
    to_rows = lambda a: a.reshape(bsz, KV_HEADS, GROUP, ts, a.shape[-1]).transpose(0, 1, 3, 2, 4).reshape(
        bsz * KV_HEADS * ts, GROUP, a.shape[-1])
    slabs = _channel_major(cache_s).reshape(n_phys, 2, KV_HEADS, HEAD_DIM, page)
    idx16 = idx[:, :, :N_SEL]
    pages = jnp.take_along_axis(page_table[:, None, :],
                                jnp.minimum(idx16, past // SEL_BLOCK - 1) // (page // SEL_BLOCK), axis=2)
    o_rows = _nsa_sample_b(idx16.reshape(-1), pages.reshape(-1), slabs,
                           to_rows(q4), to_bt(kvs_tm), to_rows(gt4), to_rows(part), past, ts, page)
    return o_rows.reshape(bsz, KV_HEADS, ts, GROUP, HEAD_DIM).transpose(2, 0, 1, 3, 4).reshape(ts * bsz, W_ATTN)


def _mix_out_kernel(x_ref, a_ref, b_ref, *refs, gated):
    if gated:
        bg_ref, w_ref, g_ref, beta_ref, o_ref = refs
        b = (b_ref[...] * bg_ref[...]).astype(BF16)
    else:
        w_ref, g_ref, beta_ref, o_ref = refs
        b = b_ref[...]
    d = _dot(a_ref[...].astype(BF16), w_ref[0:W_CONV, :]) + _dot(b, w_ref[W_CONV:W_CONV + W_ATTN, :])
    o_ref[...] = _layer_norm(DEEPNORM_ALPHA * x_ref[...] + d, g_ref[...], beta_ref[...])


def _mix_out(x2d, a2d, b2d, bg2d, w_out, ln_g, ln_b, tm):
    n = x2d.shape[0]
    row = lambda w: pl.BlockSpec((tm, w), lambda i: (i, 0))
    vec = pl.BlockSpec((1, D_MODEL), lambda i: (0, 0))
    gate = [] if bg2d is None else [bg2d]
    return pl.pallas_call(
        functools.partial(_mix_out_kernel, gated=bg2d is not None),
        grid=(n // tm,),
        in_specs=[row(D_MODEL), row(W_CONV), row(W_ATTN)] + [row(W_ATTN)] * len(gate) + [
            pl.BlockSpec((W_CONV + W_ATTN, D_MODEL), lambda i: (0, 0)), vec, vec],
        out_specs=row(D_MODEL),
        out_shape=jax.ShapeDtypeStruct((n, D_MODEL), F32),
        compiler_params=_cparams(("parallel",)),
        name="mix_out",
    )(x2d, a2d, b2d, *gate, w_out, ln_g, ln_b)


POOL_HALO = 16


def _pool_tail(x, d_groups, gate, wg_ref, sc_ref, wo_ref, g_ref, beta_ref):
    mixed = jnp.concatenate([_dot(d_groups[gi].astype(BF16), wg_ref[gi]) for gi in range(len(POOL_WINDOWS))], axis=1)
    h = (mixed * sc_ref[...] * _silu(gate)).astype(BF16)
    return _layer_norm(DEEPNORM_ALPHA * x + _dot(h, wo_ref[...]), g_ref[...], beta_ref[...])


def _pool_prompt_kernel(x_ref, wi_ref, wg_ref, sc_ref, wo_ref, g_ref, beta_ref, o_ref, tail_ref, ext_ref, *, tm):
    i = pl.program_id(1)

    @pl.when(i == 0)
    def _():
        ext_ref[0:POOL_HALO, :] = jnp.zeros((POOL_HALO, W_POOL), F32)

    x = x_ref[...]
    xb = x.astype(BF16)
    v = _dot(xb, wi_ref[:, 0:W_POOL])
    gate = _dot(xb, wi_ref[:, W_POOL:2 * W_POOL])
    ext_ref[POOL_HALO:POOL_HALO + tm, :] = v
    pos = i * tm + lax.broadcasted_iota(jnp.int32, (tm, 1), 0)
    d_groups = []
    for gi, w in enumerate(POOL_WINDOWS):
        c = slice(gi * POOL_GROUP_W, (gi + 1) * POOL_GROUP_W)
        win_sum = v[:, c]
        for k in range(1, w):
            win_sum = win_sum + ext_ref[POOL_HALO - k:POOL_HALO - k + tm, c]
        cnt = jnp.minimum(pos + 1, w).astype(F32)
        d_groups.append(win_sum / cnt - v[:, c])
    o_ref[...] = _pool_tail(x, d_groups, gate, wg_ref, sc_ref, wo_ref, g_ref, beta_ref)
    ext_ref[0:POOL_HALO, :] = ext_ref[tm:tm + POOL_HALO, :]
    tail_ref[0] = ext_ref[0:POOL_HALO, :]


def _pool_prompt(x2d, w_in, w_grp, scale, w_out, ln_g, ln_b, batch, seq, tm):
    nt = seq // tm
    const = lambda a: pl.BlockSpec(a.shape, lambda b, i: (0,) * a.ndim)
    row = pl.BlockSpec((tm, D_MODEL), lambda b, i: (b * nt + i, 0))
    return pl.pallas_call(
        functools.partial(_pool_prompt_kernel, tm=tm),
        grid=(batch, nt),
        in_specs=[row, const(w_in), const(w_grp), const(scale), const(w_out), const(ln_g), const(ln_b)],
        out_specs=[row, pl.BlockSpec((1, POOL_HALO, W_POOL), lambda b, i: (b, 0, 0))],
        out_shape=[jax.ShapeDtypeStruct((batch * seq, D_MODEL), F32),
                   jax.ShapeDtypeStruct((batch, POOL_HALO, W_POOL), F32)],
        scratch_shapes=[pltpu.VMEM((tm + POOL_HALO, W_POOL), F32)],
        compiler_params=_cparams(("arbitrary", "arbitrary")),
        name="pool_prompt",
    )(x2d, w_in, w_grp, scale, w_out, ln_g, ln_b)


def _pool_sample_kernel(x_ref, st_ref, wi_ref, wg_ref, sc_ref, wo_ref, g_ref, beta_ref, o_ref, v_ref, *,
                        ts, bsz, pos0):
    x = x_ref[...]
    xb = x.astype(BF16)
    v = _dot(xb, wi_ref[:, 0:W_POOL])
    gate = _dot(xb, wi_ref[:, W_POOL:2 * W_POOL])
    v_ref[...] = v
    n_hist = st_ref.shape[0]

    def ext(r):
        return st_ref[r] if r < n_hist else v[(r - n_hist) * bsz:(r - n_hist + 1) * bsz, :]

    d_groups = []
    for gi, w in enumerate(POOL_WINDOWS):
        c = slice(gi * POOL_GROUP_W, (gi + 1) * POOL_GROUP_W)
        per_t = []
        for t in range(ts):
            win_sum = ext(n_hist + t)[:, c]
            for k in range(1, w):
                win_sum = win_sum + ext(n_hist + t - k)[:, c]
            per_t.append(win_sum / float(min(pos0 + t + 1, w)) - ext(n_hist + t)[:, c])
        d_groups.append(jnp.concatenate(per_t, axis=0))
    o_ref[...] = _pool_tail(x, d_groups, gate, wg_ref, sc_ref, wo_ref, g_ref, beta_ref)


def _pool_sample(x_tm, state_tm, w_in, w_grp, scale, w_out, ln_g, ln_b, ts, bsz, pos0):
    return pl.pallas_call(
        functools.partial(_pool_sample_kernel, ts=ts, bsz=bsz, pos0=pos0),
        out_shape=[jax.ShapeDtypeStruct((ts * bsz, D_MODEL), F32), jax.ShapeDtypeStruct((ts * bsz, W_POOL), F32)],
        compiler_params=pltpu.CompilerParams(vmem_limit_bytes=VMEM_LIMIT),
        name="pool_sample",
    )(x_tm, state_tm, w_in, w_grp, scale, w_out, ln_g, ln_b)


def kernel(x_prompt, x_sample, cache_kv_cmp, cache_kv_sel, state_win_kv, state_conv, state_pool, page_table,
           w_in_even, w_cmp, conv_w, conv_b, conv_ln_g, conv_ln_b, w_out_even,
           w_in_odd, w_pool_grp, pool_scale, w_out_odd, ln_g, ln_b):
    batch, seq, _ = x_prompt.shape
    bsz, ts, _ = x_sample.shape
    past = page_table.shape[1] * cache_kv_cmp.shape[2]
    kv_shape = (2, KV_HEADS, HEAD_DIM)
    vec = lambda a: a.reshape(1, -1)
    assert seq >= CONV_WIDTH - 1 and seq >= POOL_MAX - 1 and seq >= WINDOW + Q_TILE and seq % K_TILE == 0

    def token_major(x_t):
        n, _, rows = x_t.shape
        return x_t.reshape((n,) + kv_shape + (rows,)).transpose(0, 4, 1, 2, 3)[None]

    w_pad = _pad_even_weights(w_in_even[0])
    cmp_lanes = min(CMP_LANES, seq, past)
    cmp_consts = _compress_consts(w_cmp[0].reshape(CMP_BLOCK, KV_W), cmp_lanes)
    w_oe = w_out_even[0].astype(BF16)
    cw, cb, cg, cbeta = conv_w[0], vec(conv_b[0]), vec(conv_ln_g[0]), vec(conv_ln_b[0])
    g0, b0 = vec(ln_g[0]), vec(ln_b[0])

    xp = x_prompt.reshape(batch * seq, D_MODEL)
    u, sg, q, bg, gt, kvc_t, kvs_t, kvw_t, kaug, vaug, kw, vwa = _proj_prompt(xp, w_pad, batch, seq, 256)
    a_out = _conv_prompt(u, sg, cw, cb, cg, cbeta, batch, seq, 256)
    n_chunks = seq // CMP_STRIDE
    n_sel = -(-seq // SEL_BLOCK)
    kcv = _compress_prompt(kvc_t, *cmp_consts)
    b_out = _nsa_prompt(q, kaug, vaug, kw, vwa, kcv.reshape(batch, 2 * KV_HEADS, HEAD_DIM, n_chunks), gt, bg,
                        _sel_matrix(n_sel, n_chunks), batch, seq, n_chunks - 1, n_sel)
    xp1 = _mix_out(xp, a_out, b_out, None, w_oe, g0, b0, 256)

    n_win_p = min(WINDOW, seq)
    kvc_p = token_major(kvc_t)
    kvs_p = token_major(kvs_t)
    win_p = token_major(kvw_t[:, :, seq - n_win_p:])
    conv_p = u.reshape(batch, seq, W_CONV)[:, seq - (CONV_WIDTH - 1):][None]

    xs = x_sample.transpose(1, 0, 2).reshape(ts * bsz, D_MODEL)
    us, sgs, qs, bgs, gts, kvcs, kvss, kvws = _proj_sample(xs, w_pad)
    conv_ext = jnp.concatenate([state_conv[0].transpose(1, 0, 2), us.reshape(ts, bsz, W_CONV)], axis=0)
    a_out_s = _conv_sample(conv_ext, sgs.reshape(ts, bsz, W_CONV), cw, cb, cg, cbeta).reshape(ts * bsz, W_CONV)
    b_raw_s = _sample_attention(qs, kvss, kvws, gts, cache_kv_cmp[0], cache_kv_sel[0], state_win_kv[0], page_table,
                                cmp_consts, bsz, ts)
    xs1 = _mix_out(xs, a_out_s, b_raw_s, bgs, w_oe, g0, b0, ts * bsz)

    to_bt = lambda a: a.reshape(ts, bsz, -1).transpose(1, 0, 2)
    kvc_s = to_bt(kvcs).reshape((1, bsz, ts) + kv_shape)
    kvs_s = to_bt(kvss).reshape((1, bsz, ts) + kv_shape)
    win_ext = jnp.concatenate([state_win_kv[0], to_bt(kvws).reshape((bsz, ts) + kv_shape)], axis=1)
    w_len = win_ext.shape[1]
    win_s = win_ext[:, w_len - min(WINDOW, w_len):][None]
    conv_s = conv_ext[-(CONV_WIDTH - 1):].transpose(1, 0, 2)[None]

    wi_o = w_in_odd[0].astype(BF16)
    wg_o = w_pool_grp[0].astype(BF16)
    wo_o = w_out_odd[0].astype(BF16)
    sc_o = vec(pool_scale[0])
    g1, b1 = vec(ln_g[1]), vec(ln_b[1])
    n_keep = POOL_MAX - 1
    yp, tail_p = _pool_prompt(xp1, wi_o, wg_o, sc_o, wo_o, g1, b1, batch, seq, 256)
    pool_p = tail_p[:, POOL_HALO - n_keep:][None]
    st_tm = state_pool[0].transpose(1, 0, 2)
    ys, vs_new = _pool_sample(xs1, st_tm, wi_o, wg_o, sc_o, wo_o, g1, b1, ts, bsz, past)
    pool_ext = jnp.concatenate([st_tm, vs_new.reshape(ts, bsz, W_POOL)], axis=0)
    pool_s = pool_ext[-n_keep:].transpose(1, 0, 2)[None]

    y_prompt = yp.reshape(batch, seq, D_MODEL)
    y_sample = ys.reshape(ts, bsz, D_MODEL).transpose(1, 0, 2)
    return (y_prompt, y_sample, kvc_p, kvs_p, win_p, conv_p, pool_p, kvc_s, kvs_s, win_s, conv_s, pool_s)
```

```python
import functools

import jax
import jax.numpy as jnp
import numpy as np
from jax import lax
from jax.experimental import pallas as pl
from jax.experimental.pallas import tpu as pltpu

F32 = jnp.float32
BF16 = jnp.bfloat16

D_MODEL = 1024
W_CONV = 512
CONV_WIDTH = 31
N_HEADS = 8
HEAD_DIM = 64
KV_HEADS = 2
GROUP = N_HEADS // KV_HEADS
W_ATTN = N_HEADS * HEAD_DIM
KV_W = 2 * KV_HEADS * HEAD_DIM
V_OFF = KV_HEADS * HEAD_DIM
CMP_STRIDE = 16
CMP_BLOCK = 2 * CMP_STRIDE
SEL_BLOCK = 64
N_SEL = 16
WINDOW = 512
W_POOL = 1024
POOL_WINDOWS = (2, 4, 8, 16)
POOL_GROUP_W = W_POOL // len(POOL_WINDOWS)
POOL_MAX = 16
LN_EPS = 1e-5
NEG_INF = -1e30
SEL_FORCE = 1e9
DEPTH = 2
DEEPNORM_ALPHA = (2 * DEPTH) ** 0.25
EVEN_SPLITS = (W_CONV, W_CONV, W_CONV, W_ATTN, KV_W, KV_W, KV_W, 3 * N_HEADS, W_ATTN)

LANES = 128
SUBLANES = 8
E_PAD = 3 * W_CONV + 2 * W_ATTN + 3 * KV_W + KV_HEADS * LANES
VMEM_LIMIT = 56 * 1024 * 1024
Q_TILE = 256
K_TILE = 512
TILES_PER_TRIP = 4
CMP_LANES = 2048
CMP_PIECE = 256
PER_SEL = SEL_BLOCK // CMP_STRIDE
LOWEST = -(2.0 ** 127)
Q_SCALE = HEAD_DIM ** -0.5
MASKED_ROW_FLOOR = 0.5 * NEG_INF


def _cparams(sem):
    return pltpu.CompilerParams(dimension_semantics=sem, vmem_limit_bytes=VMEM_LIMIT)


def _round_up(x, m):
    return m * (-(-x // m))


def _sigmoid(x):
    return 1.0 / (1.0 + jnp.exp(-x))


def _silu(x):
    return x * _sigmoid(x)


def _layer_norm(z, g, b):
    mu = jnp.mean(z, axis=-1, keepdims=True)
    zc = z - mu
    var = jnp.mean(zc * zc, axis=-1, keepdims=True)
    return zc * lax.rsqrt(var + LN_EPS) * g + b


def _dot_t(a, b):
    return lax.dot_general(a, b, (((1,), (1,)), ((), ())), preferred_element_type=F32)


def _dot(a, b):
    return jnp.dot(a, b, preferred_element_type=F32)


def _split3(x):
    hi = x.astype(BF16)
    r1 = x - hi.astype(F32)
    mid = r1.astype(BF16)
    lo = (r1 - mid.astype(F32)).astype(BF16)
    return hi, mid, lo


KV_COL0 = 3 * W_CONV + W_ATTN


def _proj_common(xb, w_ref, u_ref, sg_ref, q_ref, bg_ref, gt_ref):
    def mm(lo, hi):
        return _dot(xb, w_ref[:, lo:hi])

    o = 0
    a_val = mm(o, o + W_CONV); o += W_CONV
    a_glu = mm(o, o + W_CONV); o += W_CONV
    u_ref[...] = a_val * _sigmoid(a_glu)
    sg_ref[...] = _silu(mm(o, o + W_CONV)); o += W_CONV
    q_ref[...] = (mm(o, o + W_ATTN) * Q_SCALE).astype(BF16); o += W_ATTN
    o += 3 * KV_W
    bg_ref[...] = _silu(mm(o, o + W_ATTN)); o += W_ATTN
    gt_ref[...] = _sigmoid(mm(o, o + KV_HEADS * LANES))


def _proj_sample_kernel(x_ref, w_ref, u_ref, sg_ref, q_ref, bg_ref, gt_ref, kvc_ref, kvs_ref, kvw_ref):
    xb = x_ref[...].astype(BF16)
    _proj_common(xb, w_ref, u_ref, sg_ref, q_ref, bg_ref, gt_ref)
    for k, ref in enumerate((kvc_ref, kvs_ref, kvw_ref)):
        ref[...] = _dot(xb, w_ref[:, KV_COL0 + k * KV_W:KV_COL0 + (k + 1) * KV_W])


def _proj_prompt_kernel(x_ref, w_ref, wkv_t_ref, u_ref, sg_ref, q_ref, bg_ref, gt_ref, kvc_t_ref, kvs_t_ref,
                        kvw_t_ref, kaug_ref, vaug_ref, kw_ref, vwa_ref, *, tm, nt, n_sel):
    xb = x_ref[...].astype(BF16)
    _proj_common(xb, w_ref, u_ref, sg_ref, q_ref, bg_ref, gt_ref)
    kvc_t_ref[0] = _dot_t(wkv_t_ref[0:KV_W, :], xb)
    ks_t = _dot_t(wkv_t_ref[KV_W:2 * KV_W, :], xb)
    kvs_t_ref[0] = ks_t
    kw_t = _dot_t(wkv_t_ref[2 * KV_W:3 * KV_W, :], xb)
    kvw_t_ref[0] = kw_t
    key = (pl.program_id(0) % nt) * tm + lax.broadcasted_iota(jnp.int32, (n_sel, tm), 1)
    blk = lax.broadcasted_iota(jnp.int32, (n_sel, tm), 0)
    onehot = jnp.where(key // SEL_BLOCK == blk, NEG_INF, 0.0).astype(BF16)
    ones_row = jnp.where(lax.broadcasted_iota(jnp.int32, (LANES - HEAD_DIM, tm), 0) == 0, 1.0, 0.0).astype(BF16)
    n_zero = kaug_ref.shape[2] - n_sel - HEAD_DIM
    for h in range(KV_HEADS):
        k_rows = slice(h * HEAD_DIM, (h + 1) * HEAD_DIM)
        v_rows = slice(V_OFF + h * HEAD_DIM, V_OFF + (h + 1) * HEAD_DIM)
        kaug_ref[0, h, 0:n_sel, :] = onehot
        kaug_ref[0, h, n_sel:n_sel + HEAD_DIM, :] = ks_t[k_rows].astype(BF16)
        if n_zero:
            kaug_ref[0, h, n_sel + HEAD_DIM:, :] = jnp.zeros((n_zero, tm), BF16)
        vaug_ref[0, h, 0:HEAD_DIM, :] = ks_t[v_rows].astype(BF16)
        vaug_ref[0, h, HEAD_DIM:, :] = ones_row
        kw_ref[0, h] = kw_t[k_rows].astype(BF16)
        vwa_ref[0, h, 0:HEAD_DIM, :] = kw_t[v_rows].astype(BF16)
        vwa_ref[0, h, HEAD_DIM:, :] = ones_row


def _proj_row_specs(n, tm):
    row = lambda w: pl.BlockSpec((tm, w), lambda i: (i, 0))
    widths = (W_CONV, W_CONV, W_ATTN, W_ATTN, KV_HEADS * LANES)
    dtypes = (F32, F32, BF16, F32, F32)
    return row, [row(w) for w in widths], [jax.ShapeDtypeStruct((n, w), d) for w, d in zip(widths, dtypes)]


def _proj_sample(x2d, w_pad):
    n = x2d.shape[0]
    row, specs, shapes = _proj_row_specs(n, n)
    return pl.pallas_call(
        _proj_sample_kernel,
        grid=(1,),
        in_specs=[row(D_MODEL), pl.BlockSpec((D_MODEL, E_PAD), lambda i: (0, 0))],
        out_specs=specs + [row(KV_W)] * 3,
        out_shape=shapes + [jax.ShapeDtypeStruct((n, KV_W), F32)] * 3,
        compiler_params=_cparams(("arbitrary",)),
        name="proj_sample",
    )(x2d, w_pad)


def _proj_prompt(x2d, w_pad, batch, seq, tm):
    n = x2d.shape[0]
    wkv_t = w_pad[:, KV_COL0:KV_COL0 + 3 * KV_W].T
    nt = seq // tm
    n_sel = -(-seq // SEL_BLOCK)
    k_rows = _round_up(n_sel + HEAD_DIM, LANES)
    row, specs, shapes = _proj_row_specs(n, tm)
    chan = pl.BlockSpec((1, KV_W, tm), lambda i: (i // nt, 0, i % nt))
    head = lambda r: pl.BlockSpec((1, KV_HEADS, r, tm), lambda i: (i // nt, 0, 0, i % nt))
    hshape = lambda r: jax.ShapeDtypeStruct((batch, KV_HEADS, r, seq), BF16)
    return pl.pallas_call(
        functools.partial(_proj_prompt_kernel, tm=tm, nt=nt, n_sel=n_sel),
        grid=(n // tm,),
        in_specs=[row(D_MODEL), pl.BlockSpec((D_MODEL, E_PAD), lambda i: (0, 0)),
                  pl.BlockSpec((3 * KV_W, D_MODEL), lambda i: (0, 0))],
        out_specs=specs + [chan] * 3 + [head(k_rows), head(LANES), head(HEAD_DIM), head(LANES)],
        out_shape=shapes + [jax.ShapeDtypeStruct((batch, KV_W, seq), F32)] * 3 + [
            hshape(k_rows), hshape(LANES), hshape(HEAD_DIM), hshape(LANES)],
        compiler_params=_cparams(("parallel",)),
        name="proj_prompt",
    )(x2d, w_pad, wkv_t)


def _pad_even_weights(w):
    offs = np.cumsum(EVEN_SPLITS)[:-1].tolist()
    a_val, a_glu, a_gate, wq, wkc, wks, wkw, wg, wbg = jnp.split(w, offs, axis=1)
    wg = wg.reshape(D_MODEL, KV_HEADS, GROUP, 3).transpose(0, 1, 3, 2).reshape(D_MODEL, KV_HEADS, 3 * GROUP)
    wg = jnp.pad(wg, ((0, 0), (0, 0), (0, LANES - 3 * GROUP))).reshape(D_MODEL, KV_HEADS * LANES)
    return jnp.concatenate([a_val, a_glu, a_gate, wq, wkc, wks, wkw, wbg, wg], axis=1).astype(BF16)


CONV_HALO = 32
CONV_CHUNK = 32


def _conv_prompt_kernel(u_ref, sg_ref, w_ref, cb_ref, g_ref, b_ref, o_ref, ext_ref, sh_ref, *, tt):
    @pl.when(pl.program_id(1) == 0)
    def _():
        ext_ref[0:CONV_HALO, :] = jnp.zeros((CONV_HALO, W_CONV), F32)

    ext_ref[CONV_HALO:CONV_HALO + tt, :] = u_ref[...]
    base = CONV_HALO - (CONV_WIDTH - 1)
    n_sh = sh_ref.shape[1]
    for s in range(1, SUBLANES):
        sh_ref[s - 1] = ext_ref[s:s + n_sh, :]

    def window(start):
        s, off = start % SUBLANES, start - start % SUBLANES
        return ext_ref[off:off + CONV_CHUNK, :] if s == 0 else sh_ref[s - 1, off:off + CONV_CHUNK, :]

    for c in range(tt // CONV_CHUNK):
        r0 = c * CONV_CHUNK
        acc = jnp.zeros((CONV_CHUNK, W_CONV), F32) + cb_ref[...]
        for k in range(CONV_WIDTH):
            acc = acc + w_ref[k:k + 1, :] * window(base + r0 + k)
        y = _layer_norm(acc, g_ref[...], b_ref[...])
        o_ref[r0:r0 + CONV_CHUNK, :] = (_silu(y) * sg_ref[r0:r0 + CONV_CHUNK, :]).astype(o_ref.dtype)
    ext_ref[0:CONV_HALO, :] = ext_ref[tt:tt + CONV_HALO, :]


def _conv_prompt(u2d, sg2d, conv_w, conv_b, ln_g, ln_b, batch, seq, tt):
    nt = seq // tt
    row = pl.BlockSpec((tt, W_CONV), lambda b, i: (b * nt + i, 0))
    vec = pl.BlockSpec((1, W_CONV), lambda b, i: (0, 0))
    return pl.pallas_call(
        functools.partial(_conv_prompt_kernel, tt=tt),
        grid=(batch, nt),
        in_specs=[row, row, pl.BlockSpec((CONV_WIDTH, W_CONV), lambda b, i: (0, 0)), vec, vec, vec],
        out_specs=row,
        out_shape=jax.ShapeDtypeStruct((batch * seq, W_CONV), BF16),
        scratch_shapes=[pltpu.VMEM((tt + CONV_HALO, W_CONV), F32),
                        pltpu.VMEM((SUBLANES - 1, tt + CONV_HALO - SUBLANES, W_CONV), F32)],
        compiler_params=_cparams(("arbitrary", "arbitrary")),
        name="conv_prompt",
    )(u2d, sg2d, conv_w, conv_b, ln_g, ln_b)


def _conv_sample_kernel(ext_ref, sg_ref, w_ref, cb_ref, g_ref, b_ref, o_ref, *, ts):
    for t in range(ts):
        acc = jnp.zeros(ext_ref.shape[1:], F32) + cb_ref[...]
        for k in range(CONV_WIDTH):
            acc = acc + w_ref[k:k + 1, :] * ext_ref[t + k]
        y = _layer_norm(acc, g_ref[...], b_ref[...])
        o_ref[t] = _silu(y) * sg_ref[t]


def _conv_sample(ext, sg, conv_w, conv_b, ln_g, ln_b):
    ts, bsz, _ = sg.shape
    return pl.pallas_call(
        functools.partial(_conv_sample_kernel, ts=ts),
        out_shape=jax.ShapeDtypeStruct((ts, bsz, W_CONV), F32),
        name="conv_sample",
    )(ext, sg, conv_w, conv_b, ln_g, ln_b)


def _compress_span(x_ext, w1_ref, w2_ref, s_ref):
    lanes = x_ext.shape[1] - LANES
    piece = min(lanes, CMP_PIECE)
    reps = piece // LANES
    out = None
    for c in range(lanes // piece):
        xp = x_ext[:, c * piece:(c + 1) * piece + LANES]
        later = pltpu.roll(xp, piece + LANES - CMP_STRIDE, 1)[:, 0:piece]
        z = xp[:, 0:piece] * jnp.tile(w1_ref[...], (1, reps)) + later * jnp.tile(w2_ref[...], (1, reps))
        hi = z.astype(BF16)
        lo = (z - hi.astype(F32)).astype(BF16)
        seg = s_ref[c * piece:(c + 1) * piece, :]
        part = _dot(hi, seg) + _dot(lo, seg)
        out = part if out is None else out + part
    return out


def _compress_prompt_kernel(x_ref, w1_ref, w2_ref, s_ref, o_ref, *, lanes):
    seq = x_ref.shape[2]
    parts = []
    for c in range(seq // lanes):
        if (c + 1) * lanes + LANES <= seq:
            x_ext = x_ref[0, :, c * lanes:(c + 1) * lanes + LANES]
        else:
            x_ext = jnp.concatenate([x_ref[0, :, c * lanes:(c + 1) * lanes], jnp.zeros((KV_W, LANES), F32)], axis=1)
        parts.append(_compress_span(x_ext, w1_ref, w2_ref, s_ref))
    o_ref[0] = jnp.concatenate(parts, axis=1)


def _compress_prompt(kvc_t, w1t, w2t, seg):
    batch, _, seq = kvc_t.shape
    lanes = seg.shape[0]
    const = lambda a: pl.BlockSpec(a.shape, lambda b: (0, 0))
    return pl.pallas_call(
        functools.partial(_compress_prompt_kernel, lanes=lanes),
        grid=(batch,),
        in_specs=[pl.BlockSpec((1, KV_W, seq), lambda b: (b, 0, 0)), const(w1t), const(w2t), const(seg)],
        out_specs=pl.BlockSpec((1, KV_W, seq // CMP_STRIDE), lambda b: (b, 0, 0)),
        out_shape=jax.ShapeDtypeStruct((batch, KV_W, seq // CMP_STRIDE), F32),
        compiler_params=_cparams(("parallel",)),
        name="compress_prompt",
    )(kvc_t, w1t, w2t, seg)


def _compress_paged_kernel(pt_ref, *refs, n_pages_step):
    del pt_ref
    pages = refs[:n_pages_step + 1]
    w1_ref, w2_ref, s_ref, o_ref = refs[n_pages_step + 1:]
    x_ext = jnp.concatenate([p[0] for p in pages[:-1]] + [pages[-1][0, :, 0:LANES]], axis=1)
    o_ref[0] = _compress_span(x_ext, w1_ref, w2_ref, s_ref)


def _compress_paged(cache_t, page_table, w1t, w2t, seg):
    _, _, page = cache_t.shape
    bsz, n_pages = page_table.shape
    lanes = seg.shape[0]
    pps = lanes // page
    assert page >= LANES and lanes % page == 0 and n_pages % pps == 0
    n_steps = n_pages // pps
    n_out = lanes // CMP_STRIDE

    pt_ext = jnp.concatenate([page_table, page_table[:, -1:]], axis=1).reshape(-1)

    def page_spec(k):
        return pl.BlockSpec((1, KV_W, page), lambda b, s, pt: (pt[b * (n_pages + 1) + s * pps + k], 0, 0))

    const = lambda a: pl.BlockSpec(a.shape, lambda b, s, pt: (0, 0))
    grid_spec = pltpu.PrefetchScalarGridSpec(
        num_scalar_prefetch=1,
        grid=(bsz, n_steps),
        in_specs=[page_spec(k) for k in range(pps + 1)] + [const(w1t), const(w2t), const(seg)],
        out_specs=pl.BlockSpec((1, KV_W, n_out), lambda b, s, pt: (b, 0, s)),
    )
    return pl.pallas_call(
        functools.partial(_compress_paged_kernel, n_pages_step=pps),
        grid_spec=grid_spec,
        out_shape=jax.ShapeDtypeStruct((bsz, KV_W, n_steps * n_out), F32),
        compiler_params=_cparams(("parallel", "arbitrary")),
        name="compress_paged",
    )(pt_ext, *([cache_t] * (pps + 1)), w1t, w2t, seg)


def _compress_consts(w_cmp2d, lanes):
    w1t = jnp.tile(w_cmp2d[:CMP_STRIDE].T, (1, LANES // CMP_STRIDE))
    w2t = jnp.tile(w_cmp2d[CMP_STRIDE:].T, (1, LANES // CMP_STRIDE))
    seg = jnp.asarray(np.arange(lanes)[:, None] // CMP_STRIDE == np.arange(lanes // CMP_STRIDE)[None, :], BF16)
    return w1t, w2t, seg


N_FORCED = 3


def _pick_top_block(sc, blk):
    mx = jnp.max(sc, axis=0, keepdims=True)
    jm = jnp.min(jnp.where(sc == mx, blk, sc.shape[0]), axis=0, keepdims=True)
    return jnp.where(blk == jm, LOWEST, sc)


def _force_scores(score, blk, cur, forced_value=SEL_FORCE):
    forced = (blk == 0) | ((blk >= cur - 1) & (blk <= cur))
    return jnp.where(forced, forced_value, jnp.where(blk <= cur, score, -SEL_FORCE))


def _sel_matrix(n_sel_blocks, n_cols):
    n = np.arange(n_cols)
    m = (n[None, :] // PER_SEL == np.arange(n_sel_blocks)[:, None]) & (n[None, :] % PER_SEL < PER_SEL - 1)
    return jnp.asarray(m, BF16)


def _cmp_valid(n, qpos, n_cmp):
    return n * CMP_STRIDE + (CMP_BLOCK - 1) <= jnp.minimum(qpos, (n_cmp - 1) * CMP_STRIDE + CMP_BLOCK - 1)


def _nsa_prompt_kernel(q_ref, kaug_ref, vaug_ref, kw_ref, vwa_ref, kc_ref, vc_ref, gt_ref, bg_ref, m01_ref, o_ref,
                       qa_ref, acc_ref, mx_ref, s_ref, *, n_cmp, n_sel):
    i = pl.program_id(2)
    rows = GROUP * Q_TILE
    q = q_ref[...]
    qs = jnp.concatenate([q[:, g * HEAD_DIM:(g + 1) * HEAD_DIM] for g in range(GROUP)], axis=0)
    n_cols = kc_ref.shape[3]

    def row_pos(width):
        return i * Q_TILE + (lax.broadcasted_iota(jnp.int32, (rows, width), 0) & (Q_TILE - 1))

    tpos = i * Q_TILE + lax.broadcasted_iota(jnp.int32, (Q_TILE, n_cols), 0)
    c_bias = jnp.where(_cmp_valid(lax.broadcasted_iota(jnp.int32, (Q_TILE, n_cols), 1), tpos, n_cmp), 0.0, NEG_INF)
    s_all = _dot(qs, kc_ref[0, 0].astype(BF16))
    p_parts = []
    p_grp = None
    for g in range(GROUP):
        s = s_all[g * Q_TILE:(g + 1) * Q_TILE] + c_bias
        e = jnp.exp(s - jnp.maximum(jnp.max(s, axis=1, keepdims=True), MASKED_ROW_FLOOR))
        l = jnp.sum(e, axis=1, keepdims=True)
        p = e * (1.0 / jnp.where(l > 0.0, l, 1.0))
        p_parts.append(p.astype(BF16))
        p_grp = p if p_grp is None else p_grp + p
    o_c = _dot_t(jnp.concatenate(p_parts, axis=0), vc_ref[0, 0].astype(BF16))

    score_t = sum(_dot_t(m01_ref[...], part) for part in _split3(p_grp))
    blk = lax.broadcasted_iota(jnp.int32, (n_sel, Q_TILE), 0)
    tok = i * Q_TILE + lax.broadcasted_iota(jnp.int32, (n_sel, Q_TILE), 1)
    sc = _force_scores(score_t, blk, tok // SEL_BLOCK, LOWEST)
    chains = [sc[:, c * LANES:(c + 1) * LANES] for c in range(Q_TILE // LANES)]
    blk_c = blk[:, 0:LANES]
    n_pick = max(min(N_SEL, n_sel) - N_FORCED, 0)

    w_keys = WINDOW + Q_TILE
    w0 = pl.multiple_of(jnp.maximum(i * Q_TILE - WINDOW, 0), Q_TILE)
    rel = (i * Q_TILE + lax.broadcasted_iota(jnp.int32, (Q_TILE, w_keys), 0)
           - (w0 + lax.broadcasted_iota(jnp.int32, (Q_TILE, w_keys), 1)))
    w_bias = jnp.where((rel >= 0) & (rel < WINDOW), 0.0, NEG_INF)
    s_w_all = _dot(qs, kw_ref[0, 0, :, pl.ds(w0, w_keys)])
    e_w_parts = []
    done = 0
    for g in range(GROUP):
        s_w = s_w_all[g * Q_TILE:(g + 1) * Q_TILE] + w_bias
        e_w_parts.append(jnp.exp(s_w - jnp.max(s_w, axis=1, keepdims=True)).astype(BF16))
        upto = n_pick * (g + 1) // GROUP
        for _ in range(done, upto):
            chains = [_pick_top_block(c, blk_c) for c in chains]
        done = upto
    acc_w = _dot_t(jnp.concatenate(e_w_parts, axis=0), vwa_ref[0, 0, :, pl.ds(w0, w_keys)])
    o_w = acc_w[:, 0:HEAD_DIM] / acc_w[:, HEAD_DIM:HEAD_DIM + 1]
    not_sel_t = jnp.where(jnp.concatenate(chains, axis=1) == LOWEST, 0.0, 1.0)
    not_sel = not_sel_t.T.astype(BF16)
    for g in range(GROUP):
        qa_ref[g * Q_TILE:(g + 1) * Q_TILE, 0:n_sel] = not_sel
    qa_ref[:, n_sel:n_sel + HEAD_DIM] = qs
    if qa_ref.shape[1] > n_sel + HEAD_DIM:
        qa_ref[:, n_sel + HEAD_DIM:] = jnp.zeros((rows, qa_ref.shape[1] - n_sel - HEAD_DIM), BF16)

    mx_ref[...] = jnp.full(mx_ref.shape, NEG_INF, F32)

    def score_tiles(j, n_tiles, causal_last):
        for t in range(n_tiles):
            k0 = pl.multiple_of((j + t) * K_TILE, K_TILE)
            s = _dot(qa_ref[...], kaug_ref[0, 0, :, pl.ds(k0, K_TILE)])
            if causal_last and t == n_tiles - 1:
                kpos = k0 + lax.broadcasted_iota(jnp.int32, (rows, K_TILE), 1)
                s = jnp.where(kpos <= row_pos(K_TILE), s, NEG_INF)
            s_ref[j + t] = s
            part = s[:, 0:LANES]
            for c in range(1, K_TILE // LANES):
                part = jnp.maximum(part, s[:, c * LANES:(c + 1) * LANES])
            mx_ref[...] = jnp.maximum(mx_ref[...], part)

    def acc_tiles(j, n_tiles):
        m_b = jnp.tile(mx_ref[...], (1, K_TILE // LANES))
        for t in range(n_tiles):
            k0 = pl.multiple_of((j + t) * K_TILE, K_TILE)
            pe = jnp.exp(s_ref[j + t] - m_b).astype(BF16)
            acc_ref[...] += _dot_t(pe, vaug_ref[0, 0, :, pl.ds(k0, K_TILE)])

    def grouped(n, fn, tail_fn, tail_min):
        def body(jj, carry):
            fn(TILES_PER_TRIP * jj, TILES_PER_TRIP)
            return carry
        lax.fori_loop(0, n // TILES_PER_TRIP, body, 0)
        for r in range(tail_min, TILES_PER_TRIP):
            @pl.when(n % TILES_PER_TRIP == r)
            def _(r=r):
                tail_fn(n - r, r)

    n_full = (i * Q_TILE) // K_TILE
    grouped(n_full, lambda j, n: score_tiles(j, n, False), lambda j, r: score_tiles(j, r + 1, True), 0)

    mx_ref[...] = jnp.broadcast_to(jnp.max(mx_ref[...], axis=1, keepdims=True), mx_ref.shape)
    acc_ref[...] = jnp.zeros(acc_ref.shape, F32)
    grouped(n_full + 1, acc_tiles, acc_tiles, 1)

    acc = acc_ref[...]
    o_s = acc[:, 0:HEAD_DIM] / acc[:, HEAD_DIM:HEAD_DIM + 1]

    gt = gt_ref[...]
    outs = []
    for g in range(GROUP):
        r = slice(g * Q_TILE, (g + 1) * Q_TILE)
        outs.append(gt[:, g:g + 1] * o_c[r] + gt[:, GROUP + g:GROUP + g + 1] * o_s[r]
                    + gt[:, 2 * GROUP + g:2 * GROUP + g + 1] * o_w[r])
    o_ref[...] = (jnp.concatenate(outs, axis=1) * bg_ref[...]).astype(o_ref.dtype)


def _nsa_prompt(q2d, kaug, vaug, kw, vwa, kcv4, gt2d, bg2d, m01, batch, seq, n_cmp, n_sel):
    nq = seq // Q_TILE
    rows = GROUP * Q_TILE
    n_cols = kcv4.shape[-1]
    per_head = lambda a: pl.BlockSpec((1, 1) + a.shape[2:], lambda b, h, i: (b, h, 0, 0),
                                      pipeline_mode=pl.Buffered(1))
    return pl.pallas_call(
        functools.partial(_nsa_prompt_kernel, n_cmp=n_cmp, n_sel=n_sel),
        grid=(batch, KV_HEADS, nq),
        in_specs=[pl.BlockSpec((Q_TILE, GROUP * HEAD_DIM), lambda b, h, i: (b * nq + i, h)),
                  per_head(kaug), per_head(vaug), per_head(kw), per_head(vwa),
                  pl.BlockSpec((1, 1, HEAD_DIM, n_cols), lambda b, h, i: (b, h, 0, 0)),
                  pl.BlockSpec((1, 1, HEAD_DIM, n_cols), lambda b, h, i: (b, KV_HEADS + h, 0, 0)),
                  pl.BlockSpec((Q_TILE, LANES), lambda b, h, i: (b * nq + i, h)),
                  pl.BlockSpec((Q_TILE, GROUP * HEAD_DIM), lambda b, h, i: (b * nq + i, h)),
                  pl.BlockSpec(m01.shape, lambda b, h, i: (0, 0))],
        out_specs=pl.BlockSpec((Q_TILE, GROUP * HEAD_DIM), lambda b, h, i: (b * nq + i, h)),
        out_shape=jax.ShapeDtypeStruct((batch * seq, W_ATTN), BF16),
        scratch_shapes=[pltpu.VMEM((rows, kaug.shape[2]), BF16), pltpu.VMEM((rows, LANES), F32),
                        pltpu.VMEM((rows, LANES), F32), pltpu.VMEM((seq // K_TILE, rows, K_TILE), F32)],
        compiler_params=_cparams(("parallel", "parallel", "arbitrary")),
        name="nsa_prompt",
    )(q2d, kaug, vaug, kw, vwa, kcv4, kcv4, gt2d, bg2d, m01)


def _new_token_scores(qf, nk, tq, ts):
    cols = []
    for t in range(ts):
        s = jnp.sum(qf * nk[t:t + 1, :], axis=1, keepdims=True)
        cols.append(jnp.where(tq >= t, s, NEG_INF))
    return cols


def _bf16_round(x):
    return x.astype(BF16).astype(F32)


def _nsa_sample_a_kernel(q_ref, kcv_ref, win_ref, new_ref, gt_ref, m01_ref, part_ref, idx_ref, *,
                         past, ts, n_cmp):
    for bi in range(q_ref.shape[0]):
        _nsa_sample_a_one(bi, q_ref, kcv_ref, win_ref, new_ref, gt_ref, m01_ref, part_ref, idx_ref,
                          past=past, ts=ts, n_cmp=n_cmp)


def _nsa_sample_a_one(bi, q_ref, kcv_ref, win_ref, new_ref, gt_ref, m01_ref, part_ref, idx_ref, *,
                      past, ts, n_cmp):
    rows = GROUP * ts
    kcv = kcv_ref[bi]
    win = win_ref[bi]
    new = new_ref[bi]
    n_cols = kcv.shape[1]
    w_hist = win.shape[1]
    n_blk_pad = m01_ref.shape[0]
    tq = lax.broadcasted_iota(jnp.int32, (rows, 1), 0) % ts
    qpos = past + tq
    scores = []
    for h in range(KV_HEADS):
        qh = q_ref[bi, h]
        ksl = slice(h * HEAD_DIM, (h + 1) * HEAD_DIM)
        vsl = slice(V_OFF + h * HEAD_DIM, V_OFF + (h + 1) * HEAD_DIM)
        s = _dot(qh, kcv[ksl].astype(BF16))
        valid = _cmp_valid(lax.broadcasted_iota(jnp.int32, (rows, n_cols), 1), qpos, n_cmp)
        s = jnp.where(valid, s, NEG_INF)
        e = jnp.where(valid, jnp.exp(s - jnp.max(s, axis=1, keepdims=True)), 0.0)
        l = jnp.sum(e, axis=1, keepdims=True)
        p = e / jnp.where(l > 0.0, l, 1.0)
        o_c = _dot_t(p.astype(BF16), kcv[vsl].astype(BF16))
        p_grp = p[0:ts]
        for g in range(1, GROUP):
            p_grp = p_grp + p[g * ts:(g + 1) * ts]
        scores.append(sum(_dot_t(part, m01_ref[...]) for part in _split3(p_grp)))

        s1 = _dot(qh, win[ksl].astype(BF16))
        rel = qpos - (past - w_hist + lax.broadcasted_iota(jnp.int32, (rows, w_hist), 1))
        valid = (rel >= 0) & (rel <= jnp.minimum(qpos, WINDOW - 1))
        s1 = jnp.where(valid, s1, NEG_INF)
        s2 = _new_token_scores(qh.astype(F32), _bf16_round(new[:, ksl]), tq, ts)
        m = jnp.max(s1, axis=1, keepdims=True)
        for c in s2:
            m = jnp.maximum(m, c)
        e1 = jnp.where(valid, jnp.exp(s1 - m), 0.0)
        l = jnp.sum(e1, axis=1, keepdims=True)
        o_w = _dot_t(e1.astype(BF16), win[vsl].astype(BF16))
        nv = _bf16_round(new[:, vsl])
        for t in range(ts):
            e2 = jnp.exp(s2[t] - m)
            l = l + e2
            o_w = o_w + _bf16_round(e2) * nv[t:t + 1, :]
        gt = gt_ref[bi, h]
        part_ref[bi, h] = gt[:, 0:1] * o_c + gt[:, 2:3] * (o_w / l)

    sc = jnp.concatenate(scores, axis=0)
    nr = KV_HEADS * ts
    blk = lax.broadcasted_iota(jnp.int32, (nr, n_blk_pad), 1)
    cur = (past + (lax.broadcasted_iota(jnp.int32, (nr, 1), 0) % ts)) // SEL_BLOCK
    sc = _force_scores(sc, blk, cur)
    lane = lax.broadcasted_iota(jnp.int32, (nr, LANES), 1)
    idx = jnp.zeros((nr, LANES), jnp.int32)
    for k in range(N_SEL):
        mx = jnp.max(sc, axis=1, keepdims=True)
        jm = jnp.min(jnp.where(sc == mx, blk, n_blk_pad), axis=1, keepdims=True)
        idx = jnp.where(lane == k, jm, idx)
        sc = jnp.where(blk == jm, LOWEST, sc)
    idx_ref[bi] = idx


def _nsa_sample_a(q4, kcv_t, win_t, new_w, gt4, m01, past, ts, n_cmp):
    bsz = q4.shape[0]
    rows = GROUP * ts
    nb = max(d for d in (4, 2, 1) if bsz % d == 0)
    per_b = lambda a: pl.BlockSpec((nb,) + a.shape[1:], lambda b: (b,) + (0,) * (a.ndim - 1))
    return pl.pallas_call(
        functools.partial(_nsa_sample_a_kernel, past=past, ts=ts, n_cmp=n_cmp),
        grid=(bsz // nb,),
        in_specs=[per_b(q4), per_b(kcv_t), per_b(win_t), per_b(new_w), per_b(gt4),
                  pl.BlockSpec(m01.shape, lambda b: (0, 0))],
        out_specs=[pl.BlockSpec((nb, KV_HEADS, rows, HEAD_DIM), lambda b: (b, 0, 0, 0)),
                   pl.BlockSpec((nb, KV_HEADS * ts, LANES), lambda b: (b, 0, 0))],
        out_shape=[jax.ShapeDtypeStruct((bsz, KV_HEADS, rows, HEAD_DIM), F32),
                   jax.ShapeDtypeStruct((bsz, KV_HEADS * ts, LANES), jnp.int32)],
        compiler_params=_cparams(("parallel",)),
        name="nsa_sample_scores",
    )(q4, kcv_t, win_t, new_w, gt4, m01)


def _nsa_sample_b_kernel(idx_ref, pt_ref, *refs, past, ts, n_past_blocks, page):
    del pt_ref
    kv_blocks = refs[:N_SEL]
    q_ref, new_ref, gt_ref, part_ref, o_ref = refs[N_SEL:]
    r = pl.program_id(0)
    t = r % ts
    head0 = ((r // ts) % KV_HEADS) == 0
    bpp = page // SEL_BLOCK
    q = q_ref[0]
    lane = lax.broadcasted_iota(jnp.int32, (GROUP, page), 1)

    def pick(x, off):
        return jnp.where(head0, x[:, off:off + HEAD_DIM], x[:, off + HEAD_DIM:off + 2 * HEAD_DIM])

    s_all = _dot(q, jnp.concatenate([kv_blocks[k][0, 0, 0].astype(BF16) for k in range(N_SEL)], axis=1))
    s_list = []
    has_new = False
    for k in range(N_SEL):
        j = idx_ref[r * N_SEL + k]
        is_past = j < n_past_blocks
        has_new = jnp.logical_or(has_new, jnp.logical_not(is_past))
        off = (j % bpp) * SEL_BLOCK
        page_start = (j // bpp) * page
        lo = jnp.where(is_past, off, page)
        hi = jnp.minimum(off + SEL_BLOCK, past + t + 1 - page_start)
        s_list.append(jnp.where((lane >= lo) & (lane < hi), s_all[:, k * page:(k + 1) * page], NEG_INF))
    new = new_ref[0]
    tq = jnp.where(has_new, t, -1) + jnp.zeros((GROUP, 1), jnp.int32)
    s_new = _new_token_scores(q.astype(F32), _bf16_round(pick(new, 0)), tq, ts)
    m = s_new[0]
    for c in s_new[1:]:
        m = jnp.maximum(m, c)
    for s in s_list:
        m = jnp.maximum(m, jnp.max(s, axis=1, keepdims=True))
    nv = _bf16_round(pick(new, V_OFF))
    l = jnp.zeros((GROUP, 1), F32)
    o = jnp.zeros((GROUP, HEAD_DIM), F32)
    for tt in range(ts):
        e2 = jnp.exp(s_new[tt] - m)
        l = l + e2
        o = o + _bf16_round(e2) * nv[tt:tt + 1, :]
    e_all = jnp.exp(jnp.concatenate(s_list, axis=1) - m)
    l = l + jnp.sum(e_all, axis=1, keepdims=True)
    o = o + _dot_t(e_all.astype(BF16),
                   jnp.concatenate([kv_blocks[k][0, 1, 0].astype(BF16) for k in range(N_SEL)], axis=1))
    gt = gt_ref[0]
    o_ref[0] = part_ref[0] + gt[:, 1:2] * (o / l)


def _nsa_sample_b(idx_flat, page_flat, cache_slabs, q_rows, new_rows, gt_rows, part_rows, past, ts, page):
    n_rows = q_rows.shape[0]
    n_past_blocks = past // SEL_BLOCK

    def slab_spec(k):
        return pl.BlockSpec((1, 2, 1, HEAD_DIM, page),
                            lambda r, idx, pg: (pg[r * N_SEL + k], 0, (r // ts) % KV_HEADS, 0, 0))

    row3 = lambda a: pl.BlockSpec((1,) + a.shape[1:], lambda r, idx, pt: (r, 0, 0))
    grid_spec = pltpu.PrefetchScalarGridSpec(
        num_scalar_prefetch=2,
        grid=(n_rows,),
        in_specs=[slab_spec(k) for k in range(N_SEL)] + [
            row3(q_rows),
            pl.BlockSpec((1,) + new_rows.shape[1:], lambda r, idx, pt: (r // (KV_HEADS * ts), 0, 0)),
            row3(gt_rows), row3(part_rows)],
        out_specs=row3(part_rows),
    )
    return pl.pallas_call(
        functools.partial(_nsa_sample_b_kernel, past=past, ts=ts, n_past_blocks=n_past_blocks, page=page),
        grid_spec=grid_spec,
        out_shape=jax.ShapeDtypeStruct(part_rows.shape, F32),
        compiler_params=_cparams(("arbitrary",)),
        name="nsa_sample_select",
    )(idx_flat, page_flat, *([cache_slabs] * N_SEL), q_rows, new_rows, gt_rows, part_rows)


def _channel_major(x):
    n, rows = x.shape[:2]
    return x.transpose(0, 2, 3, 4, 1).reshape(n, KV_W, rows)


def _sample_attention(q_tm, kvs_tm, kvw_tm, gt_tm, cache_c, cache_s, state_win, page_table, cmp_consts, bsz, ts):
    n_phys, page = cache_c.shape[0], cache_c.shape[1]
    n_pages = page_table.shape[1]
    past = n_pages * page
    w_hist = state_win.shape[1]
    assert ts < CMP_STRIDE and page % SEL_BLOCK == 0 and ts <= WINDOW and w_hist <= past
    total = past + ts
    n_cmp = total // CMP_STRIDE - 1
    n_sel_blocks = -(-total // SEL_BLOCK)
    kcv_t = _compress_paged(_channel_major(cache_c), page_table, *cmp_consts)
    m01 = _sel_matrix(_round_up(n_sel_blocks, LANES), kcv_t.shape[2])
    q4 = q_tm.reshape(ts, bsz, KV_HEADS, GROUP, HEAD_DIM).transpose(1, 2, 3, 0, 4).reshape(
        bsz, KV_HEADS, GROUP * ts, HEAD_DIM)
    g5 = gt_tm.reshape(ts, bsz, KV_HEADS, LANES)[..., :3 * GROUP].reshape(ts, bsz, KV_HEADS, 3, GROUP)
    g5 = g5.transpose(1, 2, 4, 0, 3).reshape(bsz, KV_HEADS, GROUP * ts, 3)
    gt4 = jnp.pad(g5, ((0, 0), (0, 0), (0, 0), (0, LANES - 3)))
    to_bt = lambda a: a.reshape(ts, bsz, KV_W).transpose(1, 0, 2)
    part, idx = _nsa_sample_a(q4, kcv_t, _channel_major(state_win), to_bt(kvw_tm), gt4, m01, past, ts, n_cmp)
    to_rows = lambda a: a.reshape(bsz, KV_HEADS, GROUP, ts, a.shape[-1]).transpose(0, 1, 3, 2, 4).reshape(
        bsz * KV_HEADS * ts, GROUP, a.shape[-1])
    slabs = _channel_major(cache_s).reshape(n_phys, 2, KV_HEADS, HEAD_DIM, page)
    idx16 = idx[:, :, :N_SEL]
    pages = jnp.take_along_axis(page_table[:, None, :],
                                jnp.minimum(idx16, past // SEL_BLOCK - 1) // (page // SEL_BLOCK), axis=2)
    o_rows = _nsa_sample_b(idx16.reshape(-1), pages.reshape(-1), slabs,
                           to_rows(q4), to_bt(kvs_tm), to_rows(gt4), to_rows(part), past, ts, page)
    return o_rows.reshape(bsz, KV_HEADS, ts, GROUP, HEAD_DIM).transpose(2, 0, 1, 3, 4).reshape(ts * bsz, W_ATTN)


def _mix_out_kernel(x_ref, a_ref, b_ref, *refs, gated):
    if gated:
        bg_ref, w_ref, g_ref, beta_ref, o_ref = refs
        b = (b_ref[...] * bg_ref[...]).astype(BF16)
    else:
        w_ref, g_ref, beta_ref, o_ref = refs
        b = b_ref[...]
    d = _dot(a_ref[...].astype(BF16), w_ref[0:W_CONV, :]) + _dot(b, w_ref[W_CONV:W_CONV + W_ATTN, :])
    o_ref[...] = _layer_norm(DEEPNORM_ALPHA * x_ref[...] + d, g_ref[...], beta_ref[...])


def _mix_out(x2d, a2d, b2d, bg2d, w_out, ln_g, ln_b, tm):
    n = x2d.shape[0]
    row = lambda w: pl.BlockSpec((tm, w), lambda i: (i, 0))
    vec = pl.BlockSpec((1, D_MODEL), lambda i: (0, 0))
    gate = [] if bg2d is None else [bg2d]
    return pl.pallas_call(
        functools.partial(_mix_out_kernel, gated=bg2d is not None),
        grid=(n // tm,),
        in_specs=[row(D_MODEL), row(W_CONV), row(W_ATTN)] + [row(W_ATTN)] * len(gate) + [
            pl.BlockSpec((W_CONV + W_ATTN, D_MODEL), lambda i: (0, 0)), vec, vec],
        out_specs=row(D_MODEL),
        out_shape=jax.ShapeDtypeStruct((n, D_MODEL), F32),
        compiler_params=_cparams(("parallel",)),
        name="mix_out",
    )(x2d, a2d, b2d, *gate, w_out, ln_g, ln_b)


POOL_HALO = 16


def _pool_tail(x, d_groups, gate, wg_ref, sc_ref, wo_ref, g_ref, beta_ref):
    mixed = jnp.concatenate([_dot(d_groups[gi].astype(BF16), wg_ref[gi]) for gi in range(len(POOL_WINDOWS))], axis=1)
    h = (mixed * sc_ref[...] * _silu(gate)).astype(BF16)
    return _layer_norm(DEEPNORM_ALPHA * x + _dot(h, wo_ref[...]), g_ref[...], beta_ref[...])


def _pool_prompt_kernel(x0_ref, a_ref, b_ref, we_ref, g0_ref, beta0_ref, wi_ref, wg_ref, sc_ref, wo_ref, g_ref,
                        beta_ref, o_ref, tail_ref, ext_ref, *, tm):
    i = pl.program_id(1)

    @pl.when(i == 0)
    def _():
        ext_ref[0:POOL_HALO, :] = jnp.zeros((POOL_HALO, W_POOL), F32)

    d0 = _dot(a_ref[...], we_ref[0:W_CONV, :]) + _dot(b_ref[...], we_ref[W_CONV:W_CONV + W_ATTN, :])
    x = _layer_norm(DEEPNORM_ALPHA * x0_ref[...] + d0, g0_ref[...], beta0_ref[...])
    xb = x.astype(BF16)
    v = _dot(xb, wi_ref[:, 0:W_POOL])
    gate = _dot(xb, wi_ref[:, W_POOL:2 * W_POOL])
    ext_ref[POOL_HALO:POOL_HALO + tm, :] = v
    pos = i * tm + lax.broadcasted_iota(jnp.int32, (tm, 1), 0)
    d_groups = []
    for gi, w in enumerate(POOL_WINDOWS):
        c = slice(gi * POOL_GROUP_W, (gi + 1) * POOL_GROUP_W)
        win_sum = v[:, c]
        for k in range(1, w):
            win_sum = win_sum + ext_ref[POOL_HALO - k:POOL_HALO - k + tm, c]
        cnt = jnp.minimum(pos + 1, w).astype(F32)
        d_groups.append(win_sum / cnt - v[:, c])
    o_ref[...] = _pool_tail(x, d_groups, gate, wg_ref, sc_ref, wo_ref, g_ref, beta_ref)
    ext_ref[0:POOL_HALO, :] = ext_ref[tm:tm + POOL_HALO, :]
    tail_ref[0] = ext_ref[0:POOL_HALO, :]


def _pool_prompt(x2d, a2d, b2d, w_oe, ln_g0, ln_b0, w_in, w_grp, scale, w_out, ln_g, ln_b, batch, seq, tm):
    nt = seq // tm
    const = lambda a: pl.BlockSpec(a.shape, lambda b, i: (0,) * a.ndim)
    rows = lambda w: pl.BlockSpec((tm, w), lambda b, i: (b * nt + i, 0))
    row = rows(D_MODEL)
    return pl.pallas_call(
        functools.partial(_pool_prompt_kernel, tm=tm),
        grid=(batch, nt),
        in_specs=[row, rows(W_CONV), rows(W_ATTN), const(w_oe), const(ln_g0), const(ln_b0),
                  const(w_in), const(w_grp), const(scale), const(w_out), const(ln_g), const(ln_b)],
        out_specs=[row, pl.BlockSpec((1, POOL_HALO, W_POOL), lambda b, i: (b, 0, 0))],
        out_shape=[jax.ShapeDtypeStruct((batch * seq, D_MODEL), F32),
                   jax.ShapeDtypeStruct((batch, POOL_HALO, W_POOL), F32)],
        scratch_shapes=[pltpu.VMEM((tm + POOL_HALO, W_POOL), F32)],
        compiler_params=_cparams(("arbitrary", "arbitrary")),
        name="pool_prompt",
    )(x2d, a2d, b2d, w_oe, ln_g0, ln_b0, w_in, w_grp, scale, w_out, ln_g, ln_b)


def _pool_sample_kernel(x_ref, st_ref, wi_ref, wg_ref, sc_ref, wo_ref, g_ref, beta_ref, o_ref, v_ref, *,
                        ts, bsz, pos0):
    x = x_ref[...]
    xb = x.astype(BF16)
    v = _dot(xb, wi_ref[:, 0:W_POOL])
    gate = _dot(xb, wi_ref[:, W_POOL:2 * W_POOL])
    v_ref[...] = v
    n_hist = st_ref.shape[0]

    def ext(r):
        return st_ref[r] if r < n_hist else v[(r - n_hist) * bsz:(r - n_hist + 1) * bsz, :]

    d_groups = []
    for gi, w in enumerate(POOL_WINDOWS):
        c = slice(gi * POOL_GROUP_W, (gi + 1) * POOL_GROUP_W)
        per_t = []
        for t in range(ts):
            win_sum = ext(n_hist + t)[:, c]
            for k in range(1, w):
                win_sum = win_sum + ext(n_hist + t - k)[:, c]
            per_t.append(win_sum / float(min(pos0 + t + 1, w)) - ext(n_hist + t)[:, c])
        d_groups.append(jnp.concatenate(per_t, axis=0))
    o_ref[...] = _pool_tail(x, d_groups, gate, wg_ref, sc_ref, wo_ref, g_ref, beta_ref)


def _pool_sample(x_tm, state_tm, w_in, w_grp, scale, w_out, ln_g, ln_b, ts, bsz, pos0):
    return pl.pallas_call(
        functools.partial(_pool_sample_kernel, ts=ts, bsz=bsz, pos0=pos0),
        out_shape=[jax.ShapeDtypeStruct((ts * bsz, D_MODEL), F32), jax.ShapeDtypeStruct((ts * bsz, W_POOL), F32)],
        compiler_params=pltpu.CompilerParams(vmem_limit_bytes=VMEM_LIMIT),
        name="pool_sample",
    )(x_tm, state_tm, w_in, w_grp, scale, w_out, ln_g, ln_b)


def kernel(x_prompt, x_sample, cache_kv_cmp, cache_kv_sel, state_win_kv, state_conv, state_pool, page_table,
           w_in_even, w_cmp, conv_w, conv_b, conv_ln_g, conv_ln_b, w_out_even,
           w_in_odd, w_pool_grp, pool_scale, w_out_odd, ln_g, ln_b):
    batch, seq, _ = x_prompt.shape
    bsz, ts, _ = x_sample.shape
    past = page_table.shape[1] * cache_kv_cmp.shape[2]
    kv_shape = (2, KV_HEADS, HEAD_DIM)
    vec = lambda a: a.reshape(1, -1)
    assert seq >= CONV_WIDTH - 1 and seq >= POOL_MAX - 1 and seq >= WINDOW + Q_TILE and seq % K_TILE == 0

    def token_major(x_t):
        n, _, rows = x_t.shape
        return x_t.reshape((n,) + kv_shape + (rows,)).transpose(0, 4, 1, 2, 3)[None]

    w_pad = _pad_even_weights(w_in_even[0])
    cmp_lanes = min(CMP_LANES, seq, past)
    cmp_consts = _compress_consts(w_cmp[0].reshape(CMP_BLOCK, KV_W), cmp_lanes)
    w_oe = w_out_even[0].astype(BF16)
    cw, cb, cg, cbeta = conv_w[0], vec(conv_b[0]), vec(conv_ln_g[0]), vec(conv_ln_b[0])
    g0, b0 = vec(ln_g[0]), vec(ln_b[0])

    xp = x_prompt.reshape(batch * seq, D_MODEL)
    u, sg, q, bg, gt, kvc_t, kvs_t, kvw_t, kaug, vaug, kw, vwa = _proj_prompt(xp, w_pad, batch, seq, 256)
    a_out = _conv_prompt(u, sg, cw, cb, cg, cbeta, batch, seq, 256)
    n_chunks = seq // CMP_STRIDE
    n_sel = -(-seq // SEL_BLOCK)
    kcv = _compress_prompt(kvc_t, *cmp_consts)
    b_out = _nsa_prompt(q, kaug, vaug, kw, vwa, kcv.reshape(batch, 2 * KV_HEADS, HEAD_DIM, n_chunks), gt, bg,
                        _sel_matrix(n_sel, n_chunks), batch, seq, n_chunks - 1, n_sel)

    n_win_p = min(WINDOW, seq)
    kvc_p = token_major(kvc_t)
    kvs_p = token_major(kvs_t)
    win_p = token_major(kvw_t[:, :, seq - n_win_p:])
    conv_p = u.reshape(batch, seq, W_CONV)[:, seq - (CONV_WIDTH - 1):][None]

    xs = x_sample.transpose(1, 0, 2).reshape(ts * bsz, D_MODEL)
    us, sgs, qs, bgs, gts, kvcs, kvss, kvws = _proj_sample(xs, w_pad)
    conv_ext = jnp.concatenate([state_conv[0].transpose(1, 0, 2), us.reshape(ts, bsz, W_CONV)], axis=0)
    a_out_s = _conv_sample(conv_ext, sgs.reshape(ts, bsz, W_CONV), cw, cb, cg, cbeta).reshape(ts * bsz, W_CONV)
    b_raw_s = _sample_attention(qs, kvss, kvws, gts, cache_kv_cmp[0], cache_kv_sel[0], state_win_kv[0], page_table,
                                cmp_consts, bsz, ts)
    xs1 = _mix_out(xs, a_out_s, b_raw_s, bgs, w_oe, g0, b0, ts * bsz)

    to_bt = lambda a: a.reshape(ts, bsz, -1).transpose(1, 0, 2)
    kvc_s = to_bt(kvcs).reshape((1, bsz, ts) + kv_shape)
    kvs_s = to_bt(kvss).reshape((1, bsz, ts) + kv_shape)
    win_ext = jnp.concatenate([state_win_kv[0], to_bt(kvws).reshape((bsz, ts) + kv_shape)], axis=1)
    w_len = win_ext.shape[1]
    win_s = win_ext[:, w_len - min(WINDOW, w_len):][None]
    conv_s = conv_ext[-(CONV_WIDTH - 1):].transpose(1, 0, 2)[None]

    wi_o = w_in_odd[0].astype(BF16)
    wg_o = w_pool_grp[0].astype(BF16)
    wo_o = w_out_odd[0].astype(BF16)
    sc_o = vec(pool_scale[0])
    g1, b1 = vec(ln_g[1]), vec(ln_b[1])
    n_keep = POOL_MAX - 1
    yp, tail_p = _pool_prompt(xp, a_out, b_out, w_oe, g0, b0, wi_o, wg_o, sc_o, wo_o, g1, b1, batch, seq, 256)
    pool_p = tail_p[:, POOL_HALO - n_keep:][None]
    st_tm = state_pool[0].transpose(1, 0, 2)
    ys, vs_new = _pool_sample(xs1, st_tm, wi_o, wg_o, sc_o, wo_o, g1, b1, ts, bsz, past)
    pool_ext = jnp.concatenate([st_tm, vs_new.reshape(ts, bsz, W_POOL)], axis=0)
    pool_s = pool_ext[-n_keep:].transpose(1, 0, 2)[None]

    y_prompt = yp.reshape(batch, seq, D_MODEL)
    y_sample = ys.reshape(ts, bsz, D_MODEL).transpose(1, 0, 2)
    return (y_prompt, y_sample, kvc_p, kvs_p, win_p, conv_p, pool_p, kvc_s, kvs_s, win_s, conv_s, pool_s)
```

```python
import functools

import jax
import jax.numpy as jnp
import numpy as np
from jax import lax
from jax.experimental import pallas as pl
from jax.experimental.pallas import tpu as pltpu

F32 = jnp.float32
BF16 = jnp.bfloat16

D_MODEL = 1024
W_CONV = 512
CONV_WIDTH = 31
N_HEADS = 8
HEAD_DIM = 64
KV_HEADS = 2
GROUP = N_HEADS // KV_HEADS
W_ATTN = N_HEADS * HEAD_DIM
KV_W = 2 * KV_HEADS * HEAD_DIM
V_OFF = KV_HEADS * HEAD_DIM
CMP_STRIDE = 16
CMP_BLOCK = 2 * CMP_STRIDE
SEL_BLOCK = 64
N_SEL = 16
WINDOW = 512
W_POOL = 1024
POOL_WINDOWS = (2, 4, 8, 16)
POOL_GROUP_W = W_POOL // len(POOL_WINDOWS)
POOL_MAX = 16
LN_EPS = 1e-5
NEG_INF = -1e30
SEL_FORCE = 1e9
DEPTH = 2
DEEPNORM_ALPHA = (2 * DEPTH) ** 0.25
EVEN_SPLITS = (W_CONV, W_CONV, W_CONV, W_ATTN, KV_W, KV_W, KV_W, 3 * N_HEADS, W_ATTN)

LANES = 128
SUBLANES = 8
E_PAD = 3 * W_CONV + 2 * W_ATTN + 3 * KV_W + KV_HEADS * LANES
VMEM_LIMIT = 56 * 1024 * 1024
Q_TILE = 256
K_TILE = 512
TILES_PER_TRIP = 4
CMP_LANES = 2048
CMP_PIECE = 256
PER_SEL = SEL_BLOCK // CMP_STRIDE
LOWEST = -(2.0 ** 127)
Q_SCALE = HEAD_DIM ** -0.5 * float(np.log2(np.e))
MASKED_ROW_FLOOR = 0.5 * NEG_INF


def _cparams(sem):
    return pltpu.CompilerParams(dimension_semantics=sem, vmem_limit_bytes=VMEM_LIMIT)


def _round_up(x, m):
    return m * (-(-x // m))


def _sigmoid(x):
    return 1.0 / (1.0 + jnp.exp(-x))


def _silu(x):
    return x * _sigmoid(x)


def _layer_norm(z, g, b):
    mu = jnp.mean(z, axis=-1, keepdims=True)
    zc = z - mu
    var = jnp.mean(zc * zc, axis=-1, keepdims=True)
    return zc * lax.rsqrt(var + LN_EPS) * g + b


def _dot_t(a, b):
    return lax.dot_general(a, b, (((1,), (1,)), ((), ())), preferred_element_type=F32)


def _dot(a, b):
    return jnp.dot(a, b, preferred_element_type=F32)


def _split3(x):
    hi = x.astype(BF16)
    r1 = x - hi.astype(F32)
    mid = r1.astype(BF16)
    lo = (r1 - mid.astype(F32)).astype(BF16)
    return hi, mid, lo


KV_COL0 = 3 * W_CONV + W_ATTN


def _proj_common(xb, w_ref, u_ref, sg_ref, q_ref, bg_ref, gt_ref):
    def mm(lo, hi):
        return _dot(xb, w_ref[:, lo:hi])

    o = 0
    a_val = mm(o, o + W_CONV); o += W_CONV
    a_glu = mm(o, o + W_CONV); o += W_CONV
    u_ref[...] = a_val * _sigmoid(a_glu)
    sg_ref[...] = _silu(mm(o, o + W_CONV)); o += W_CONV
    q_ref[...] = (mm(o, o + W_ATTN) * Q_SCALE).astype(BF16); o += W_ATTN
    o += 3 * KV_W
    bg_ref[...] = _silu(mm(o, o + W_ATTN)); o += W_ATTN
    gt_ref[...] = _sigmoid(mm(o, o + KV_HEADS * LANES))


def _proj_sample_kernel(x_ref, w_ref, u_ref, sg_ref, q_ref, bg_ref, gt_ref, kvc_ref, kvs_ref, kvw_ref):
    xb = x_ref[...].astype(BF16)
    _proj_common(xb, w_ref, u_ref, sg_ref, q_ref, bg_ref, gt_ref)
    for k, ref in enumerate((kvc_ref, kvs_ref, kvw_ref)):
        ref[...] = _dot(xb, w_ref[:, KV_COL0 + k * KV_W:KV_COL0 + (k + 1) * KV_W])


def _proj_prompt_kernel(x_ref, w_ref, wkv_t_ref, u_ref, sg_ref, q_ref, bg_ref, gt_ref, kvc_t_ref, kvs_t_ref,
                        kvw_t_ref, kaug_ref, vaug_ref, kw_ref, vwa_ref, *, tm, nt, n_sel):
    xb = x_ref[...].astype(BF16)
    _proj_common(xb, w_ref, u_ref, sg_ref, q_ref, bg_ref, gt_ref)
    kvc_t_ref[0] = _dot_t(wkv_t_ref[0:KV_W, :], xb)
    ks_t = _dot_t(wkv_t_ref[KV_W:2 * KV_W, :], xb)
    kvs_t_ref[0] = ks_t
    kw_t = _dot_t(wkv_t_ref[2 * KV_W:3 * KV_W, :], xb)
    kvw_t_ref[0] = kw_t
    key = (pl.program_id(0) % nt) * tm + lax.broadcasted_iota(jnp.int32, (n_sel, tm), 1)
    blk = lax.broadcasted_iota(jnp.int32, (n_sel, tm), 0)
    onehot = jnp.where(key // SEL_BLOCK == blk, NEG_INF, 0.0).astype(BF16)
    ones_row = jnp.where(lax.broadcasted_iota(jnp.int32, (LANES - HEAD_DIM, tm), 0) == 0, 1.0, 0.0).astype(BF16)
    n_zero = kaug_ref.shape[2] - n_sel - HEAD_DIM
    for h in range(KV_HEADS):
        k_rows = slice(h * HEAD_DIM, (h + 1) * HEAD_DIM)
        v_rows = slice(V_OFF + h * HEAD_DIM, V_OFF + (h + 1) * HEAD_DIM)
        kaug_ref[0, h, 0:n_sel, :] = onehot
        kaug_ref[0, h, n_sel:n_sel + HEAD_DIM, :] = ks_t[k_rows].astype(BF16)
        if n_zero:
            kaug_ref[0, h, n_sel + HEAD_DIM:, :] = jnp.zeros((n_zero, tm), BF16)
        vaug_ref[0, h, 0:HEAD_DIM, :] = ks_t[v_rows].astype(BF16)
        vaug_ref[0, h, HEAD_DIM:, :] = ones_row
        kw_ref[0, h] = kw_t[k_rows].astype(BF16)
        vwa_ref[0, h, 0:HEAD_DIM, :] = kw_t[v_rows].astype(BF16)
        vwa_ref[0, h, HEAD_DIM:, :] = ones_row


def _proj_row_specs(n, tm):
    row = lambda w: pl.BlockSpec((tm, w), lambda i: (i, 0))
    widths = (W_CONV, W_CONV, W_ATTN, W_ATTN, KV_HEADS * LANES)
    dtypes = (F32, F32, BF16, F32, F32)
    return row, [row(w) for w in widths], [jax.ShapeDtypeStruct((n, w), d) for w, d in zip(widths, dtypes)]


def _proj_sample(x2d, w_pad):
    n = x2d.shape[0]
    row, specs, shapes = _proj_row_specs(n, n)
    return pl.pallas_call(
        _proj_sample_kernel,
        grid=(1,),
        in_specs=[row(D_MODEL), pl.BlockSpec((D_MODEL, E_PAD), lambda i: (0, 0))],
        out_specs=specs + [row(KV_W)] * 3,
        out_shape=shapes + [jax.ShapeDtypeStruct((n, KV_W), F32)] * 3,
        compiler_params=_cparams(("arbitrary",)),
        name="proj_sample",
    )(x2d, w_pad)


def _proj_prompt(x2d, w_pad, batch, seq, tm):
    n = x2d.shape[0]
    wkv_t = w_pad[:, KV_COL0:KV_COL0 + 3 * KV_W].T
    nt = seq // tm
    n_sel = -(-seq // SEL_BLOCK)
    k_rows = _round_up(n_sel + HEAD_DIM, LANES)
    row, specs, shapes = _proj_row_specs(n, tm)
    chan = pl.BlockSpec((1, KV_W, tm), lambda i: (i // nt, 0, i % nt))
    head = lambda r: pl.BlockSpec((1, KV_HEADS, r, tm), lambda i: (i // nt, 0, 0, i % nt))
    hshape = lambda r: jax.ShapeDtypeStruct((batch, KV_HEADS, r, seq), BF16)
    return pl.pallas_call(
        functools.partial(_proj_prompt_kernel, tm=tm, nt=nt, n_sel=n_sel),
        grid=(n // tm,),
        in_specs=[row(D_MODEL), pl.BlockSpec((D_MODEL, E_PAD), lambda i: (0, 0)),
                  pl.BlockSpec((3 * KV_W, D_MODEL), lambda i: (0, 0))],
        out_specs=specs + [chan] * 3 + [head(k_rows), head(LANES), head(HEAD_DIM), head(LANES)],
        out_shape=shapes + [jax.ShapeDtypeStruct((batch, KV_W, seq), F32)] * 3 + [
            hshape(k_rows), hshape(LANES), hshape(HEAD_DIM), hshape(LANES)],
        compiler_params=_cparams(("parallel",)),
        name="proj_prompt",
    )(x2d, w_pad, wkv_t)


def _pad_even_weights(w):
    offs = np.cumsum(EVEN_SPLITS)[:-1].tolist()
    a_val, a_glu, a_gate, wq, wkc, wks, wkw, wg, wbg = jnp.split(w, offs, axis=1)
    wg = wg.reshape(D_MODEL, KV_HEADS, GROUP, 3).transpose(0, 1, 3, 2).reshape(D_MODEL, KV_HEADS, 3 * GROUP)
    wg = jnp.pad(wg, ((0, 0), (0, 0), (0, LANES - 3 * GROUP))).reshape(D_MODEL, KV_HEADS * LANES)
    return jnp.concatenate([a_val, a_glu, a_gate, wq, wkc, wks, wkw, wbg, wg], axis=1).astype(BF16)


CONV_HALO = 32
CONV_CHUNK = 32


def _conv_prompt_kernel(u_ref, sg_ref, w_ref, cb_ref, g_ref, b_ref, o_ref, ext_ref, sh_ref, *, tt):
    @pl.when(pl.program_id(1) == 0)
    def _():
        ext_ref[0:CONV_HALO, :] = jnp.zeros((CONV_HALO, W_CONV), F32)

    ext_ref[CONV_HALO:CONV_HALO + tt, :] = u_ref[...]
    base = CONV_HALO - (CONV_WIDTH - 1)
    n_sh = sh_ref.shape[1]
    for s in range(1, SUBLANES):
        sh_ref[s - 1] = ext_ref[s:s + n_sh, :]

    def window(start):
        s, off = start % SUBLANES, start - start % SUBLANES
        return ext_ref[off:off + CONV_CHUNK, :] if s == 0 else sh_ref[s - 1, off:off + CONV_CHUNK, :]

    for c in range(tt // CONV_CHUNK):
        r0 = c * CONV_CHUNK
        acc = jnp.zeros((CONV_CHUNK, W_CONV), F32) + cb_ref[...]
        for k in range(CONV_WIDTH):
            acc = acc + w_ref[k:k + 1, :] * window(base + r0 + k)
        y = _layer_norm(acc, g_ref[...], b_ref[...])
        o_ref[r0:r0 + CONV_CHUNK, :] = (_silu(y) * sg_ref[r0:r0 + CONV_CHUNK, :]).astype(o_ref.dtype)
    ext_ref[0:CONV_HALO, :] = ext_ref[tt:tt + CONV_HALO, :]


def _conv_prompt(u2d, sg2d, conv_w, conv_b, ln_g, ln_b, batch, seq, tt):
    nt = seq // tt
    row = pl.BlockSpec((tt, W_CONV), lambda b, i: (b * nt + i, 0))
    vec = pl.BlockSpec((1, W_CONV), lambda b, i: (0, 0))
    return pl.pallas_call(
        functools.partial(_conv_prompt_kernel, tt=tt),
        grid=(batch, nt),
        in_specs=[row, row, pl.BlockSpec((CONV_WIDTH, W_CONV), lambda b, i: (0, 0)), vec, vec, vec],
        out_specs=row,
        out_shape=jax.ShapeDtypeStruct((batch * seq, W_CONV), BF16),
        scratch_shapes=[pltpu.VMEM((tt + CONV_HALO, W_CONV), F32),
                        pltpu.VMEM((SUBLANES - 1, tt + CONV_HALO - SUBLANES, W_CONV), F32)],
        compiler_params=_cparams(("arbitrary", "arbitrary")),
        name="conv_prompt",
    )(u2d, sg2d, conv_w, conv_b, ln_g, ln_b)


def _conv_sample_kernel(ext_ref, sg_ref, w_ref, cb_ref, g_ref, b_ref, o_ref, *, ts):
    for t in range(ts):
        acc = jnp.zeros(ext_ref.shape[1:], F32) + cb_ref[...]
        for k in range(CONV_WIDTH):
            acc = acc + w_ref[k:k + 1, :] * ext_ref[t + k]
        y = _layer_norm(acc, g_ref[...], b_ref[...])
        o_ref[t] = _silu(y) * sg_ref[t]


def _conv_sample(ext, sg, conv_w, conv_b, ln_g, ln_b):
    ts, bsz, _ = sg.shape
    return pl.pallas_call(
        functools.partial(_conv_sample_kernel, ts=ts),
        out_shape=jax.ShapeDtypeStruct((ts, bsz, W_CONV), F32),
        name="conv_sample",
    )(ext, sg, conv_w, conv_b, ln_g, ln_b)


def _compress_span(x_ext, w1_ref, w2_ref, s_ref):
    lanes = x_ext.shape[1] - LANES
    piece = min(lanes, CMP_PIECE)
    reps = piece // LANES
    out = None
    for c in range(lanes // piece):
        xp = x_ext[:, c * piece:(c + 1) * piece + LANES]
        later = pltpu.roll(xp, piece + LANES - CMP_STRIDE, 1)[:, 0:piece]
        z = xp[:, 0:piece] * jnp.tile(w1_ref[...], (1, reps)) + later * jnp.tile(w2_ref[...], (1, reps))
        hi = z.astype(BF16)
        lo = (z - hi.astype(F32)).astype(BF16)
        seg = s_ref[c * piece:(c + 1) * piece, :]
        part = _dot(hi, seg) + _dot(lo, seg)
        out = part if out is None else out + part
    return out


def _compress_prompt_kernel(x_ref, w1_ref, w2_ref, s_ref, o_ref, *, lanes):
    seq = x_ref.shape[2]
    parts = []
    for c in range(seq // lanes):
        if (c + 1) * lanes + LANES <= seq:
            x_ext = x_ref[0, :, c * lanes:(c + 1) * lanes + LANES]
        else:
            x_ext = jnp.concatenate([x_ref[0, :, c * lanes:(c + 1) * lanes], jnp.zeros((KV_W, LANES), F32)], axis=1)
        parts.append(_compress_span(x_ext, w1_ref, w2_ref, s_ref))
    o_ref[0] = jnp.concatenate(parts, axis=1)


def _compress_prompt(kvc_t, w1t, w2t, seg):
    batch, _, seq = kvc_t.shape
    lanes = seg.shape[0]
    const = lambda a: pl.BlockSpec(a.shape, lambda b: (0, 0))
    return pl.pallas_call(
        functools.partial(_compress_prompt_kernel, lanes=lanes),
        grid=(batch,),
        in_specs=[pl.BlockSpec((1, KV_W, seq), lambda b: (b, 0, 0)), const(w1t), const(w2t), const(seg)],
        out_specs=pl.BlockSpec((1, KV_W, seq // CMP_STRIDE), lambda b: (b, 0, 0)),
        out_shape=jax.ShapeDtypeStruct((batch, KV_W, seq // CMP_STRIDE), F32),
        compiler_params=_cparams(("parallel",)),
        name="compress_prompt",
    )(kvc_t, w1t, w2t, seg)


def _compress_paged_kernel(pt_ref, *refs, n_pages_step):
    del pt_ref
    pages = refs[:n_pages_step + 1]
    w1_ref, w2_ref, s_ref, o_ref = refs[n_pages_step + 1:]
    x_ext = jnp.concatenate([p[0] for p in pages[:-1]] + [pages[-1][0, :, 0:LANES]], axis=1)
    o_ref[0] = _compress_span(x_ext, w1_ref, w2_ref, s_ref)


def _compress_paged(cache_t, page_table, w1t, w2t, seg):
    _, _, page = cache_t.shape
    bsz, n_pages = page_table.shape
    lanes = seg.shape[0]
    pps = lanes // page
    assert page >= LANES and lanes % page == 0 and n_pages % pps == 0
    n_steps = n_pages // pps
    n_out = lanes // CMP_STRIDE

    pt_ext = jnp.concatenate([page_table, page_table[:, -1:]], axis=1).reshape(-1)

    def page_spec(k):
        return pl.BlockSpec((1, KV_W, page), lambda b, s, pt: (pt[b * (n_pages + 1) + s * pps + k], 0, 0))

    const = lambda a: pl.BlockSpec(a.shape, lambda b, s, pt: (0, 0))
    grid_spec = pltpu.PrefetchScalarGridSpec(
        num_scalar_prefetch=1,
        grid=(bsz, n_steps),
        in_specs=[page_spec(k) for k in range(pps + 1)] + [const(w1t), const(w2t), const(seg)],
        out_specs=pl.BlockSpec((1, KV_W, n_out), lambda b, s, pt: (b, 0, s)),
    )
    return pl.pallas_call(
        functools.partial(_compress_paged_kernel, n_pages_step=pps),
        grid_spec=grid_spec,
        out_shape=jax.ShapeDtypeStruct((bsz, KV_W, n_steps * n_out), F32),
        compiler_params=_cparams(("parallel", "arbitrary")),
        name="compress_paged",
    )(pt_ext, *([cache_t] * (pps + 1)), w1t, w2t, seg)


def _compress_consts(w_cmp2d, lanes):
    w1t = jnp.tile(w_cmp2d[:CMP_STRIDE].T, (1, LANES // CMP_STRIDE))
    w2t = jnp.tile(w_cmp2d[CMP_STRIDE:].T, (1, LANES // CMP_STRIDE))
    seg = jnp.asarray(np.arange(lanes)[:, None] // CMP_STRIDE == np.arange(lanes // CMP_STRIDE)[None, :], BF16)
    return w1t, w2t, seg


N_FORCED = 3


def _pick_top_block(sc, blk):
    mx = jnp.max(sc, axis=0, keepdims=True)
    jm = jnp.min(jnp.where(sc == mx, blk, sc.shape[0]), axis=0, keepdims=True)
    return jnp.where(blk == jm, LOWEST, sc)


def _force_scores(score, blk, cur, forced_value=SEL_FORCE):
    forced = (blk == 0) | ((blk >= cur - 1) & (blk <= cur))
    return jnp.where(forced, forced_value, jnp.where(blk <= cur, score, -SEL_FORCE))


def _sel_matrix(n_sel_blocks, n_cols):
    n = np.arange(n_cols)
    m = (n[None, :] // PER_SEL == np.arange(n_sel_blocks)[:, None]) & (n[None, :] % PER_SEL < PER_SEL - 1)
    return jnp.asarray(m, BF16)


def _cmp_valid(n, qpos, n_cmp):
    return n * CMP_STRIDE + (CMP_BLOCK - 1) <= jnp.minimum(qpos, (n_cmp - 1) * CMP_STRIDE + CMP_BLOCK - 1)


def _nsa_prompt_kernel(q_ref, kaug_ref, vaug_ref, kw_ref, vwa_ref, kc_ref, vc_ref, gt_ref, bg_ref, m01_ref, o_ref,
                       qa_ref, acc_ref, mx_ref, s_ref, oc_ref, score_ref, *, n_cmp, n_sel):
    i = pl.program_id(2)
    rows = GROUP * Q_TILE
    q = q_ref[...]
    qs = jnp.concatenate([q[:, g * HEAD_DIM:(g + 1) * HEAD_DIM] for g in range(GROUP)], axis=0)
    n_cols = kc_ref.shape[3]

    def row_pos(width):
        return i * Q_TILE + (lax.broadcasted_iota(jnp.int32, (rows, width), 0) & (Q_TILE - 1))

    def compressed_branch(width):
        tpos = i * Q_TILE + lax.broadcasted_iota(jnp.int32, (Q_TILE, width), 0)
        c_bias = jnp.where(_cmp_valid(lax.broadcasted_iota(jnp.int32, (Q_TILE, width), 1), tpos, n_cmp),
                           0.0, NEG_INF)
        s_all = _dot(qs, kc_ref[0, 0, :, 0:width].astype(BF16))
        p_parts = []
        p_grp = None
        for g in range(GROUP):
            s = s_all[g * Q_TILE:(g + 1) * Q_TILE] + c_bias
            e = jnp.exp2(s - jnp.maximum(jnp.max(s, axis=1, keepdims=True), MASKED_ROW_FLOOR))
            l = jnp.sum(e, axis=1, keepdims=True)
            p = e * (1.0 / jnp.where(l > 0.0, l, 1.0))
            p_parts.append(p.astype(BF16))
            p_grp = p if p_grp is None else p_grp + p
        oc_ref[...] = _dot_t(jnp.concatenate(p_parts, axis=0), vc_ref[0, 0, :, 0:width].astype(BF16))
        score_ref[...] = sum(_dot_t(m01_ref[:, 0:width], part) for part in _split3(p_grp))

    half = n_cols // 2
    if half % LANES == 0:
        last_visible = (i * Q_TILE + Q_TILE - 1 - (CMP_BLOCK - 1)) // CMP_STRIDE

        @pl.when(last_visible < half)
        def _():
            compressed_branch(half)

        @pl.when(last_visible >= half)
        def _():
            compressed_branch(n_cols)
    else:
        compressed_branch(n_cols)
    o_c = oc_ref[...]
    score_t = score_ref[...]
    blk = lax.broadcasted_iota(jnp.int32, (n_sel, Q_TILE), 0)
    tok = i * Q_TILE + lax.broadcasted_iota(jnp.int32, (n_sel, Q_TILE), 1)
    sc = _force_scores(score_t, blk, tok // SEL_BLOCK, LOWEST)
    chains = [sc[:, c * LANES:(c + 1) * LANES] for c in range(Q_TILE // LANES)]
    blk_c = blk[:, 0:LANES]
    n_pick = max(min(N_SEL, n_sel) - N_FORCED, 0)

    w_keys = WINDOW + Q_TILE
    w0 = pl.multiple_of(jnp.maximum(i * Q_TILE - WINDOW, 0), Q_TILE)
    rel = (i * Q_TILE + lax.broadcasted_iota(jnp.int32, (Q_TILE, w_keys), 0)
           - (w0 + lax.broadcasted_iota(jnp.int32, (Q_TILE, w_keys), 1)))
    w_bias = jnp.where((rel >= 0) & (rel < WINDOW), 0.0, NEG_INF)
    s_w_all = _dot(qs, kw_ref[0, 0, :, pl.ds(w0, w_keys)])
    e_w_parts = []
    done = 0
    for g in range(GROUP):
        s_w = s_w_all[g * Q_TILE:(g + 1) * Q_TILE] + w_bias
        e_w_parts.append(jnp.exp2(s_w - jnp.max(s_w, axis=1, keepdims=True)).astype(BF16))
        upto = n_pick * (g + 1) // GROUP
        for _ in range(done, upto):
            chains = [_pick_top_block(c, blk_c) for c in chains]
        done = upto
    acc_w = _dot_t(jnp.concatenate(e_w_parts, axis=0), vwa_ref[0, 0, :, pl.ds(w0, w_keys)])
    o_w = acc_w[:, 0:HEAD_DIM] / acc_w[:, HEAD_DIM:HEAD_DIM + 1]
    not_sel_t = jnp.where(jnp.concatenate(chains, axis=1) == LOWEST, 0.0, 1.0)
    not_sel = not_sel_t.T.astype(BF16)
    for g in range(GROUP):
        qa_ref[g * Q_TILE:(g + 1) * Q_TILE, 0:n_sel] = not_sel
    qa_ref[:, n_sel:n_sel + HEAD_DIM] = qs
    if qa_ref.shape[1] > n_sel + HEAD_DIM:
        qa_ref[:, n_sel + HEAD_DIM:] = jnp.zeros((rows, qa_ref.shape[1] - n_sel - HEAD_DIM), BF16)

    mx_ref[...] = jnp.full(mx_ref.shape, NEG_INF, F32)

    def score_tiles(j, n_tiles, causal_last):
        for t in range(n_tiles):
            k0 = pl.multiple_of((j + t) * K_TILE, K_TILE)
            s = _dot(qa_ref[...], kaug_ref[0, 0, :, pl.ds(k0, K_TILE)])
            if causal_last and t == n_tiles - 1:
                kpos = k0 + lax.broadcasted_iota(jnp.int32, (rows, K_TILE), 1)
                s = jnp.where(kpos <= row_pos(K_TILE), s, NEG_INF)
            s_ref[j + t] = s
            part = s[:, 0:LANES]
            for c in range(1, K_TILE // LANES):
                part = jnp.maximum(part, s[:, c * LANES:(c + 1) * LANES])
            mx_ref[...] = jnp.maximum(mx_ref[...], part)

    def acc_tiles(j, n_tiles):
        m_b = jnp.tile(mx_ref[...], (1, K_TILE // LANES))
        for t in range(n_tiles):
            k0 = pl.multiple_of((j + t) * K_TILE, K_TILE)
            pe = jnp.exp2(s_ref[j + t] - m_b).astype(BF16)
            acc_ref[...] += _dot_t(pe, vaug_ref[0, 0, :, pl.ds(k0, K_TILE)])

    def grouped(n, fn, tail_fn, tail_min):
        def body(jj, carry):
            fn(TILES_PER_TRIP * jj, TILES_PER_TRIP)
            return carry
        lax.fori_loop(0, n // TILES_PER_TRIP, body, 0)
        for r in range(tail_min, TILES_PER_TRIP):
            @pl.when(n % TILES_PER_TRIP == r)
            def _(r=r):
                tail_fn(n - r, r)

    n_full = (i * Q_TILE) // K_TILE
    grouped(n_full, lambda j, n: score_tiles(j, n, False), lambda j, r: score_tiles(j, r + 1, True), 0)

    mx_ref[...] = jnp.broadcast_to(jnp.max(mx_ref[...], axis=1, keepdims=True), mx_ref.shape)
    acc_ref[...] = jnp.zeros(acc_ref.shape, F32)
    grouped(n_full + 1, acc_tiles, acc_tiles, 1)

    acc = acc_ref[...]
    o_s = acc[:, 0:HEAD_DIM] / acc[:, HEAD_DIM:HEAD_DIM + 1]

    gt = gt_ref[...]
    outs = []
    for g in range(GROUP):
        r = slice(g * Q_TILE, (g + 1) * Q_TILE)
        outs.append(gt[:, g:g + 1] * o_c[r] + gt[:, GROUP + g:GROUP + g + 1] * o_s[r]
                    + gt[:, 2 * GROUP + g:2 * GROUP + g + 1] * o_w[r])
    o_ref[...] = (jnp.concatenate(outs, axis=1) * bg_ref[...]).astype(o_ref.dtype)


def _nsa_prompt(q2d, kaug, vaug, kw, vwa, kcv4, gt2d, bg2d, m01, batch, seq, n_cmp, n_sel):
    nq = seq // Q_TILE
    rows = GROUP * Q_TILE
    n_cols = kcv4.shape[-1]
    per_head = lambda a: pl.BlockSpec((1, 1) + a.shape[2:], lambda b, h, i: (b, h, 0, 0),
                                      pipeline_mode=pl.Buffered(1))
    return pl.pallas_call(
        functools.partial(_nsa_prompt_kernel, n_cmp=n_cmp, n_sel=n_sel),
        grid=(batch, KV_HEADS, nq),
        in_specs=[pl.BlockSpec((Q_TILE, GROUP * HEAD_DIM), lambda b, h, i: (b * nq + i, h)),
                  per_head(kaug), per_head(vaug), per_head(kw), per_head(vwa),
                  pl.BlockSpec((1, 1, HEAD_DIM, n_cols), lambda b, h, i: (b, h, 0, 0)),
                  pl.BlockSpec((1, 1, HEAD_DIM, n_cols), lambda b, h, i: (b, KV_HEADS + h, 0, 0)),
                  pl.BlockSpec((Q_TILE, LANES), lambda b, h, i: (b * nq + i, h)),
                  pl.BlockSpec((Q_TILE, GROUP * HEAD_DIM), lambda b, h, i: (b * nq + i, h)),
                  pl.BlockSpec(m01.shape, lambda b, h, i: (0, 0))],
        out_specs=pl.BlockSpec((Q_TILE, GROUP * HEAD_DIM), lambda b, h, i: (b * nq + i, h)),
        out_shape=jax.ShapeDtypeStruct((batch * seq, W_ATTN), BF16),
        scratch_shapes=[pltpu.VMEM((rows, kaug.shape[2]), BF16), pltpu.VMEM((rows, LANES), F32),
                        pltpu.VMEM((rows, LANES), F32), pltpu.VMEM((seq // K_TILE, rows, K_TILE), F32),
                        pltpu.VMEM((rows, HEAD_DIM), F32), pltpu.VMEM((n_sel, Q_TILE), F32)],
        compiler_params=_cparams(("parallel", "parallel", "arbitrary")),
        name="nsa_prompt",
    )(q2d, kaug, vaug, kw, vwa, kcv4, kcv4, gt2d, bg2d, m01)


def _new_token_scores(qf, nk, tq, ts):
    cols = []
    for t in range(ts):
        s = jnp.sum(qf * nk[t:t + 1, :], axis=1, keepdims=True)
        cols.append(jnp.where(tq >= t, s, NEG_INF))
    return cols


def _bf16_round(x):
    return x.astype(BF16).astype(F32)


def _nsa_sample_a_kernel(q_ref, kcv_ref, win_ref, new_ref, gt_ref, m01_ref, part_ref, idx_ref, *,
                         past, ts, n_cmp):
    for bi in range(q_ref.shape[0]):
        _nsa_sample_a_one(bi, q_ref, kcv_ref, win_ref, new_ref, gt_ref, m01_ref, part_ref, idx_ref,
                          past=past, ts=ts, n_cmp=n_cmp)


def _nsa_sample_a_one(bi, q_ref, kcv_ref, win_ref, new_ref, gt_ref, m01_ref, part_ref, idx_ref, *,
                      past, ts, n_cmp):
    rows = GROUP * ts
    kcv = kcv_ref[bi]
    win = win_ref[bi]
    new = new_ref[bi]
    n_cols = kcv.shape[1]
    w_hist = win.shape[1]
    n_blk_pad = m01_ref.shape[0]
    tq = lax.broadcasted_iota(jnp.int32, (rows, 1), 0) % ts
    qpos = past + tq
    scores = []
    for h in range(KV_HEADS):
        qh = q_ref[bi, h]
        ksl = slice(h * HEAD_DIM, (h + 1) * HEAD_DIM)
        vsl = slice(V_OFF + h * HEAD_DIM, V_OFF + (h + 1) * HEAD_DIM)
        s = _dot(qh, kcv[ksl].astype(BF16))
        valid = _cmp_valid(lax.broadcasted_iota(jnp.int32, (rows, n_cols), 1), qpos, n_cmp)
        s = jnp.where(valid, s, NEG_INF)
        e = jnp.where(valid, jnp.exp2(s - jnp.max(s, axis=1, keepdims=True)), 0.0)
        l = jnp.sum(e, axis=1, keepdims=True)
        p = e / jnp.where(l > 0.0, l, 1.0)
        o_c = _dot_t(p.astype(BF16), kcv[vsl].astype(BF16))
        p_grp = p[0:ts]
        for g in range(1, GROUP):
            p_grp = p_grp + p[g * ts:(g + 1) * ts]
        scores.append(sum(_dot_t(part, m01_ref[...]) for part in _split3(p_grp)))

        s1 = _dot(qh, win[ksl].astype(BF16))
        rel = qpos - (past - w_hist + lax.broadcasted_iota(jnp.int32, (rows, w_hist), 1))
        valid = (rel >= 0) & (rel <= jnp.minimum(qpos, WINDOW - 1))
        s1 = jnp.where(valid, s1, NEG_INF)
        s2 = _new_token_scores(qh.astype(F32), _bf16_round(new[:, ksl]), tq, ts)
        m = jnp.max(s1, axis=1, keepdims=True)
        for c in s2:
            m = jnp.maximum(m, c)
        e1 = jnp.where(valid, jnp.exp2(s1 - m), 0.0)
        l = jnp.sum(e1, axis=1, keepdims=True)
        o_w = _dot_t(e1.astype(BF16), win[vsl].astype(BF16))
        nv = _bf16_round(new[:, vsl])
        for t in range(ts):
            e2 = jnp.exp2(s2[t] - m)
            l = l + e2
            o_w = o_w + _bf16_round(e2) * nv[t:t + 1, :]
        gt = gt_ref[bi, h]
        part_ref[bi, h] = gt[:, 0:1] * o_c + gt[:, 2:3] * (o_w / l)

    sc = jnp.concatenate(scores, axis=0)
    nr = KV_HEADS * ts
    blk = lax.broadcasted_iota(jnp.int32, (nr, n_blk_pad), 1)
    cur = (past + (lax.broadcasted_iota(jnp.int32, (nr, 1), 0) % ts)) // SEL_BLOCK
    sc = _force_scores(sc, blk, cur)
    lane = lax.broadcasted_iota(jnp.int32, (nr, LANES), 1)
    idx = jnp.zeros((nr, LANES), jnp.int32)
    for k in range(N_SEL):
        mx = jnp.max(sc, axis=1, keepdims=True)
        jm = jnp.min(jnp.where(sc == mx, blk, n_blk_pad), axis=1, keepdims=True)
        idx = jnp.where(lane == k, jm, idx)
        sc = jnp.where(blk == jm, LOWEST, sc)
    idx_ref[bi] = idx


def _nsa_sample_a(q4, kcv_t, win_t, new_w, gt4, m01, past, ts, n_cmp):
    bsz = q4.shape[0]
    rows = GROUP * ts
    nb = max(d for d in (4, 2, 1) if bsz % d == 0)
    per_b = lambda a: pl.BlockSpec((nb,) + a.shape[1:], lambda b: (b,) + (0,) * (a.ndim - 1))
    return pl.pallas_call(
        functools.partial(_nsa_sample_a_kernel, past=past, ts=ts, n_cmp=n_cmp),
        grid=(bsz // nb,),
        in_specs=[per_b(q4), per_b(kcv_t), per_b(win_t), per_b(new_w), per_b(gt4),
                  pl.BlockSpec(m01.shape, lambda b: (0, 0))],
        out_specs=[pl.BlockSpec((nb, KV_HEADS, rows, HEAD_DIM), lambda b: (b, 0, 0, 0)),
                   pl.BlockSpec((nb, KV_HEADS * ts, LANES), lambda b: (b, 0, 0))],
        out_shape=[jax.ShapeDtypeStruct((bsz, KV_HEADS, rows, HEAD_DIM), F32),
                   jax.ShapeDtypeStruct((bsz, KV_HEADS * ts, LANES), jnp.int32)],
        compiler_params=_cparams(("parallel",)),
        name="nsa_sample_scores",
    )(q4, kcv_t, win_t, new_w, gt4, m01)


def _nsa_sample_b_kernel(idx_ref, pt_ref, *refs, past, ts, n_past_blocks, page):
    del pt_ref
    kv_blocks = refs[:N_SEL]
    q_ref, new_ref, gt_ref, part_ref, o_ref = refs[N_SEL:]
    r = pl.program_id(0)
    t = r % ts
    head0 = ((r // ts) % KV_HEADS) == 0
    bpp = page // SEL_BLOCK
    q = q_ref[0]
    lane = lax.broadcasted_iota(jnp.int32, (GROUP, page), 1)

    def pick(x, off):
        return jnp.where(head0, x[:, off:off + HEAD_DIM], x[:, off + HEAD_DIM:off + 2 * HEAD_DIM])

    s_all = _dot(q, jnp.concatenate([kv_blocks[k][0, 0, 0].astype(BF16) for k in range(N_SEL)], axis=1))
    s_list = []
    has_new = False
    for k in range(N_SEL):
        j = idx_ref[r * N_SEL + k]
        is_past = j < n_past_blocks
        has_new = jnp.logical_or(has_new, jnp.logical_not(is_past))
        off = (j % bpp) * SEL_BLOCK
        page_start = (j // bpp) * page
        lo = jnp.where(is_past, off, page)
        hi = jnp.minimum(off + SEL_BLOCK, past + t + 1 - page_start)
        s_list.append(jnp.where((lane >= lo) & (lane < hi), s_all[:, k * page:(k + 1) * page], NEG_INF))
    new = new_ref[0]
    tq = jnp.where(has_new, t, -1) + jnp.zeros((GROUP, 1), jnp.int32)
    s_new = _new_token_scores(q.astype(F32), _bf16_round(pick(new, 0)), tq, ts)
    m = s_new[0]
    for c in s_new[1:]:
        m = jnp.maximum(m, c)
    for s in s_list:
        m = jnp.maximum(m, jnp.max(s, axis=1, keepdims=True))
    nv = _bf16_round(pick(new, V_OFF))
    l = jnp.zeros((GROUP, 1), F32)
    o = jnp.zeros((GROUP, HEAD_DIM), F32)
    for tt in range(ts):
        e2 = jnp.exp2(s_new[tt] - m)
        l = l + e2
        o = o + _bf16_round(e2) * nv[tt:tt + 1, :]
    e_all = jnp.exp2(jnp.concatenate(s_list, axis=1) - m)
    l = l + jnp.sum(e_all, axis=1, keepdims=True)
    o = o + _dot_t(e_all.astype(BF16),
                   jnp.concatenate([kv_blocks[k][0, 1, 0].astype(BF16) for k in range(N_SEL)], axis=1))
    gt = gt_ref[0]
    o_ref[0] = part_ref[0] + gt[:, 1:2] * (o / l)


def _nsa_sample_b(idx_flat, page_flat, cache_slabs, q_rows, new_rows, gt_rows, part_rows, past, ts, page):
    n_rows = q_rows.shape[0]
    n_past_blocks = past // SEL_BLOCK

    def slab_spec(k):
        return pl.BlockSpec((1, 2, 1, HEAD_DIM, page),
                            lambda r, idx, pg: (pg[r * N_SEL + k], 0, (r // ts) % KV_HEADS, 0, 0))

    row3 = lambda a: pl.BlockSpec((1,) + a.shape[1:], lambda r, idx, pt: (r, 0, 0))
    grid_spec = pltpu.PrefetchScalarGridSpec(
        num_scalar_prefetch=2,
        grid=(n_rows,),
        in_specs=[slab_spec(k) for k in range(N_SEL)] + [
            row3(q_rows),
            pl.BlockSpec((1,) + new_rows.shape[1:], lambda r, idx, pt: (r // (KV_HEADS * ts), 0, 0)),
            row3(gt_rows), row3(part_rows)],
        out_specs=row3(part_rows),
    )
    return pl.pallas_call(
        functools.partial(_nsa_sample_b_kernel, past=past, ts=ts, n_past_blocks=n_past_blocks, page=page),
        grid_spec=grid_spec,
        out_shape=jax.ShapeDtypeStruct(part_rows.shape, F32),
        compiler_params=_cparams(("arbitrary",)),
        name="nsa_sample_select",
    )(idx_flat, page_flat, *([cache_slabs] * N_SEL), q_rows, new_rows, gt_rows, part_rows)


def _channel_major(x):
    n, rows = x.shape[:2]
    return x.transpose(0, 2, 3, 4, 1).reshape(n, KV_W, rows)


def _sample_attention(q_tm, kvs_tm, kvw_tm, gt_tm, cache_c, cache_s, state_win, page_table, cmp_consts, bsz, ts):
    n_phys, page = cache_c.shape[0], cache_c.shape[1]
    n_pages = page_table.shape[1]
    past = n_pages * page
    w_hist = state_win.shape[1]
    assert ts < CMP_STRIDE and page % SEL_BLOCK == 0 and ts <= WINDOW and w_hist <= past
    total = past + ts
    n_cmp = total // CMP_STRIDE - 1
    n_sel_blocks = -(-total // SEL_BLOCK)
    kcv_t = _compress_paged(_channel_major(cache_c), page_table, *cmp_consts)
    m01 = _sel_matrix(_round_up(n_sel_blocks, LANES), kcv_t.shape[2])
    q4 = q_tm.reshape(ts, bsz, KV_HEADS, GROUP, HEAD_DIM).transpose(1, 2, 3, 0, 4).reshape(
        bsz, KV_HEADS, GROUP * ts, HEAD_DIM)
    g5 = gt_tm.reshape(ts, bsz, KV_HEADS, LANES)[..., :3 * GROUP].reshape(ts, bsz, KV_HEADS, 3, GROUP)
    g5 = g5.transpose(1, 2, 4, 0, 3).reshape(bsz, KV_HEADS, GROUP * ts, 3)
    gt4 = jnp.pad(g5, ((0, 0), (0, 0), (0, 0), (0, LANES - 3)))
    to_bt = lambda a: a.reshape(ts, bsz, KV_W).transpose(1, 0, 2)
    part, idx = _nsa_sample_a(q4, kcv_t, _channel_major(state_win), to_bt(kvw_tm), gt4, m01, past, ts, n_cmp)
    to_rows = lambda a: a.reshape(bsz, KV_HEADS, GROUP, ts, a.shape[-1]).transpose(0, 1, 3, 2, 4).reshape(
        bsz * KV_HEADS * ts, GROUP, a.shape[-1])
    slabs = _channel_major(cache_s).reshape(n_phys, 2, KV_HEADS, HEAD_DIM, page)
    idx16 = idx[:, :, :N_SEL]
    pages = jnp.take_along_axis(page_table[:, None, :],
                                jnp.minimum(idx16, past // SEL_BLOCK - 1) // (page // SEL_BLOCK), axis=2)
    o_rows = _nsa_sample_b(idx16.reshape(-1), pages.reshape(-1), slabs,
                           to_rows(q4), to_bt(kvs_tm), to_rows(gt4), to_rows(part), past, ts, page)
    return o_rows.reshape(bsz, KV_HEADS, ts, GROUP, HEAD_DIM).transpose(2, 0, 1, 3, 4).reshape(ts * bsz, W_ATTN)


def _mix_out_kernel(x_ref, a_ref, b_ref, *refs, gated):
    if gated:
        bg_ref, w_ref, g_ref, beta_ref, o_ref = refs
        b = (b_ref[...] * bg_ref[...]).astype(BF16)
    else:
        w_ref, g_ref, beta_ref, o_ref = refs
        b = b_ref[...]
    d = _dot(a_ref[...].astype(BF16), w_ref[0:W_CONV, :]) + _dot(b, w_ref[W_CONV:W_CONV + W_ATTN, :])
    o_ref[...] = _layer_norm(DEEPNORM_ALPHA * x_ref[...] + d, g_ref[...], beta_ref[...])


def _mix_out(x2d, a2d, b2d, bg2d, w_out, ln_g, ln_b, tm):
    n = x2d.shape[0]
    row = lambda w: pl.BlockSpec((tm, w), lambda i: (i, 0))
    vec = pl.BlockSpec((1, D_MODEL), lambda i: (0, 0))
    gate = [] if bg2d is None else [bg2d]
    return pl.pallas_call(
        functools.partial(_mix_out_kernel, gated=bg2d is not None),
        grid=(n // tm,),
        in_specs=[row(D_MODEL), row(W_CONV), row(W_ATTN)] + [row(W_ATTN)] * len(gate) + [
            pl.BlockSpec((W_CONV + W_ATTN, D_MODEL), lambda i: (0, 0)), vec, vec],
        out_specs=row(D_MODEL),
        out_shape=jax.ShapeDtypeStruct((n, D_MODEL), F32),
        compiler_params=_cparams(("parallel",)),
        name="mix_out",
    )(x2d, a2d, b2d, *gate, w_out, ln_g, ln_b)


POOL_HALO = 16


def _pool_tail(x, d_groups, gate, wg_ref, sc_ref, wo_ref, g_ref, beta_ref):
    mixed = jnp.concatenate([_dot(d_groups[gi].astype(BF16), wg_ref[gi]) for gi in range(len(POOL_WINDOWS))], axis=1)
    h = (mixed * sc_ref[...] * _silu(gate)).astype(BF16)
    return _layer_norm(DEEPNORM_ALPHA * x + _dot(h, wo_ref[...]), g_ref[...], beta_ref[...])


def _pool_prompt_kernel(x0_ref, a_ref, b_ref, we_ref, g0_ref, beta0_ref, wi_ref, wg_ref, sc_ref, wo_ref, g_ref,
                        beta_ref, o_ref, tail_ref, ext_ref, *, tm):
    i = pl.program_id(1)

    @pl.when(i == 0)
    def _():
        ext_ref[0:POOL_HALO, :] = jnp.zeros((POOL_HALO, W_POOL), F32)

    d0 = _dot(a_ref[...], we_ref[0:W_CONV, :]) + _dot(b_ref[...], we_ref[W_CONV:W_CONV + W_ATTN, :])
    x = _layer_norm(DEEPNORM_ALPHA * x0_ref[...] + d0, g0_ref[...], beta0_ref[...])
    xb = x.astype(BF16)
    v = _dot(xb, wi_ref[:, 0:W_POOL])
    gate = _dot(xb, wi_ref[:, W_POOL:2 * W_POOL])
    ext_ref[POOL_HALO:POOL_HALO + tm, :] = v
    pos = i * tm + lax.broadcasted_iota(jnp.int32, (tm, 1), 0)
    d_groups = []
    for gi, w in enumerate(POOL_WINDOWS):
        c = slice(gi * POOL_GROUP_W, (gi + 1) * POOL_GROUP_W)
        win_sum = v[:, c]
        for k in range(1, w):
            win_sum = win_sum + ext_ref[POOL_HALO - k:POOL_HALO - k + tm, c]
        cnt = jnp.minimum(pos + 1, w).astype(F32)
        d_groups.append(win_sum / cnt - v[:, c])
    o_ref[...] = _pool_tail(x, d_groups, gate, wg_ref, sc_ref, wo_ref, g_ref, beta_ref)
    ext_ref[0:POOL_HALO, :] = ext_ref[tm:tm + POOL_HALO, :]
    tail_ref[0] = ext_ref[0:POOL_HALO, :]


def _pool_prompt(x2d, a2d, b2d, w_oe, ln_g0, ln_b0, w_in, w_grp, scale, w_out, ln_g, ln_b, batch, seq, tm):
    nt = seq // tm
    const = lambda a: pl.BlockSpec(a.shape, lambda b, i: (0,) * a.ndim)
    rows = lambda w: pl.BlockSpec((tm, w), lambda b, i: (b * nt + i, 0))
    row = rows(D_MODEL)
    return pl.pallas_call(
        functools.partial(_pool_prompt_kernel, tm=tm),
        grid=(batch, nt),
        in_specs=[row, rows(W_CONV), rows(W_ATTN), const(w_oe), const(ln_g0), const(ln_b0),
                  const(w_in), const(w_grp), const(scale), const(w_out), const(ln_g), const(ln_b)],
        out_specs=[row, pl.BlockSpec((1, POOL_HALO, W_POOL), lambda b, i: (b, 0, 0))],
        out_shape=[jax.ShapeDtypeStruct((batch * seq, D_MODEL), F32),
                   jax.ShapeDtypeStruct((batch, POOL_HALO, W_POOL), F32)],
        scratch_shapes=[pltpu.VMEM((tm + POOL_HALO, W_POOL), F32)],
        compiler_params=_cparams(("arbitrary", "arbitrary")),
        name="pool_prompt",
    )(x2d, a2d, b2d, w_oe, ln_g0, ln_b0, w_in, w_grp, scale, w_out, ln_g, ln_b)


def _pool_sample_kernel(x_ref, st_ref, wi_ref, wg_ref, sc_ref, wo_ref, g_ref, beta_ref, o_ref, v_ref, *,
                        ts, bsz, pos0):
    x = x_ref[...]
    xb = x.astype(BF16)
    v = _dot(xb, wi_ref[:, 0:W_POOL])
    gate = _dot(xb, wi_ref[:, W_POOL:2 * W_POOL])
    v_ref[...] = v
    n_hist = st_ref.shape[0]

    def ext(r):
        return st_ref[r] if r < n_hist else v[(r - n_hist) * bsz:(r - n_hist + 1) * bsz, :]

    d_groups = []
    for gi, w in enumerate(POOL_WINDOWS):
        c = slice(gi * POOL_GROUP_W, (gi + 1) * POOL_GROUP_W)
        per_t = []
        for t in range(ts):
            win_sum = ext(n_hist + t)[:, c]
            for k in range(1, w):
                win_sum = win_sum + ext(n_hist + t - k)[:, c]
            per_t.append(win_sum / float(min(pos0 + t + 1, w)) - ext(n_hist + t)[:, c])
        d_groups.append(jnp.concatenate(per_t, axis=0))
    o_ref[...] = _pool_tail(x, d_groups, gate, wg_ref, sc_ref, wo_ref, g_ref, beta_ref)


def _pool_sample(x_tm, state_tm, w_in, w_grp, scale, w_out, ln_g, ln_b, ts, bsz, pos0):
    return pl.pallas_call(
        functools.partial(_pool_sample_kernel, ts=ts, bsz=bsz, pos0=pos0),
        out_shape=[jax.ShapeDtypeStruct((ts * bsz, D_MODEL), F32), jax.ShapeDtypeStruct((ts * bsz, W_POOL), F32)],
        compiler_params=pltpu.CompilerParams(vmem_limit_bytes=VMEM_LIMIT),
        name="pool_sample",
    )(x_tm, state_tm, w_in, w_grp, scale, w_out, ln_g, ln_b)


def kernel(x_prompt, x_sample, cache_kv_cmp, cache_kv_sel, state_win_kv, state_conv, state_pool, page_table,
           w_in_even, w_cmp, conv_w, conv_b, conv_ln_g, conv_ln_b, w_out_even,
           w_in_odd, w_pool_grp, pool_scale, w_out_odd, ln_g, ln_b):
    batch, seq, _ = x_prompt.shape
    bsz, ts, _ = x_sample.shape
    past = page_table.shape[1] * cache_kv_cmp.shape[2]
    kv_shape = (2, KV_HEADS, HEAD_DIM)
    vec = lambda a: a.reshape(1, -1)
    assert seq >= CONV_WIDTH - 1 and seq >= POOL_MAX - 1 and seq >= WINDOW + Q_TILE and seq % K_TILE == 0

    def token_major(x_t):
        n, _, rows = x_t.shape
        return x_t.reshape((n,) + kv_shape + (rows,)).transpose(0, 4, 1, 2, 3)[None]

    w_pad = _pad_even_weights(w_in_even[0])
    cmp_lanes = min(CMP_LANES, seq, past)
    cmp_consts = _compress_consts(w_cmp[0].reshape(CMP_BLOCK, KV_W), cmp_lanes)
    w_oe = w_out_even[0].astype(BF16)
    cw, cb, cg, cbeta = conv_w[0], vec(conv_b[0]), vec(conv_ln_g[0]), vec(conv_ln_b[0])
    g0, b0 = vec(ln_g[0]), vec(ln_b[0])

    xp = x_prompt.reshape(batch * seq, D_MODEL)
    u, sg, q, bg, gt, kvc_t, kvs_t, kvw_t, kaug, vaug, kw, vwa = _proj_prompt(xp, w_pad, batch, seq, 256)
    a_out = _conv_prompt(u, sg, cw, cb, cg, cbeta, batch, seq, 256)
    n_chunks = seq // CMP_STRIDE
    n_sel = -(-seq // SEL_BLOCK)
    kcv = _compress_prompt(kvc_t, *cmp_consts)
    b_out = _nsa_prompt(q, kaug, vaug, kw, vwa, kcv.reshape(batch, 2 * KV_HEADS, HEAD_DIM, n_chunks), gt, bg,
                        _sel_matrix(n_sel, n_chunks), batch, seq, n_chunks - 1, n_sel)

    n_win_p = min(WINDOW, seq)
    kvc_p = token_major(kvc_t)
    kvs_p = token_major(kvs_t)
    win_p = token_major(kvw_t[:, :, seq - n_win_p:])
    conv_p = u.reshape(batch, seq, W_CONV)[:, seq - (CONV_WIDTH - 1):][None]

    xs = x_sample.transpose(1, 0, 2).reshape(ts * bsz, D_MODEL)
    us, sgs, qs, bgs, gts, kvcs, kvss, kvws = _proj_sample(xs, w_pad)
    conv_ext = jnp.concatenate([state_conv[0].transpose(1, 0, 2), us.reshape(ts, bsz, W_CONV)], axis=0)
    a_out_s = _conv_sample(conv_ext, sgs.reshape(ts, bsz, W_CONV), cw, cb, cg, cbeta).reshape(ts * bsz, W_CONV)
    b_raw_s = _sample_attention(qs, kvss, kvws, gts, cache_kv_cmp[0], cache_kv_sel[0], state_win_kv[0], page_table,
                                cmp_consts, bsz, ts)
    xs1 = _mix_out(xs, a_out_s, b_raw_s, bgs, w_oe, g0, b0, ts * bsz)

    to_bt = lambda a: a.reshape(ts, bsz, -1).transpose(1, 0, 2)
    kvc_s = to_bt(kvcs).reshape((1, bsz, ts) + kv_shape)
    kvs_s = to_bt(kvss).reshape((1, bsz, ts) + kv_shape)
    win_ext = jnp.concatenate([state_win_kv[0], to_bt(kvws).reshape((bsz, ts) + kv_shape)], axis=1)
    w_len = win_ext.shape[1]
    win_s = win_ext[:, w_len - min(WINDOW, w_len):][None]
    conv_s = conv_ext[-(CONV_WIDTH - 1):].transpose(1, 0, 2)[None]

    wi_o = w_in_odd[0].astype(BF16)
    wg_o = w_pool_grp[0].astype(BF16)
    wo_o = w_out_odd[0].astype(BF16)
    sc_o = vec(pool_scale[0])
    g1, b1 = vec(ln_g[1]), vec(ln_b[1])
    n_keep = POOL_MAX - 1
    yp, tail_p = _pool_prompt(xp, a_out, b_out, w_oe, g0, b0, wi_o, wg_o, sc_o, wo_o, g1, b1, batch, seq, 256)
    pool_p = tail_p[:, POOL_HALO - n_keep:][None]
    st_tm = state_pool[0].transpose(1, 0, 2)
    ys, vs_new = _pool_sample(xs1, st_tm, wi_o, wg_o, sc_o, wo_o, g1, b1, ts, bsz, past)
    pool_ext = jnp.concatenate([st_tm, vs_new.reshape(ts, bsz, W_POOL)], axis=0)
    pool_s = pool_ext[-n_keep:].transpose(1, 0, 2)[None]

    y_prompt = yp.reshape(batch, seq, D_MODEL)
    y_sample = ys.reshape(ts, bsz, D_MODEL).transpose(1, 0, 2)
    return (y_prompt, y_sample, kvc_p, kvs_p, win_p, conv_p, pool_p, kvc_s, kvs_s, win_s, conv_s, pool_s)
```
